```python
import math
import jax, jax.numpy as jnp
from jax import lax
import numpy as np

D_MODEL = 1024
BATCH = 4
SEQ = 8192
DEPTH = 2

GRID_W = 64
CTX_LEN = 256
MIX_WIDTH = D_MODEL
F_GROUPS = 4
F_CH = MIX_WIDTH // 2 // F_GROUPS
F_WIDTH = F_GROUPS * F_CH
H_DIFF = 4
DH = MIX_WIDTH // 2 // H_DIFF // 2
DV = 2 * DH
QK_WIDTH = H_DIFF * 2 * DH
ATT_WIDTH = H_DIFF * DV
F0 = 0
Q0 = F0 + F_WIDTH
K0 = Q0 + QK_WIDTH
V0 = K0 + QK_WIDTH
G0 = V0 + ATT_WIDTH
EVEN_IN = G0 + MIX_WIDTH
CONV_WIDTH = MIX_WIDTH
CONV_K = 3
ODD_IN = 4 * CONV_WIDTH
N_EVEN = (DEPTH + 1) // 2
N_ODD = DEPTH // 2
ROPE_BASE = 10000.0
Q_BLOCK = 128
EPS = 1e-6
ATTN_SCALE = DH ** -0.5

kernel_name = "hybrid_fourier_diffattn_shortconv_dit"


def rms_norm(x, g):
    x32 = x.astype(jnp.float32)
    y = x32 * lax.rsqrt(jnp.mean(x32 * x32, axis=-1, keepdims=True) + EPS)
    return (y * g.astype(jnp.float32)).astype(x.dtype)


def ada_params(cvec, w, b):
    m = jax.nn.silu(cvec) @ w + b
    return jnp.split(m, 3, axis=-1)


def modulate(x, g, shift, scale):
    return rms_norm(x, g) * (1.0 + scale) + shift


def need_ctx_after(i):
    return any(j % 2 == 0 for j in range(i + 1, DEPTH))


def lambda_init_fn(layer_idx):
    return 0.8 - 0.6 * math.exp(-0.3 * layer_idx)


def axial_rope_tables(rows):
    pos_r = jnp.repeat(jnp.arange(rows, dtype=jnp.float32), GRID_W)
    pos_c = jnp.tile(jnp.arange(GRID_W, dtype=jnp.float32), rows)
    half = DH // 2
    freqs = ROPE_BASE ** (-jnp.arange(0, half, 2, dtype=jnp.float32) / half)
    ang_r = pos_r[:, None] * freqs
    ang_c = pos_c[:, None] * freqs
    return (jnp.cos(ang_r), jnp.sin(ang_r), jnp.cos(ang_c), jnp.sin(ang_c))


def _rotate(x, cos, sin):
    x1, x2 = jnp.split(x, 2, axis=-1)
    return jnp.concatenate([x1 * cos - x2 * sin, x1 * sin + x2 * cos], axis=-1)


def apply_axial_rope(x, tabs):
    cos_r, sin_r, cos_c, sin_c = [t[None, :, None, None, :].astype(x.dtype) for t in tabs]
    xr, xc = jnp.split(x, 2, axis=-1)
    return jnp.concatenate([_rotate(xr, cos_r, sin_r), _rotate(xc, cos_c, sin_c)], axis=-1)


def fourier_mix(u):
    b, n, _ = u.shape
    u32 = u.reshape(b, n, F_GROUPS, F_CH).astype(jnp.float32)
    f = jnp.fft.fftn(u32, axes=(1, 3), norm="ortho").real
    return f.reshape(b, n, F_WIDTH).astype(u.dtype)


def split_kv(kv, k_g):
    b, n, _ = kv.shape
    k = rms_norm(kv[..., :QK_WIDTH].reshape(b, n, H_DIFF, 2, DH), k_g)
    v = kv[..., QK_WIDTH:].reshape(b, n, H_DIFF, DV)
    return k, v


def split_qkv(proj, q_g, k_g):
    b, n, _ = proj.shape
    q = rms_norm(proj[..., Q0:K0].reshape(b, n, H_DIFF, 2, DH), q_g)
    k, v = split_kv(proj[..., K0:G0], k_g)
    return q, k, v


def diff_attention(q, k, v, lam):
    b, n = q.shape[:2]
    nblk = n // Q_BLOCK
    qb = q.reshape(b, nblk, Q_BLOCK, H_DIFF, 2, DH).swapaxes(0, 1)

    def one_block(qblk):
        s = jnp.einsum('bqhmd,bkhmd->bhmqk', qblk, k,
                       preferred_element_type=jnp.float32) * ATTN_SCALE
        p = jax.nn.softmax(s, axis=-1)
        a = p[:, :, 0] - lam * p[:, :, 1]
        return jnp.einsum('bhqk,bkhv->bqhv', a.astype(v.dtype), v)

    o = lax.map(one_block, qb)
    return o.swapaxes(0, 1).reshape(b, n, H_DIFF, DV)


def even_output(proj, attn_o, subln_g, lam_init, w_out):
    b, n, _ = proj.shape
    fo = fourier_mix(proj[..., F0:Q0])
    ao = (rms_norm(attn_o, subln_g) * (1.0 - lam_init)).reshape(b, n, ATT_WIDTH)
    y = jnp.concatenate([fo, ao], axis=-1) * jax.nn.silu(proj[..., G0:])
    return y @ w_out


def short_conv_mixer(proj, conv_w, w_out):
    bg, cg, xt, g = jnp.split(proj, 4, axis=-1)
    u = cg * xt
    n = u.shape[1]
    pad = CONV_K // 2
    up = jnp.pad(u, ((0, 0), (pad, pad), (0, 0)))
    conv = sum(conv_w[t] * up[:, t:t + n] for t in range(CONV_K))
    return (bg * conv * jax.nn.silu(g)) @ w_out


def setup_inputs(seed: int = 0) -> dict:
    key = jax.random.key(seed)
    ks = jax.random.split(key, 20)
    f32 = jnp.float32
    D = D_MODEL
    nrm = lambda k, shape, s: (jax.random.normal(k, shape, f32) * s)
    return {
        "x": nrm(ks[0], (BATCH, SEQ, D), 1.0),
        "c": nrm(ks[1], (BATCH, D), 1.0),
        "ctx": nrm(ks[2], (BATCH, CTX_LEN, D), 1.0),
        "c_ctx": nrm(ks[3], (D,), 1.0),
        "norm_g": 1.0 + nrm(ks[4], (DEPTH, D), 0.02),
        "ada_w": nrm(ks[5], (DEPTH, D, 3 * D), D ** -0.5),
        "ada_b": nrm(ks[6], (DEPTH, 3 * D), 0.01),
        "even_w_in": nrm(ks[7], (N_EVEN, D, EVEN_IN), D ** -0.5),
        "even_q_norm": 1.0 + nrm(ks[8], (N_EVEN, DH), 0.02),
        "even_k_norm": 1.0 + nrm(ks[9], (N_EVEN, DH), 0.02),
        "even_lambda_q1": nrm(ks[10], (N_EVEN, DH), 0.1),
        "even_lambda_k1": nrm(ks[11], (N_EVEN, DH), 0.1),
        "even_lambda_q2": nrm(ks[12], (N_EVEN, DH), 0.1),
        "even_lambda_k2": nrm(ks[13], (N_EVEN, DH), 0.1),
        "even_subln": 1.0 + nrm(ks[14], (N_EVEN, DV), 0.02),
        "even_w_out": nrm(ks[15], (N_EVEN, MIX_WIDTH, D), MIX_WIDTH ** -0.5),
        "odd_w_in": nrm(ks[16], (N_ODD, D, ODD_IN), D ** -0.5),
        "odd_conv_w": nrm(ks[17], (N_ODD, CONV_K, CONV_WIDTH), CONV_K ** -0.5),
        "odd_w_out": nrm(ks[18], (N_ODD, CONV_WIDTH, D), CONV_WIDTH ** -0.5),
    }


def reference(x, c, ctx, c_ctx, norm_g, ada_w, ada_b, even_w_in, even_q_norm, even_k_norm,
              even_lambda_q1, even_lambda_k1, even_lambda_q2, even_lambda_k2, even_subln,
              even_w_out, odd_w_in, odd_conv_w, odd_w_out):
    n = x.shape[1]
    ROWS = n // GRID_W
    tabs = axial_rope_tables(ROWS)
    x_lat = x
    x_ctx = ctx
    for i in range(DEPTH):
        ctx_out = need_ctx_after(i)
        shift, scale, gate = ada_params(c, ada_w[i], ada_b[i])
        shift_c, scale_c, gate_c = ada_params(c_ctx, ada_w[i], ada_b[i])
        h = modulate(x_lat, norm_g[i], shift[:, None, :], scale[:, None, :])
        if i % 2 == 0:
            e = i // 2
            w_in = even_w_in[e]
            lam_init = lambda_init_fn(i)
            lam = (jnp.exp(jnp.sum(even_lambda_q1[e].astype(jnp.float32) * even_lambda_k1[e].astype(jnp.float32)))
                   - jnp.exp(jnp.sum(even_lambda_q2[e].astype(jnp.float32) * even_lambda_k2[e].astype(jnp.float32)))
                   + lam_init)
            hc = modulate(x_ctx, norm_g[i], shift_c, scale_c)
            if ctx_out:
                proj_c = hc @ w_in
                qc, kc, vc = split_qkv(proj_c, even_q_norm[e], even_k_norm[e])
                oc = diff_attention(qc, kc, vc, lam)
                yc = even_output(proj_c, oc, even_subln[e], lam_init, even_w_out[e])
            else:
                kc, vc = split_kv(hc @ w_in[:, K0:G0], even_k_norm[e])
            proj = h @ w_in
            q, k, v = split_qkv(proj, even_q_norm[e], even_k_norm[e])
            q = apply_axial_rope(q, tabs)
            k = apply_axial_rope(k, tabs)
            k_all = jnp.concatenate([kc, k], axis=1)
            v_all = jnp.concatenate([vc, v], axis=1)
            o = diff_attention(q, k_all, v_all, lam)
            y = even_output(proj, o, even_subln[e], lam_init, even_w_out[e])
        else:
            o_i = i // 2
            y = short_conv_mixer(h @ odd_w_in[o_i], odd_conv_w[o_i], odd_w_out[o_i])
            if ctx_out:
                hc = modulate(x_ctx, norm_g[i], shift_c, scale_c)
                yc = short_conv_mixer(hc @ odd_w_in[o_i], odd_conv_w[o_i], odd_w_out[o_i])
        x_lat = x_lat + gate[:, None, :] * y
        if ctx_out:
            x_ctx = x_ctx + gate_c * yc
    return x_lat
```

```python
import functools
import math

import numpy as np
import jax
import jax.numpy as jnp
from jax import lax
from jax.experimental import pallas as pl
from jax.experimental.pallas import tpu as pltpu

F32 = jnp.float32
BF16 = jnp.bfloat16

D_MODEL = 1024
GRID_W = 64
F_GROUPS = 4
F_CH = 128
F_WIDTH = F_GROUPS * F_CH
H_DIFF = 4
DH = 64
DV = 128
QK_WIDTH = H_DIFF * 2 * DH
ATT_WIDTH = H_DIFF * DV
CONV_K = 3
ROPE_BASE = 10000.0
EPS = 1e-6
ATTN_SCALE = DH ** -0.5
LOG2E = 1.4426950408889634

LANES = 128
MXU_DIM = 256
VMEM_LIMIT = 56 * 1024 * 1024

ROW_BLOCK = 512
FFT_N1 = 128
FFT_KB = 8
Q_BLOCK_ATT = 256
NEG_BIG = -1e30


def _dot(a, b):
    return jnp.dot(a, b, preferred_element_type=F32)


def _dot_nt(a, b):
    return lax.dot_general(a, b, (((1,), (1,)), ((), ())), preferred_element_type=F32)


def _sigmoid(x):
    return 1.0 / (1.0 + jnp.exp(-x))


def _params(sem):
    return pltpu.CompilerParams(dimension_semantics=sem, vmem_limit_bytes=VMEM_LIMIT)


def _mod_kernel(c_ref, w_ref, b_ref, o_ref):
    cv = c_ref[...]
    a = cv * _sigmoid(cv)
    w = w_ref[0]
    a_hi = a.astype(BF16)
    a_lo = (a - a_hi.astype(F32)).astype(BF16)
    w_hi = w.astype(BF16)
    w_lo = (w - w_hi.astype(F32)).astype(BF16)
    acc = _dot(a_hi, w_hi) + _dot(a_hi, w_lo) + _dot(a_lo, w_hi)
    o_ref[0] = acc + b_ref[0]


def _mod_params(cvec, ada_w, ada_b):
    depth, d, d3 = ada_w.shape
    tn = 1024
    return pl.pallas_call(
        _mod_kernel,
        grid=(depth, d3 // tn),
        in_specs=[
            pl.BlockSpec((8, d), lambda l, j: (0, 0)),
            pl.BlockSpec((1, d, tn), lambda l, j: (l, 0, j)),
            pl.BlockSpec((1, 1, tn), lambda l, j: (l, 0, j)),
        ],
        out_specs=pl.BlockSpec((1, 8, tn), lambda l, j: (l, 0, j)),
        out_shape=jax.ShapeDtypeStruct((depth, 8, d3), F32),
        compiler_params=_params(("parallel", "parallel")),
        name="mod_params",
    )(cvec, ada_w, ada_b.reshape(depth, 1, d3))


def _modulated(x, g, shift, scale):
    ms = jnp.mean(x * x, axis=-1, keepdims=True)
    y = x * lax.rsqrt(ms + EPS) * g
    return y * (1.0 + scale) + shift


def _group_norm(t, gain, bd):
    sq = (t * t).astype(BF16)
    half = MXU_DIM
    ss = jnp.concatenate([_dot(sq[:, :half], bd), _dot(sq[:, half:], bd)], axis=-1)
    return t * lax.rsqrt(ss * (1.0 / DH) + EPS) * gain


def _rope(t, cos, sin):
    width = t.shape[-1]
    quarter = DH // 4
    lane = lax.broadcasted_iota(jnp.int32, t.shape, 1)
    from_above = pltpu.roll(t, width - quarter, axis=1)
    from_below = pltpu.roll(t, quarter, axis=1)
    partner = jnp.where((lane & (2 * quarter - 1)) < quarter, from_above, from_below)
    reps = width // cos.shape[-1]
    cosf = jnp.concatenate([cos] * reps, axis=-1)
    sinf = jnp.concatenate([sin] * reps, axis=-1)
    return t * cosf + partner * sinf


def _inproj_kernel(x_ref, shift_ref, scale_ref, g_ref, cos_ref, sin_ref, qg_ref, kg_ref, bd_ref,
                   wf_ref, wq_ref, wk_ref, wv_ref, wg_ref,
                   u_ref, q_ref, k_ref, vt_ref, gs_ref):
    h = _modulated(x_ref[...], g_ref[...], shift_ref[0], scale_ref[0]).astype(BF16)
    cos = cos_ref[...]
    sin = sin_ref[...]
    bd = bd_ref[...]
    u_ref[...] = _dot(h, wf_ref[...]).astype(BF16)
    q = _rope(_group_norm(_dot(h, wq_ref[...]), qg_ref[...], bd), cos, sin)
    q_ref[...] = q.astype(BF16)
    k = _rope(_group_norm(_dot(h, wk_ref[...]), kg_ref[...], bd), cos, sin)
    k_ref[...] = k.astype(BF16)
    v = _dot(h, wv_ref[...])
    for hh in range(H_DIFF):
        vt_ref[0, hh] = v[:, hh * DV:(hh + 1) * DV].T.astype(BF16)
    g = _dot(h, wg_ref[...])
    gs_ref[...] = (g * _sigmoid(g)).astype(BF16)


def _ctx_kv_kernel(x_ref, shift_ref, scale_ref, g_ref, kg_ref, bd_ref, wk_ref, wv_ref, k_ref, vt_ref):
    h = _modulated(x_ref[...], g_ref[...], shift_ref[...], scale_ref[...]).astype(BF16)
    k = _group_norm(_dot(h, wk_ref[...]), kg_ref[...], bd_ref[...])
    k_ref[...] = k.astype(BF16)
    v = _dot(h, wv_ref[...])
    for hh in range(H_DIFF):
        vt_ref[0, hh] = v[:, hh * DV:(hh + 1) * DV].T.astype(BF16)


def _const_spec(shape):
    zeros = (0,) * len(shape)
    return pl.BlockSpec(shape, lambda *_: zeros)


def _inproj0(x2, shift, scale, g, cos, sin, qg, kg, bd, wf, wq, wk, wv, wg, batch, n):
    rows, d = x2.shape
    tm = ROW_BLOCK
    bpb = n // tm
    row_spec = lambda w: pl.BlockSpec((tm, w), lambda i: (i, 0))
    return pl.pallas_call(
        _inproj_kernel,
        grid=(rows // tm,),
        in_specs=[
            row_spec(d),
            pl.BlockSpec((1, 1, d), lambda i: (i // bpb, 0, 0)),
            pl.BlockSpec((1, 1, d), lambda i: (i // bpb, 0, 0)),
            _const_spec((1, d)),
            pl.BlockSpec((tm, LANES), lambda i: (i % bpb, 0)),
            pl.BlockSpec((tm, LANES), lambda i: (i % bpb, 0)),
            _const_spec((1, QK_WIDTH)),
            _const_spec((1, QK_WIDTH)),
            _const_spec((MXU_DIM, MXU_DIM)),
            _const_spec(wf.shape), _const_spec(wq.shape), _const_spec(wk.shape),
            _const_spec(wv.shape), _const_spec(wg.shape),
        ],
        out_specs=[
            row_spec(F_WIDTH), row_spec(QK_WIDTH), row_spec(QK_WIDTH),
            pl.BlockSpec((1, H_DIFF, DV, tm), lambda i: (i // bpb, 0, 0, i % bpb)),
            row_spec(d),
        ],
        out_shape=[
            jax.ShapeDtypeStruct((rows, F_WIDTH), BF16),
            jax.ShapeDtypeStruct((rows, QK_WIDTH), BF16),
            jax.ShapeDtypeStruct((rows, QK_WIDTH), BF16),
            jax.ShapeDtypeStruct((batch, H_DIFF, DV, n), BF16),
            jax.ShapeDtypeStruct((rows, d), BF16),
        ],
        compiler_params=_params(("parallel",)),
        name="inproj0",
    )(x2, shift, scale, g, cos, sin, qg, kg, bd, wf, wq, wk, wv, wg)


def _ctx_kv(ctx2, shift, scale, g, kg, bd, wk, wv, batch, m_ctx):
    rows, d = ctx2.shape
    return pl.pallas_call(
        _ctx_kv_kernel,
        grid=(batch,),
        in_specs=[
            pl.BlockSpec((m_ctx, d), lambda b: (b, 0)),
            _const_spec((1, d)), _const_spec((1, d)), _const_spec((1, d)),
            _const_spec((1, QK_WIDTH)), _const_spec((MXU_DIM, MXU_DIM)),
            _const_spec(wk.shape), _const_spec(wv.shape),
        ],
        out_specs=[
            pl.BlockSpec((m_ctx, QK_WIDTH), lambda b: (b, 0)),
            pl.BlockSpec((1, H_DIFF, DV, m_ctx), lambda b: (b, 0, 0, 0)),
        ],
        out_shape=[
            jax.ShapeDtypeStruct((rows, QK_WIDTH), BF16),
            jax.ShapeDtypeStruct((batch, H_DIFF, DV, m_ctx), BF16),
        ],
        compiler_params=_params(("parallel",)),
        name="ctx_kv",
    )(ctx2, shift, scale, g, kg, bd, wk, wv)


def _attn_kernel(lam_ref, q_ref, k_ref, vt_ref, sg_ref, o_ref, acc1_ref, acc2_ref, *, tk, lam_init):
    tq = q_ref.shape[1]
    m_keys = k_ref.shape[1]
    q = q_ref[0]
    lane = lax.broadcasted_iota(jnp.int32, q.shape, 1)
    zero = jnp.zeros_like(q)
    q1 = jnp.where(lane < DH, q, zero)
    q2 = jnp.where(lane >= DH, q, zero)
    acc1_ref[...] = jnp.zeros_like(acc1_ref)
    acc2_ref[...] = jnp.zeros_like(acc2_ref)

    def one_map(kc, vc, qz, m_old, l_old, acc_ref):
        s = _dot_nt(kc, qz)
        m_new = jnp.maximum(m_old, jnp.max(s, axis=0, keepdims=True))
        alpha = jnp.exp2(m_old - m_new)
        p = jnp.exp2(s - m_new)
        l_new = alpha * l_old + jnp.sum(p, axis=0, keepdims=True)
        acc_ref[...] = alpha * acc_ref[...] + _dot(vc, p.astype(BF16))
        return m_new, l_new

    def body(c, carry):
        m1, l1, m2, l2 = carry
        start = pl.multiple_of(c * tk, tk)
        kc = k_ref[0, pl.ds(start, tk), :]
        vc = vt_ref[0, 0, :, pl.ds(start, tk)]
        m1, l1 = one_map(kc, vc, q1, m1, l1, acc1_ref)
        m2, l2 = one_map(kc, vc, q2, m2, l2, acc2_ref)
        return m1, l1, m2, l2

    neg = jnp.full((1, tq), NEG_BIG, F32)
    zer = jnp.zeros((1, tq), F32)
    _, l1, _, l2 = lax.fori_loop(0, m_keys // tk, body, (neg, zer, neg, zer))

    lp = lam_ref[...]
    lam = (jnp.exp(jnp.sum(lp[0:1] * lp[1:2], axis=-1, keepdims=True))
           - jnp.exp(jnp.sum(lp[2:3] * lp[3:4], axis=-1, keepdims=True)) + lam_init)
    ot = acc1_ref[...] / l1 - lam * (acc2_ref[...] / l2)
    o = ot.T
    ms = jnp.mean(o * o, axis=-1, keepdims=True)
    o = o * lax.rsqrt(ms + EPS) * sg_ref[...] * (1.0 - lam_init)
    o_ref[0] = o.astype(BF16)


def _pick_key_chunk(m_keys):
    best = LANES
    for cand in range(LANES, 1024 + 1, LANES):
        if m_keys % cand == 0:
            best = cand
    return best


def _diff_attention(lam_p, q, k_all, vt_all, sg, lam_init):
    batch, n, _ = q.shape
    m_keys = k_all.shape[1]
    tq = Q_BLOCK_ATT
    tk = _pick_key_chunk(m_keys)
    kern = functools.partial(_attn_kernel, tk=tk, lam_init=lam_init)
    return pl.pallas_call(
        kern,
        grid=(batch, H_DIFF, n // tq),
        in_specs=[
            _const_spec((4, DH)),
            pl.BlockSpec((1, tq, 2 * DH), lambda b, h, i: (b, i, h)),
            pl.BlockSpec((1, m_keys, 2 * DH), lambda b, h, i: (b, 0, h)),
            pl.BlockSpec((1, 1, DV, m_keys), lambda b, h, i: (b, h, 0, 0)),
            _const_spec((1, DV)),
        ],
        out_specs=pl.BlockSpec((1, tq, DV), lambda b, h, i: (b, i, h)),
        out_shape=jax.ShapeDtypeStruct((batch, n, ATT_WIDTH), BF16),
        scratch_shapes=[pltpu.VMEM((DV, tq), F32), pltpu.VMEM((DV, tq), F32)],
        compiler_params=_params(("parallel", "parallel", "arbitrary")),
        name="diff_attn",
    )(lam_p, q, k_all, vt_all, sg)


def _fft_rows_kernel(f_ref, u_ref, y_ref):
    y_ref[0] = _dot(f_ref[...], u_ref[0]).astype(BF16)


def _fft_rows(f1cs, u_view):
    batch, n1, cols = u_view.shape
    ch = min(cols, 4096)
    return pl.pallas_call(
        _fft_rows_kernel,
        grid=(batch, cols // ch),
        in_specs=[
            _const_spec(f1cs.shape),
            pl.BlockSpec((1, n1, ch), lambda b, j: (b, 0, j)),
        ],
        out_specs=pl.BlockSpec((1, 2 * n1, ch), lambda b, j: (b, 0, j)),
        out_shape=jax.ShapeDtypeStruct((batch, 2 * n1, cols), BF16),
        compiler_params=_params(("parallel", "parallel")),
        name="fft_rows",
    )(f1cs, u_view)


def _fft_cols_kernel(g_ref, y_ref, bdc_ref, bds_ref, o_ref):
    _, _, kb, n2, width = y_ref.shape
    rows = kb * n2
    y = y_ref[0].reshape(2 * rows, width)
    z = _dot(g_ref[0], y)
    xr = z[:rows].astype(BF16)
    xi = z[rows:].astype(BF16)
    bdc = bdc_ref[...]
    bds = bds_ref[...]
    half = MXU_DIM
    o = jnp.concatenate(
        [_dot(xr[:, :half], bdc) + _dot(xi[:, :half], bds),
         _dot(xr[:, half:], bdc) + _dot(xi[:, half:], bds)], axis=-1)
    o_ref[0] = o.reshape(n2, kb, width)


def _fft_cols(gbig, y5, bdc, bds):
    batch, _, n1, n2, width = y5.shape
    kb = FFT_KB
    gr = gbig.shape[1]
    return pl.pallas_call(
        _fft_cols_kernel,
        grid=(n1 // kb, batch),
        in_specs=[
            pl.BlockSpec((1, gr, gr), lambda j, b: (j, 0, 0)),
            pl.BlockSpec((1, 2, kb, n2, width), lambda j, b: (b, 0, j, 0, 0)),
            _const_spec(bdc.shape), _const_spec(bds.shape),
        ],
        out_specs=pl.BlockSpec((1, n2, kb, width), lambda j, b: (b, 0, j, 0)),
        out_shape=jax.ShapeDtypeStruct((batch, n2, n1, width), F32),
        compiler_params=_params(("parallel", "arbitrary")),
        name="fft_cols",
    )(gbig, y5, bdc, bds)


@functools.lru_cache(maxsize=None)
def _fft_tables(n):
    n1 = FFT_N1
    n2 = n // n1
    kb = FFT_KB
    k1 = np.arange(n1)
    ang = 2.0 * np.pi * np.outer(k1, k1) / n1
    f1cs = np.concatenate([np.cos(ang), -np.sin(ang)], axis=0)
    scale = 1.0 / math.sqrt(n * F_CH)
    nblk = n1 // kb
    rows = n2 * kb
    gbig = np.zeros((nblk, 2, n2, kb, 2, kb, n2), np.float32)
    n2i = np.arange(n2)
    for j in range(kb):
        kk = (np.arange(nblk)[:, None] * kb + j) + n1 * n2i[None, :]
        a = 2.0 * np.pi * (kk[:, :, None] * n2i[None, None, :] % n) / n
        gr = np.cos(a) * scale
        gi = -np.sin(a) * scale
        gbig[:, 0, :, j, 0, j, :] = gr
        gbig[:, 0, :, j, 1, j, :] = -gi
        gbig[:, 1, :, j, 0, j, :] = gi
        gbig[:, 1, :, j, 1, j, :] = gr
    gbig = gbig.reshape(nblk, 2 * rows, 2 * rows)
    c = np.arange(F_CH)
    angc = 2.0 * np.pi * np.outer(c, c) / F_CH
    eye2 = np.eye(MXU_DIM // F_CH)
    bdc = np.kron(eye2, np.cos(angc))
    bds = np.kron(eye2, np.sin(angc))
    to_bf16 = lambda a: jnp.asarray(a, dtype=F32).astype(BF16)
    return to_bf16(f1cs), to_bf16(gbig), to_bf16(bdc), to_bf16(bds)


def _fourier_mix(u, batch, n):
    f1cs, gbig, bdc, bds = _fft_tables(n)
    n1 = FFT_N1
    n2 = n // n1
    y = _fft_rows(f1cs, u.reshape(batch, n1, n2 * F_WIDTH))
    fo = _fft_cols(gbig, y.reshape(batch, 2, n1, n2, F_WIDTH), bdc, bds)
    return fo.reshape(batch * n, F_WIDTH)


def _mid_kernel(x_ref, fo_ref, ao_ref, gs_ref, gate_ref, wo_ref, g1_ref, shift_ref, scale_ref, wi_ref,
                x1_ref, u_ref, bgs_ref):
    gs = gs_ref[...].astype(F32)
    yf = (fo_ref[...] * gs[:, :F_WIDTH]).astype(BF16)
    ya = (ao_ref[...].astype(F32) * gs[:, F_WIDTH:]).astype(BF16)
    y = _dot(yf, wo_ref[:F_WIDTH, :]) + _dot(ya, wo_ref[F_WIDTH:, :])
    x1 = x_ref[...] + gate_ref[0] * y
    x1_ref[...] = x1
    h = _modulated(x1, g1_ref[...], shift_ref[0], scale_ref[0]).astype(BF16)
    w = x1.shape[-1]
    bg = _dot(h, wi_ref[:, 0 * w:1 * w])
    gg = _dot(h, wi_ref[:, 3 * w:4 * w])
    bgs_ref[...] = (bg * (gg * _sigmoid(gg))).astype(BF16)
    cg = _dot(h, wi_ref[:, 1 * w:2 * w])
    xt = _dot(h, wi_ref[:, 2 * w:3 * w])
    u_ref[...] = (cg * xt).astype(BF16)


def _mid(x2, fo, ao, gs, gate0, wo, g1, shift1, scale1, wi, n):
    rows, d = x2.shape
    tm = ROW_BLOCK
    bpb = n // tm
    row_spec = lambda w: pl.BlockSpec((tm, w), lambda i: (i, 0))
    per_batch = pl.BlockSpec((1, 1, d), lambda i: (i // bpb, 0, 0))
    return pl.pallas_call(
        _mid_kernel,
        grid=(rows // tm,),
        in_specs=[
            row_spec(d), row_spec(F_WIDTH), row_spec(ATT_WIDTH), row_spec(d), per_batch,
            _const_spec(wo.shape), _const_spec((1, d)), per_batch, per_batch, _const_spec(wi.shape),
        ],
        out_specs=[row_spec(d), row_spec(d), row_spec(d)],
        out_shape=[
            jax.ShapeDtypeStruct((rows, d), F32),
            jax.ShapeDtypeStruct((rows, d), BF16),
            jax.ShapeDtypeStruct((rows, d), BF16),
        ],
        compiler_params=_params(("parallel",)),
        name="mid",
    )(x2, fo, ao, gs, gate0, wo, g1, shift1, scale1, wi)


def _conv_out_kernel(x1_ref, u_ref, up_ref, un_ref, bgs_ref, cw_ref, gate_ref, wo_ref, o_ref, *, bpb):
    i = pl.program_id(0)
    tm = u_ref.shape[0]
    halo = up_ref.shape[0]
    u = u_ref[...].astype(F32)
    first = (i % bpb) == 0
    last = (i % bpb) == (bpb - 1)
    prev_row = jnp.where(first, 0.0, up_ref[halo - 1:halo, :].astype(F32))
    next_row = jnp.where(last, 0.0, un_ref[0:1, :].astype(F32))
    row = lax.broadcasted_iota(jnp.int32, u.shape, 0)
    u_prev = jnp.where(row == 0, prev_row, pltpu.roll(u, 1, axis=0))
    u_next = jnp.where(row == tm - 1, next_row, pltpu.roll(u, tm - 1, axis=0))
    cw = cw_ref[...]
    conv = cw[0:1] * u_prev + cw[1:2] * u + cw[2:3] * u_next
    z = (conv * bgs_ref[...].astype(F32)).astype(BF16)
    o_ref[...] = x1_ref[...] + gate_ref[0] * _dot(z, wo_ref[...])


def _conv_out(x1, u, bgs, cw, gate1, wo, n):
    rows, d = x1.shape
    tm = ROW_BLOCK
    bpb = n // tm
    halo = 16
    hb = tm // halo
    nhalo = rows // halo
    row_spec = lambda w: pl.BlockSpec((tm, w), lambda i: (i, 0))
    kern = functools.partial(_conv_out_kernel, bpb=bpb)
    return pl.pallas_call(
        kern,
        grid=(rows // tm,),
        in_specs=[
            row_spec(d), row_spec(d),
            pl.BlockSpec((halo, d), lambda i: (jnp.maximum(i * hb - 1, 0), 0)),
            pl.BlockSpec((halo, d), lambda i: (jnp.minimum((i + 1) * hb, nhalo - 1), 0)),
            row_spec(d),
            _const_spec(cw.shape),
            pl.BlockSpec((1, 1, d), lambda i: (i // bpb, 0, 0)),
            _const_spec(wo.shape),
        ],
        out_specs=row_spec(d),
        out_shape=jax.ShapeDtypeStruct((rows, d), F32),
        compiler_params=_params(("parallel",)),
        name="conv_out",
    )(x1, u, u, u, bgs, cw, gate1, wo)


@functools.lru_cache(maxsize=None)
def _rope_tables(n):
    rows = n // GRID_W
    pos_r = np.repeat(np.arange(rows, dtype=np.float64), GRID_W)
    pos_c = np.tile(np.arange(GRID_W, dtype=np.float64), rows)
    half = DH // 2
    freqs = ROPE_BASE ** (-np.arange(0, half, 2, dtype=np.float64) / half)
    ang_r = pos_r[:, None] * freqs
    ang_c = pos_c[:, None] * freqs
    cos64 = np.concatenate([np.cos(ang_r), np.cos(ang_r), np.cos(ang_c), np.cos(ang_c)], axis=-1)
    sin64 = np.concatenate([-np.sin(ang_r), np.sin(ang_r), -np.sin(ang_c), np.sin(ang_c)], axis=-1)
    cos = np.concatenate([cos64, cos64], axis=-1).astype(np.float32)
    sin = np.concatenate([sin64, sin64], axis=-1).astype(np.float32)
    return jnp.asarray(cos), jnp.asarray(sin)


@functools.lru_cache(maxsize=None)
def _group_sum_matrix():
    return jnp.asarray(np.kron(np.eye(MXU_DIM // DH), np.ones((DH, DH))), dtype=BF16)


def _lambda_init(layer_idx):
    return 0.8 - 0.6 * math.exp(-0.3 * layer_idx)


def kernel(x, c, ctx, c_ctx, norm_g, ada_w, ada_b, even_w_in, even_q_norm, even_k_norm, even_lambda_q1,
           even_lambda_k1, even_lambda_q2, even_lambda_k2, even_subln, even_w_out, odd_w_in, odd_conv_w,
           odd_w_out):
    batch, n, d = x.shape
    m_ctx = ctx.shape[1]
    assert d == D_MODEL and n % (FFT_N1 * FFT_KB) == 0 and n % ROW_BLOCK == 0 and n % GRID_W == 0
    assert batch < 8 and norm_g.shape[0] == 2

    cvec = jnp.zeros((8, d), F32).at[:batch].set(c).at[batch].set(c_ctx)
    mods = _mod_params(cvec, ada_w, ada_b)
    shift0, scale0, gate0 = [mods[0, :, j * d:(j + 1) * d].reshape(8, 1, d) for j in range(3)]
    shift1, scale1, gate1 = [mods[1, :, j * d:(j + 1) * d].reshape(8, 1, d) for j in range(3)]

    w_in = even_w_in[0].astype(BF16)
    wf = w_in[:, 0:F_WIDTH]
    wq = w_in[:, F_WIDTH:F_WIDTH + QK_WIDTH]
    wk = w_in[:, F_WIDTH + QK_WIDTH:F_WIDTH + 2 * QK_WIDTH]
    wv = w_in[:, F_WIDTH + 2 * QK_WIDTH:F_WIDTH + 2 * QK_WIDTH + ATT_WIDTH]
    wg = w_in[:, F_WIDTH + 2 * QK_WIDTH + ATT_WIDTH:]
    reps = QK_WIDTH // DH
    qg = (jnp.tile(even_q_norm[0].astype(F32), reps) * (ATTN_SCALE * LOG2E)).reshape(1, QK_WIDTH)
    kg = jnp.tile(even_k_norm[0].astype(F32), reps).reshape(1, QK_WIDTH)
    bd = _group_sum_matrix()
    cos, sin = _rope_tables(n)
    g0 = norm_g[0].reshape(1, d)
    g1 = norm_g[1].reshape(1, d)

    x2 = x.reshape(batch * n, d)
    u, q, k, vt, gs = _inproj0(x2, shift0, scale0, g0, cos, sin, qg, kg, bd, wf, wq, wk, wv, wg, batch, n)
    kc, vtc = _ctx_kv(ctx.reshape(batch * m_ctx, d), shift0[batch], scale0[batch], g0, kg, bd, wk, wv,
                      batch, m_ctx)

    k_all = jnp.concatenate([k.reshape(batch, n, QK_WIDTH), kc.reshape(batch, m_ctx, QK_WIDTH)], axis=1)
    vt_all = jnp.concatenate([vt, vtc], axis=-1)
    lam_p = jnp.stack([even_lambda_q1[0], even_lambda_k1[0], even_lambda_q2[0], even_lambda_k2[0]]).astype(F32)
    sg = even_subln[0].astype(F32).reshape(1, DV)
    ao = _diff_attention(lam_p, q.reshape(batch, n, QK_WIDTH), k_all, vt_all, sg, _lambda_init(0))

    fo = _fourier_mix(u, batch, n)

    x1, u1, bgs = _mid(x2, fo, ao.reshape(batch * n, ATT_WIDTH), gs, gate0, even_w_out[0].astype(BF16), g1,
                       shift1, scale1, odd_w_in[0].astype(BF16), n)
    out = _conv_out(x1, u1, bgs, odd_conv_w[0].astype(F32), gate1, odd_w_out[0].astype(BF16), n)
    return out.reshape(batch, n, d)
```

```python
import functools
import math

import numpy as np
import jax
import jax.numpy as jnp
from jax import lax
from jax.experimental import pallas as pl
from jax.experimental.pallas import tpu as pltpu

F32 = jnp.float32
BF16 = jnp.bfloat16

D_MODEL = 1024
GRID_W = 64
F_GROUPS = 4
F_CH = 128
F_WIDTH = F_GROUPS * F_CH
H_DIFF = 4
DH = 64
DV = 128
QK_WIDTH = H_DIFF * 2 * DH
ATT_WIDTH = H_DIFF * DV
CONV_K = 3
ROPE_BASE = 10000.0
EPS = 1e-6
ATTN_SCALE = DH ** -0.5
LOG2E = 1.4426950408889634

LANES = 128
MXU_DIM = 256
VMEM_LIMIT = 56 * 1024 * 1024

ROW_BLOCK = 512
FFT_N1 = 128
FFT_KB = 8
Q_BLOCK_ATT = 256
NEG_BIG = -1e30


def _dot(a, b):
    return jnp.dot(a, b, preferred_element_type=F32)


def _dot_nt(a, b):
    return lax.dot_general(a, b, (((1,), (1,)), ((), ())), preferred_element_type=F32)


def _sigmoid(x):
    return 1.0 / (1.0 + jnp.exp(-x))


def _params(sem):
    return pltpu.CompilerParams(dimension_semantics=sem, vmem_limit_bytes=VMEM_LIMIT)


def _mod_kernel(c_ref, w_ref, b_ref, o_ref):
    cv = c_ref[...]
    a = cv * _sigmoid(cv)
    w = w_ref[0]
    a_hi = a.astype(BF16)
    a_lo = (a - a_hi.astype(F32)).astype(BF16)
    w_hi = w.astype(BF16)
    w_lo = (w - w_hi.astype(F32)).astype(BF16)
    acc = _dot(a_hi, w_hi) + _dot(a_hi, w_lo) + _dot(a_lo, w_hi)
    o_ref[0] = acc + b_ref[0]


def _mod_params(cvec, ada_w, ada_b):
    depth, d, d3 = ada_w.shape
    tn = 1024
    return pl.pallas_call(
        _mod_kernel,
        grid=(depth, d3 // tn),
        in_specs=[
            pl.BlockSpec((8, d), lambda l, j: (0, 0)),
            pl.BlockSpec((1, d, tn), lambda l, j: (l, 0, j)),
            pl.BlockSpec((1, 1, tn), lambda l, j: (l, 0, j)),
        ],
        out_specs=pl.BlockSpec((1, 8, tn), lambda l, j: (l, 0, j)),
        out_shape=jax.ShapeDtypeStruct((depth, 8, d3), F32),
        compiler_params=_params(("parallel", "parallel")),
        name="mod_params",
    )(cvec, ada_w, ada_b.reshape(depth, 1, d3))


def _modulated(x, g, shift, scale):
    ms = jnp.mean(x * x, axis=-1, keepdims=True)
    y = x * lax.rsqrt(ms + EPS) * g
    return y * (1.0 + scale) + shift


def _group_norm(t, gain, bd):
    sq = (t * t).astype(BF16)
    half = MXU_DIM
    ss = jnp.concatenate([_dot(sq[:, :half], bd), _dot(sq[:, half:], bd)], axis=-1)
    return t * lax.rsqrt(ss * (1.0 / DH) + EPS) * gain


def _rope(t, cos, sin):
    width = t.shape[-1]
    quarter = DH // 4
    lane = lax.broadcasted_iota(jnp.int32, t.shape, 1)
    from_above = pltpu.roll(t, width - quarter, axis=1)
    from_below = pltpu.roll(t, quarter, axis=1)
    partner = jnp.where((lane & (2 * quarter - 1)) < quarter, from_above, from_below)
    reps = width // cos.shape[-1]
    cosf = jnp.concatenate([cos] * reps, axis=-1)
    sinf = jnp.concatenate([sin] * reps, axis=-1)
    return t * cosf + partner * sinf


def _inproj_kernel(x_ref, shift_ref, scale_ref, g_ref, cos_ref, sin_ref, qg_ref, kg_ref, bd_ref,
                   wf_ref, wq_ref, wk_ref, wv_ref, wg_ref,
                   u_ref, q_ref, k_ref, vt_ref, gs_ref):
    h = _modulated(x_ref[...], g_ref[...], shift_ref[0], scale_ref[0]).astype(BF16)
    cos = cos_ref[...]
    sin = sin_ref[...]
    bd = bd_ref[...]
    u_ref[...] = _dot(h, wf_ref[...]).astype(BF16)
    q = _rope(_group_norm(_dot(h, wq_ref[...]), qg_ref[...], bd), cos, sin)
    q_ref[...] = q.astype(BF16)
    k = _rope(_group_norm(_dot(h, wk_ref[...]), kg_ref[...], bd), cos, sin)
    k_ref[...] = k.astype(BF16)
    v = _dot(h, wv_ref[...])
    for hh in range(H_DIFF):
        vt_ref[0, hh] = v[:, hh * DV:(hh + 1) * DV].T.astype(BF16)
    g = _dot(h, wg_ref[...])
    gs_ref[...] = (g * _sigmoid(g)).astype(BF16)


def _ctx_kv_kernel(x_ref, shift_ref, scale_ref, g_ref, kg_ref, bd_ref, wk_ref, wv_ref, k_ref, vt_ref):
    h = _modulated(x_ref[...], g_ref[...], shift_ref[...], scale_ref[...]).astype(BF16)
    k = _group_norm(_dot(h, wk_ref[...]), kg_ref[...], bd_ref[...])
    k_ref[...] = k.astype(BF16)
    v = _dot(h, wv_ref[...])
    for hh in range(H_DIFF):
        vt_ref[0, hh] = v[:, hh * DV:(hh + 1) * DV].T.astype(BF16)


def _const_spec(shape):
    zeros = (0,) * len(shape)
    return pl.BlockSpec(shape, lambda *_: zeros)


def _inproj0(x2, shift, scale, g, cos, sin, qg, kg, bd, wf, wq, wk, wv, wg, batch, n):
    rows, d = x2.shape
    tm = ROW_BLOCK
    bpb = n // tm
    row_spec = lambda w: pl.BlockSpec((tm, w), lambda i: (i, 0))
    return pl.pallas_call(
        _inproj_kernel,
        grid=(rows // tm,),
        in_specs=[
            row_spec(d),
            pl.BlockSpec((1, 1, d), lambda i: (i // bpb, 0, 0)),
            pl.BlockSpec((1, 1, d), lambda i: (i // bpb, 0, 0)),
            _const_spec((1, d)),
            pl.BlockSpec((tm, LANES), lambda i: (i % bpb, 0)),
            pl.BlockSpec((tm, LANES), lambda i: (i % bpb, 0)),
            _const_spec((1, QK_WIDTH)),
            _const_spec((1, QK_WIDTH)),
            _const_spec((MXU_DIM, MXU_DIM)),
            _const_spec(wf.shape), _const_spec(wq.shape), _const_spec(wk.shape),
            _const_spec(wv.shape), _const_spec(wg.shape),
        ],
        out_specs=[
            row_spec(F_WIDTH), row_spec(QK_WIDTH), row_spec(QK_WIDTH),
            pl.BlockSpec((1, H_DIFF, DV, tm), lambda i: (i // bpb, 0, 0, i % bpb)),
            row_spec(d),
        ],
        out_shape=[
            jax.ShapeDtypeStruct((rows, F_WIDTH), BF16),
            jax.ShapeDtypeStruct((rows, QK_WIDTH), BF16),
            jax.ShapeDtypeStruct((rows, QK_WIDTH), BF16),
            jax.ShapeDtypeStruct((batch, H_DIFF, DV, n), BF16),
            jax.ShapeDtypeStruct((rows, d), BF16),
        ],
        compiler_params=_params(("parallel",)),
        name="inproj0",
    )(x2, shift, scale, g, cos, sin, qg, kg, bd, wf, wq, wk, wv, wg)


def _ctx_kv(ctx2, shift, scale, g, kg, bd, wk, wv, batch, m_ctx):
    rows, d = ctx2.shape
    return pl.pallas_call(
        _ctx_kv_kernel,
        grid=(batch,),
        in_specs=[
            pl.BlockSpec((m_ctx, d), lambda b: (b, 0)),
            _const_spec((1, d)), _const_spec((1, d)), _const_spec((1, d)),
            _const_spec((1, QK_WIDTH)), _const_spec((MXU_DIM, MXU_DIM)),
            _const_spec(wk.shape), _const_spec(wv.shape),
        ],
        out_specs=[
            pl.BlockSpec((m_ctx, QK_WIDTH), lambda b: (b, 0)),
            pl.BlockSpec((1, H_DIFF, DV, m_ctx), lambda b: (b, 0, 0, 0)),
        ],
        out_shape=[
            jax.ShapeDtypeStruct((rows, QK_WIDTH), BF16),
            jax.ShapeDtypeStruct((batch, H_DIFF, DV, m_ctx), BF16),
        ],
        compiler_params=_params(("parallel",)),
        name="ctx_kv",
    )(ctx2, shift, scale, g, kg, bd, wk, wv)


def _attn_kernel(lam_ref, q_ref, k_ref, vt_ref, sg_ref, o_ref,
                 acc1_ref, acc2_ref, s1a_ref, s1b_ref, s2a_ref, s2b_ref,
                 p1a_ref, p1b_ref, p2a_ref, p2b_ref, *, tk, lam_init):
    tq = q_ref.shape[1]
    n_chunks = k_ref.shape[1] // tk
    q = q_ref[0]
    lane = lax.broadcasted_iota(jnp.int32, q.shape, 1)
    zero = jnp.zeros_like(q)
    qz = (jnp.where(lane < DH, q, zero), jnp.where(lane >= DH, q, zero))
    acc_refs = (acc1_ref, acc2_ref)
    s_refs = ((s1a_ref, s1b_ref), (s2a_ref, s2b_ref))
    p_refs = ((p1a_ref, p1b_ref), (p2a_ref, p2b_ref))

    def scores(c):
        kc = k_ref[0, c * tk:(c + 1) * tk, :]
        cm = []
        for mp in range(2):
            s = _dot_nt(kc, qz[mp])
            s_refs[mp][c % 2][...] = s
            cm.append(jnp.max(s, axis=0, keepdims=True))
        return cm

    def softmax(c, cm, m_old, l_old):
        m_new, l_new, alpha = [], [], []
        for mp in range(2):
            mn = jnp.maximum(m_old[mp], cm[mp])
            al = jnp.exp2(m_old[mp] - mn)
            p = jnp.exp2(s_refs[mp][c % 2][...] - mn)
            p_refs[mp][c % 2][...] = p.astype(BF16)
            m_new.append(mn)
            l_new.append(al * l_old[mp] + jnp.sum(p, axis=0, keepdims=True))
            alpha.append(al)
        return m_new, l_new, alpha

    def pv(c, alpha):
        vc = vt_ref[0, 0, :, c * tk:(c + 1) * tk]
        for mp in range(2):
            upd = _dot(vc, p_refs[mp][c % 2][...])
            if c == 0:
                acc_refs[mp][...] = upd
            else:
                acc_refs[mp][...] = alpha[mp] * acc_refs[mp][...] + upd

    m = [jnp.full((1, tq), NEG_BIG, F32)] * 2
    l = [jnp.zeros((1, tq), F32)] * 2
    cm = scores(0)
    alpha_prev = None
    for c in range(n_chunks):
        cm_next = scores(c + 1) if c + 1 < n_chunks else None
        m, l, alpha = softmax(c, cm, m, l)
        if c >= 1:
            pv(c - 1, alpha_prev)
        alpha_prev = alpha
        cm = cm_next
    pv(n_chunks - 1, alpha_prev)
    l1, l2 = l

    lp = lam_ref[...]
    lam = (jnp.exp(jnp.sum(lp[0:1] * lp[1:2], axis=-1, keepdims=True))
           - jnp.exp(jnp.sum(lp[2:3] * lp[3:4], axis=-1, keepdims=True)) + lam_init)
    ot = acc1_ref[...] / l1 - lam * (acc2_ref[...] / l2)
    o = ot.T
    ms = jnp.mean(o * o, axis=-1, keepdims=True)
    o = o * lax.rsqrt(ms + EPS) * sg_ref[...] * (1.0 - lam_init)
    o_ref[0] = o.astype(BF16)


def _pick_key_chunk(m_keys):
    best = LANES
    for cand in range(LANES, 1024 + 1, LANES):
        if m_keys % cand == 0:
            best = cand
    return best


def _diff_attention(lam_p, q, k_all, vt_all, sg, lam_init):
    batch, n, _ = q.shape
    m_keys = k_all.shape[1]
    tq = Q_BLOCK_ATT
    tk = _pick_key_chunk(m_keys)
    kern = functools.partial(_attn_kernel, tk=tk, lam_init=lam_init)
    return pl.pallas_call(
        kern,
        grid=(batch, H_DIFF, n // tq),
        in_specs=[
            _const_spec((4, DH)),
            pl.BlockSpec((1, tq, 2 * DH), lambda b, h, i: (b, i, h)),
            pl.BlockSpec((1, m_keys, 2 * DH), lambda b, h, i: (b, 0, h)),
            pl.BlockSpec((1, 1, DV, m_keys), lambda b, h, i: (b, h, 0, 0)),
            _const_spec((1, DV)),
        ],
        out_specs=pl.BlockSpec((1, tq, DV), lambda b, h, i: (b, i, h)),
        out_shape=jax.ShapeDtypeStruct((batch, n, ATT_WIDTH), BF16),
        scratch_shapes=([pltpu.VMEM((DV, tq), F32)] * 2 + [pltpu.VMEM((tk, tq), F32)] * 4
                        + [pltpu.VMEM((tk, tq), BF16)] * 4),
        compiler_params=_params(("parallel", "parallel", "arbitrary")),
        name="diff_attn",
    )(lam_p, q, k_all, vt_all, sg)


def _fft_rows_kernel(f_ref, u_ref, y_ref):
    y_ref[0] = _dot(f_ref[...], u_ref[0]).astype(BF16)


def _fft_rows(f1cs, u_view):
    batch, n1, cols = u_view.shape
    ch = min(cols, 4096)
    return pl.pallas_call(
        _fft_rows_kernel,
        grid=(batch, cols // ch),
        in_specs=[
            _const_spec(f1cs.shape),
            pl.BlockSpec((1, n1, ch), lambda b, j: (b, 0, j)),
        ],
        out_specs=pl.BlockSpec((1, 2 * n1, ch), lambda b, j: (b, 0, j)),
        out_shape=jax.ShapeDtypeStruct((batch, 2 * n1, cols), BF16),
        compiler_params=_params(("parallel", "parallel")),
        name="fft_rows",
    )(f1cs, u_view)


def _fft_cols_kernel(g_ref, y_ref, bdc_ref, bds_ref, o_ref):
    _, _, kb, n2, width = y_ref.shape
    rows = kb * n2
    y = y_ref[0].reshape(2 * rows, width)
    z = _dot(g_ref[0], y)
    xr = z[:rows].astype(BF16)
    xi = z[rows:].astype(BF16)
    bdc = bdc_ref[...]
    bds = bds_ref[...]
    half = MXU_DIM
    o = jnp.concatenate(
        [_dot(xr[:, :half], bdc) + _dot(xi[:, :half], bds),
         _dot(xr[:, half:], bdc) + _dot(xi[:, half:], bds)], axis=-1)
    o_ref[0] = o.reshape(n2, kb, width)


def _fft_cols(gbig, y5, bdc, bds):
    batch, _, n1, n2, width = y5.shape
    kb = FFT_KB
    gr = gbig.shape[1]
    return pl.pallas_call(
        _fft_cols_kernel,
        grid=(n1 // kb, batch),
        in_specs=[
            pl.BlockSpec((1, gr, gr), lambda j, b: (j, 0, 0)),
            pl.BlockSpec((1, 2, kb, n2, width), lambda j, b: (b, 0, j, 0, 0)),
            _const_spec(bdc.shape), _const_spec(bds.shape),
        ],
        out_specs=pl.BlockSpec((1, n2, kb, width), lambda j, b: (b, 0, j, 0)),
        out_shape=jax.ShapeDtypeStruct((batch, n2, n1, width), F32),
        compiler_params=_params(("parallel", "arbitrary")),
        name="fft_cols",
    )(gbig, y5, bdc, bds)


@functools.lru_cache(maxsize=None)
def _fft_tables(n):
    n1 = FFT_N1
    n2 = n // n1
    kb = FFT_KB
    k1 = np.arange(n1)
    ang = 2.0 * np.pi * np.outer(k1, k1) / n1
    f1cs = np.concatenate([np.cos(ang), -np.sin(ang)], axis=0)
    scale = 1.0 / math.sqrt(n * F_CH)
    nblk = n1 // kb
    rows = n2 * kb
    gbig = np.zeros((nblk, 2, n2, kb, 2, kb, n2), np.float32)
    n2i = np.arange(n2)
    for j in range(kb):
        kk = (np.arange(nblk)[:, None] * kb + j) + n1 * n2i[None, :]
        a = 2.0 * np.pi * (kk[:, :, None] * n2i[None, None, :] % n) / n
        gr = np.cos(a) * scale
        gi = -np.sin(a) * scale
        gbig[:, 0, :, j, 0, j, :] = gr
        gbig[:, 0, :, j, 1, j, :] = -gi
        gbig[:, 1, :, j, 0, j, :] = gi
        gbig[:, 1, :, j, 1, j, :] = gr
    gbig = gbig.reshape(nblk, 2 * rows, 2 * rows)
    c = np.arange(F_CH)
    angc = 2.0 * np.pi * np.outer(c, c) / F_CH
    eye2 = np.eye(MXU_DIM // F_CH)
    bdc = np.kron(eye2, np.cos(angc))
    bds = np.kron(eye2, np.sin(angc))
    to_bf16 = lambda a: jnp.asarray(a, dtype=F32).astype(BF16)
    return to_bf16(f1cs), to_bf16(gbig), to_bf16(bdc), to_bf16(bds)


def _fourier_mix(u, batch, n):
    f1cs, gbig, bdc, bds = _fft_tables(n)
    n1 = FFT_N1
    n2 = n // n1
    y = _fft_rows(f1cs, u.reshape(batch, n1, n2 * F_WIDTH))
    fo = _fft_cols(gbig, y.reshape(batch, 2, n1, n2, F_WIDTH), bdc, bds)
    return fo.reshape(batch * n, F_WIDTH)


def _mid_kernel(x_ref, fo_ref, ao_ref, gs_ref, gate_ref, wo_ref, g1_ref, shift_ref, scale_ref, wi_ref,
                x1_ref, u_ref, bgs_ref):
    gs = gs_ref[...].astype(F32)
    yf = (fo_ref[...] * gs[:, :F_WIDTH]).astype(BF16)
    ya = (ao_ref[...].astype(F32) * gs[:, F_WIDTH:]).astype(BF16)
    y = _dot(yf, wo_ref[:F_WIDTH, :]) + _dot(ya, wo_ref[F_WIDTH:, :])
    x1 = x_ref[...] + gate_ref[0] * y
    x1_ref[...] = x1
    h = _modulated(x1, g1_ref[...], shift_ref[0], scale_ref[0]).astype(BF16)
    w = x1.shape[-1]
    bg = _dot(h, wi_ref[:, 0 * w:1 * w])
    gg = _dot(h, wi_ref[:, 3 * w:4 * w])
    bgs_ref[...] = (bg * (gg * _sigmoid(gg))).astype(BF16)
    cg = _dot(h, wi_ref[:, 1 * w:2 * w])
    xt = _dot(h, wi_ref[:, 2 * w:3 * w])
    u_ref[...] = (cg * xt).astype(BF16)


def _mid(x2, fo, ao, gs, gate0, wo, g1, shift1, scale1, wi, n):
    rows, d = x2.shape
    tm = ROW_BLOCK
    bpb = n // tm
    row_spec = lambda w: pl.BlockSpec((tm, w), lambda i: (i, 0))
    per_batch = pl.BlockSpec((1, 1, d), lambda i: (i // bpb, 0, 0))
    return pl.pallas_call(
        _mid_kernel,
        grid=(rows // tm,),
        in_specs=[
            row_spec(d), row_spec(F_WIDTH), row_spec(ATT_WIDTH), row_spec(d), per_batch,
            _const_spec(wo.shape), _const_spec((1, d)), per_batch, per_batch, _const_spec(wi.shape),
        ],
        out_specs=[row_spec(d), row_spec(d), row_spec(d)],
        out_shape=[
            jax.ShapeDtypeStruct((rows, d), F32),
            jax.ShapeDtypeStruct((rows, d), BF16),
            jax.ShapeDtypeStruct((rows, d), BF16),
        ],
        compiler_params=_params(("parallel",)),
        name="mid",
    )(x2, fo, ao, gs, gate0, wo, g1, shift1, scale1, wi)


def _conv_out_kernel(x1_ref, u_ref, up_ref, un_ref, bgs_ref, cw_ref, gate_ref, wo_ref, o_ref, *, bpb):
    i = pl.program_id(0)
    tm = u_ref.shape[0]
    halo = up_ref.shape[0]
    u = u_ref[...].astype(F32)
    first = (i % bpb) == 0
    last = (i % bpb) == (bpb - 1)
    prev_row = jnp.where(first, 0.0, up_ref[halo - 1:halo, :].astype(F32))
    next_row = jnp.where(last, 0.0, un_ref[0:1, :].astype(F32))
    row = lax.broadcasted_iota(jnp.int32, u.shape, 0)
    u_prev = jnp.where(row == 0, prev_row, pltpu.roll(u, 1, axis=0))
    u_next = jnp.where(row == tm - 1, next_row, pltpu.roll(u, tm - 1, axis=0))
    cw = cw_ref[...]
    conv = cw[0:1] * u_prev + cw[1:2] * u + cw[2:3] * u_next
    z = (conv * bgs_ref[...].astype(F32)).astype(BF16)
    o_ref[...] = x1_ref[...] + gate_ref[0] * _dot(z, wo_ref[...])


def _conv_out(x1, u, bgs, cw, gate1, wo, n):
    rows, d = x1.shape
    tm = ROW_BLOCK
    bpb = n // tm
    halo = 16
    hb = tm // halo
    nhalo = rows // halo
    row_spec = lambda w: pl.BlockSpec((tm, w), lambda i: (i, 0))
    kern = functools.partial(_conv_out_kernel, bpb=bpb)
    return pl.pallas_call(
        kern,
        grid=(rows // tm,),
        in_specs=[
            row_spec(d), row_spec(d),
            pl.BlockSpec((halo, d), lambda i: (jnp.maximum(i * hb - 1, 0), 0)),
            pl.BlockSpec((halo, d), lambda i: (jnp.minimum((i + 1) * hb, nhalo - 1), 0)),
            row_spec(d),
            _const_spec(cw.shape),
            pl.BlockSpec((1, 1, d), lambda i: (i // bpb, 0, 0)),
            _const_spec(wo.shape),
        ],
        out_specs=row_spec(d),
        out_shape=jax.ShapeDtypeStruct((rows, d), F32),
        compiler_params=_params(("parallel",)),
        name="conv_out",
    )(x1, u, u, u, bgs, cw, gate1, wo)


@functools.lru_cache(maxsize=None)
def _rope_tables(n):
    rows = n // GRID_W
    pos_r = np.repeat(np.arange(rows, dtype=np.float64), GRID_W)
    pos_c = np.tile(np.arange(GRID_W, dtype=np.float64), rows)
    half = DH // 2
    freqs = ROPE_BASE ** (-np.arange(0, half, 2, dtype=np.float64) / half)
    ang_r = pos_r[:, None] * freqs
    ang_c = pos_c[:, None] * freqs
    cos64 = np.concatenate([np.cos(ang_r), np.cos(ang_r), np.cos(ang_c), np.cos(ang_c)], axis=-1)
    sin64 = np.concatenate([-np.sin(ang_r), np.sin(ang_r), -np.sin(ang_c), np.sin(ang_c)], axis=-1)
    cos = np.concatenate([cos64, cos64], axis=-1).astype(np.float32)
    sin = np.concatenate([sin64, sin64], axis=-1).astype(np.float32)
    return jnp.asarray(cos), jnp.asarray(sin)


@functools.lru_cache(maxsize=None)
def _group_sum_matrix():
    return jnp.asarray(np.kron(np.eye(MXU_DIM // DH), np.ones((DH, DH))), dtype=BF16)


def _lambda_init(layer_idx):
    return 0.8 - 0.6 * math.exp(-0.3 * layer_idx)


def kernel(x, c, ctx, c_ctx, norm_g, ada_w, ada_b, even_w_in, even_q_norm, even_k_norm, even_lambda_q1,
           even_lambda_k1, even_lambda_q2, even_lambda_k2, even_subln, even_w_out, odd_w_in, odd_conv_w,
           odd_w_out):
    batch, n, d = x.shape
    m_ctx = ctx.shape[1]
    assert d == D_MODEL and n % (FFT_N1 * FFT_KB) == 0 and n % ROW_BLOCK == 0 and n % GRID_W == 0
    assert batch < 8 and norm_g.shape[0] == 2

    cvec = jnp.zeros((8, d), F32).at[:batch].set(c).at[batch].set(c_ctx)
    mods = _mod_params(cvec, ada_w, ada_b)
    shift0, scale0, gate0 = [mods[0, :, j * d:(j + 1) * d].reshape(8, 1, d) for j in range(3)]
    shift1, scale1, gate1 = [mods[1, :, j * d:(j + 1) * d].reshape(8, 1, d) for j in range(3)]

    w_in = even_w_in[0].astype(BF16)
    wf = w_in[:, 0:F_WIDTH]
    wq = w_in[:, F_WIDTH:F_WIDTH + QK_WIDTH]
    wk = w_in[:, F_WIDTH + QK_WIDTH:F_WIDTH + 2 * QK_WIDTH]
    wv = w_in[:, F_WIDTH + 2 * QK_WIDTH:F_WIDTH + 2 * QK_WIDTH + ATT_WIDTH]
    wg = w_in[:, F_WIDTH + 2 * QK_WIDTH + ATT_WIDTH:]
    reps = QK_WIDTH // DH
    qg = (jnp.tile(even_q_norm[0].astype(F32), reps) * (ATTN_SCALE * LOG2E)).reshape(1, QK_WIDTH)
    kg = jnp.tile(even_k_norm[0].astype(F32), reps).reshape(1, QK_WIDTH)
    bd = _group_sum_matrix()
    cos, sin = _rope_tables(n)
    g0 = norm_g[0].reshape(1, d)
    g1 = norm_g[1].reshape(1, d)

    x2 = x.reshape(batch * n, d)
    u, q, k, vt, gs = _inproj0(x2, shift0, scale0, g0, cos, sin, qg, kg, bd, wf, wq, wk, wv, wg, batch, n)
    kc, vtc = _ctx_kv(ctx.reshape(batch * m_ctx, d), shift0[batch], scale0[batch], g0, kg, bd, wk, wv,
                      batch, m_ctx)

    k_all = jnp.concatenate([k.reshape(batch, n, QK_WIDTH), kc.reshape(batch, m_ctx, QK_WIDTH)], axis=1)
    vt_all = jnp.concatenate([vt, vtc], axis=-1)
    lam_p = jnp.stack([even_lambda_q1[0], even_lambda_k1[0], even_lambda_q2[0], even_lambda_k2[0]]).astype(F32)
    sg = even_subln[0].astype(F32).reshape(1, DV)
    ao = _diff_attention(lam_p, q.reshape(batch, n, QK_WIDTH), k_all, vt_all, sg, _lambda_init(0))

    fo = _fourier_mix(u, batch, n)

    x1, u1, bgs = _mid(x2, fo, ao.reshape(batch * n, ATT_WIDTH), gs, gate0, even_w_out[0].astype(BF16), g1,
                       shift1, scale1, odd_w_in[0].astype(BF16), n)
    out = _conv_out(x1, u1, bgs, odd_conv_w[0].astype(F32), gate1, odd_w_out[0].astype(BF16), n)
    return out.reshape(batch, n, d)
```

```python
import functools
import math

import numpy as np
import jax
import jax.numpy as jnp
from jax import lax
from jax.experimental import pallas as pl
from jax.experimental.pallas import tpu as pltpu

F32 = jnp.float32
BF16 = jnp.bfloat16

D_MODEL = 1024
GRID_W = 64
F_GROUPS = 4
F_CH = 128
F_WIDTH = F_GROUPS * F_CH
H_DIFF = 4
DH = 64
DV = 128
QK_WIDTH = H_DIFF * 2 * DH
ATT_WIDTH = H_DIFF * DV
CONV_K = 3
ROPE_BASE = 10000.0
EPS = 1e-6
ATTN_SCALE = DH ** -0.5
LOG2E = 1.4426950408889634

LANES = 128
MXU_DIM = 256
VMEM_LIMIT = 56 * 1024 * 1024

ROW_BLOCK = 512
FFT_N1 = 128
FFT_KB = 8
Q_BLOCK_ATT = 512
NEG_BIG = -1e30
SAFE_SUM_LO = 2.0 ** -60
SAFE_SUM_HI = 2.0 ** 100


def _dot(a, b):
    return jnp.dot(a, b, preferred_element_type=F32)


def _dot_nt(a, b):
    return lax.dot_general(a, b, (((1,), (1,)), ((), ())), preferred_element_type=F32)


def _sigmoid(x):
    return 1.0 / (1.0 + jnp.exp(-x))


def _params(sem):
    return pltpu.CompilerParams(dimension_semantics=sem, vmem_limit_bytes=VMEM_LIMIT)


def _mod_kernel(c_ref, w_ref, b_ref, o_ref):
    cv = c_ref[...]
    a = cv * _sigmoid(cv)
    w = w_ref[0]
    a_hi = a.astype(BF16)
    a_lo = (a - a_hi.astype(F32)).astype(BF16)
    w_hi = w.astype(BF16)
    w_lo = (w - w_hi.astype(F32)).astype(BF16)
    acc = _dot(a_hi, w_hi) + _dot(a_hi, w_lo) + _dot(a_lo, w_hi)
    o_ref[0] = acc + b_ref[0]


def _mod_params(cvec, ada_w, ada_b):
    depth, d, d3 = ada_w.shape
    tn = 1024
    return pl.pallas_call(
        _mod_kernel,
        grid=(depth, d3 // tn),
        in_specs=[
            pl.BlockSpec((8, d), lambda l, j: (0, 0)),
            pl.BlockSpec((1, d, tn), lambda l, j: (l, 0, j)),
            pl.BlockSpec((1, 1, tn), lambda l, j: (l, 0, j)),
        ],
        out_specs=pl.BlockSpec((1, 8, tn), lambda l, j: (l, 0, j)),
        out_shape=jax.ShapeDtypeStruct((depth, 8, d3), F32),
        compiler_params=_params(("parallel", "parallel")),
        name="mod_params",
    )(cvec, ada_w, ada_b.reshape(depth, 1, d3))


def _modulated(x, g, shift, scale):
    ms = jnp.mean(x * x, axis=-1, keepdims=True)
    y = x * lax.rsqrt(ms + EPS) * g
    return y * (1.0 + scale) + shift


def _group_norm(t, gain, bd):
    sq = (t * t).astype(BF16)
    half = MXU_DIM
    ss = jnp.concatenate([_dot(sq[:, :half], bd), _dot(sq[:, half:], bd)], axis=-1)
    return t * lax.rsqrt(ss * (1.0 / DH) + EPS) * gain


def _rope(t, cos, sin):
    width = t.shape[-1]
    quarter = DH // 4
    lane = lax.broadcasted_iota(jnp.int32, t.shape, 1)
    from_above = pltpu.roll(t, width - quarter, axis=1)
    from_below = pltpu.roll(t, quarter, axis=1)
    partner = jnp.where((lane & (2 * quarter - 1)) < quarter, from_above, from_below)
    reps = width // cos.shape[-1]
    cosf = jnp.concatenate([cos] * reps, axis=-1)
    sinf = jnp.concatenate([sin] * reps, axis=-1)
    return t * cosf + partner * sinf


def _inproj_kernel(x_ref, shift_ref, scale_ref, g_ref, cos_ref, sin_ref, qg_ref, kg_ref, bd_ref,
                   wf_ref, wq_ref, wk_ref, wv_ref, wg_ref,
                   u_ref, q_ref, k_ref, vt_ref, gs_ref):
    h = _modulated(x_ref[...], g_ref[...], shift_ref[0], scale_ref[0]).astype(BF16)
    cos = cos_ref[...]
    sin = sin_ref[...]
    bd = bd_ref[...]
    u_ref[...] = _dot(h, wf_ref[...]).astype(BF16)
    q = _rope(_group_norm(_dot(h, wq_ref[...]), qg_ref[...], bd), cos, sin)
    q_ref[...] = q.astype(BF16)
    k = _rope(_group_norm(_dot(h, wk_ref[...]), kg_ref[...], bd), cos, sin)
    k_ref[...] = k.astype(BF16)
    v = _dot(h, wv_ref[...])
    for hh in range(H_DIFF):
        vt_ref[0, hh] = v[:, hh * DV:(hh + 1) * DV].T.astype(BF16)
    g = _dot(h, wg_ref[...])
    gs_ref[...] = (g * _sigmoid(g)).astype(BF16)


def _ctx_kv_kernel(x_ref, shift_ref, scale_ref, g_ref, kg_ref, bd_ref, wk_ref, wv_ref, k_ref, vt_ref):
    h = _modulated(x_ref[...], g_ref[...], shift_ref[...], scale_ref[...]).astype(BF16)
    k = _group_norm(_dot(h, wk_ref[...]), kg_ref[...], bd_ref[...])
    k_ref[...] = k.astype(BF16)
    v = _dot(h, wv_ref[...])
    for hh in range(H_DIFF):
        vt_ref[0, hh] = v[:, hh * DV:(hh + 1) * DV].T.astype(BF16)


def _const_spec(shape):
    zeros = (0,) * len(shape)
    return pl.BlockSpec(shape, lambda *_: zeros)


def _inproj0(x2, shift, scale, g, cos, sin, qg, kg, bd, wf, wq, wk, wv, wg, batch, n):
    rows, d = x2.shape
    tm = ROW_BLOCK
    bpb = n // tm
    row_spec = lambda w: pl.BlockSpec((tm, w), lambda i: (i, 0))
    return pl.pallas_call(
        _inproj_kernel,
        grid=(rows // tm,),
        in_specs=[
            row_spec(d),
            pl.BlockSpec((1, 1, d), lambda i: (i // bpb, 0, 0)),
            pl.BlockSpec((1, 1, d), lambda i: (i // bpb, 0, 0)),
            _const_spec((1, d)),
            pl.BlockSpec((tm, LANES), lambda i: (i % bpb, 0)),
            pl.BlockSpec((tm, LANES), lambda i: (i % bpb, 0)),
            _const_spec((1, QK_WIDTH)),
            _const_spec((1, QK_WIDTH)),
            _const_spec((MXU_DIM, MXU_DIM)),
            _const_spec(wf.shape), _const_spec(wq.shape), _const_spec(wk.shape),
            _const_spec(wv.shape), _const_spec(wg.shape),
        ],
        out_specs=[
            row_spec(F_WIDTH), row_spec(QK_WIDTH), row_spec(QK_WIDTH),
            pl.BlockSpec((1, H_DIFF, DV, tm), lambda i: (i // bpb, 0, 0, i % bpb)),
            row_spec(d),
        ],
        out_shape=[
            jax.ShapeDtypeStruct((rows, F_WIDTH), BF16),
            jax.ShapeDtypeStruct((rows, QK_WIDTH), BF16),
            jax.ShapeDtypeStruct((rows, QK_WIDTH), BF16),
            jax.ShapeDtypeStruct((batch, H_DIFF, DV, n), BF16),
            jax.ShapeDtypeStruct((rows, d), BF16),
        ],
        compiler_params=_params(("parallel",)),
        name="inproj0",
    )(x2, shift, scale, g, cos, sin, qg, kg, bd, wf, wq, wk, wv, wg)


def _ctx_kv(ctx2, shift, scale, g, kg, bd, wk, wv, batch, m_ctx):
    rows, d = ctx2.shape
    return pl.pallas_call(
        _ctx_kv_kernel,
        grid=(batch,),
        in_specs=[
            pl.BlockSpec((m_ctx, d), lambda b: (b, 0)),
            _const_spec((1, d)), _const_spec((1, d)), _const_spec((1, d)),
            _const_spec((1, QK_WIDTH)), _const_spec((MXU_DIM, MXU_DIM)),
            _const_spec(wk.shape), _const_spec(wv.shape),
        ],
        out_specs=[
            pl.BlockSpec((m_ctx, QK_WIDTH), lambda b: (b, 0)),
            pl.BlockSpec((1, H_DIFF, DV, m_ctx), lambda b: (b, 0, 0, 0)),
        ],
        out_shape=[
            jax.ShapeDtypeStruct((rows, QK_WIDTH), BF16),
            jax.ShapeDtypeStruct((batch, H_DIFF, DV, m_ctx), BF16),
        ],
        compiler_params=_params(("parallel",)),
        name="ctx_kv",
    )(ctx2, shift, scale, g, kg, bd, wk, wv)


def _attn_kernel(lam_ref, q_ref, k_ref, vt_ref, sg_ref, o_ref,
                 acc1_ref, acc2_ref, p1a_ref, p1b_ref, p2a_ref, p2b_ref, *, tk, lam_init):
    tq = q_ref.shape[1]
    n_chunks = k_ref.shape[1] // tk
    q = q_ref[0]
    lane = lax.broadcasted_iota(jnp.int32, q.shape, 1)
    zero = jnp.zeros_like(q)
    qz = (jnp.where(lane < DH, q, zero), jnp.where(lane >= DH, q, zero))
    acc_refs = (acc1_ref, acc2_ref)
    p_refs = ((p1a_ref, p1b_ref), (p2a_ref, p2b_ref))

    def finish(l1, l2):
        lp = lam_ref[...]
        lam = (jnp.exp(jnp.sum(lp[0:1] * lp[1:2], axis=-1, keepdims=True))
               - jnp.exp(jnp.sum(lp[2:3] * lp[3:4], axis=-1, keepdims=True)) + lam_init)
        ot = acc1_ref[...] / l1 - lam * (acc2_ref[...] / l2)
        o = ot.T
        ms = jnp.mean(o * o, axis=-1, keepdims=True)
        o = o * lax.rsqrt(ms + EPS) * sg_ref[...] * (1.0 - lam_init)
        o_ref[0] = o.astype(BF16)

    def exp_scores(c):
        kc = k_ref[0, c * tk:(c + 1) * tk, :]
        sums = []
        for mp in range(2):
            p = jnp.exp2(_dot_nt(kc, qz[mp]))
            p_refs[mp][c % 2][...] = p.astype(BF16)
            sums.append(jnp.sum(p, axis=0, keepdims=True))
        return sums

    def pv(c):
        vc = vt_ref[0, 0, :, c * tk:(c + 1) * tk]
        for mp in range(2):
            upd = _dot(vc, p_refs[mp][c % 2][...])
            acc_refs[mp][...] = upd if c == 0 else acc_refs[mp][...] + upd

    l = exp_scores(0)
    for c in range(1, n_chunks):
        sums = exp_scores(c)
        pv(c - 1)
        l = [l[mp] + sums[mp] for mp in range(2)]
    pv(n_chunks - 1)
    finish(l[0], l[1])

    in_range = jnp.logical_and(jnp.minimum(l[0], l[1]) > SAFE_SUM_LO, jnp.maximum(l[0], l[1]) < SAFE_SUM_HI)
    all_safe = jnp.min(jnp.where(in_range, 1.0, 0.0)) > 0.5

    @pl.when(jnp.logical_not(all_safe))
    def _():
        def one_map(kc, vc, qm, m_old, l_old, acc_ref):
            s = _dot_nt(kc, qm)
            m_new = jnp.maximum(m_old, jnp.max(s, axis=0, keepdims=True))
            alpha = jnp.exp2(m_old - m_new)
            p = jnp.exp2(s - m_new)
            acc_ref[...] = alpha * acc_ref[...] + _dot(vc, p.astype(BF16))
            return m_new, alpha * l_old + jnp.sum(p, axis=0, keepdims=True)

        def body(c, carry):
            m1, l1, m2, l2 = carry
            start = pl.multiple_of(c * tk, tk)
            kc = k_ref[0, pl.ds(start, tk), :]
            vc = vt_ref[0, 0, :, pl.ds(start, tk)]
            m1, l1 = one_map(kc, vc, qz[0], m1, l1, acc1_ref)
            m2, l2 = one_map(kc, vc, qz[1], m2, l2, acc2_ref)
            return m1, l1, m2, l2

        acc1_ref[...] = jnp.zeros_like(acc1_ref)
        acc2_ref[...] = jnp.zeros_like(acc2_ref)
        neg = jnp.full((1, tq), NEG_BIG, F32)
        zer = jnp.zeros((1, tq), F32)
        _, l1, _, l2 = lax.fori_loop(0, n_chunks, body, (neg, zer, neg, zer))
        finish(l1, l2)


def _pick_key_chunk(m_keys):
    best = LANES
    for cand in range(LANES, 1024 + 1, LANES):
        if m_keys % cand == 0:
            best = cand
    return best


def _diff_attention(lam_p, q, k_all, vt_all, sg, lam_init):
    batch, n, _ = q.shape
    m_keys = k_all.shape[1]
    tq = Q_BLOCK_ATT
    tk = _pick_key_chunk(m_keys)
    kern = functools.partial(_attn_kernel, tk=tk, lam_init=lam_init)
    return pl.pallas_call(
        kern,
        grid=(batch, H_DIFF, n // tq),
        in_specs=[
            _const_spec((4, DH)),
            pl.BlockSpec((1, tq, 2 * DH), lambda b, h, i: (b, i, h)),
            pl.BlockSpec((1, m_keys, 2 * DH), lambda b, h, i: (b, 0, h)),
            pl.BlockSpec((1, 1, DV, m_keys), lambda b, h, i: (b, h, 0, 0)),
            _const_spec((1, DV)),
        ],
        out_specs=pl.BlockSpec((1, tq, DV), lambda b, h, i: (b, i, h)),
        out_shape=jax.ShapeDtypeStruct((batch, n, ATT_WIDTH), BF16),
        scratch_shapes=[pltpu.VMEM((DV, tq), F32)] * 2 + [pltpu.VMEM((tk, tq), BF16)] * 4,
        compiler_params=_params(("parallel", "parallel", "arbitrary")),
        name="diff_attn",
    )(lam_p, q, k_all, vt_all, sg)


def _fft_rows_kernel(f_ref, u_ref, y_ref):
    y_ref[0] = _dot(f_ref[...], u_ref[0]).astype(BF16)


def _fft_rows(f1cs, u_view):
    batch, n1, cols = u_view.shape
    ch = min(cols, 4096)
    return pl.pallas_call(
        _fft_rows_kernel,
        grid=(batch, cols // ch),
        in_specs=[
            _const_spec(f1cs.shape),
            pl.BlockSpec((1, n1, ch), lambda b, j: (b, 0, j)),
        ],
        out_specs=pl.BlockSpec((1, 2 * n1, ch), lambda b, j: (b, 0, j)),
        out_shape=jax.ShapeDtypeStruct((batch, 2 * n1, cols), BF16),
        compiler_params=_params(("parallel", "parallel")),
        name="fft_rows",
    )(f1cs, u_view)


def _fft_cols_kernel(g_ref, y_ref, bdc_ref, bds_ref, o_ref):
    _, _, kb, n2, width = y_ref.shape
    rows = kb * n2
    y = y_ref[0].reshape(2 * rows, width)
    z = _dot(g_ref[0], y)
    xr = z[:rows].astype(BF16)
    xi = z[rows:].astype(BF16)
    bdc = bdc_ref[...]
    bds = bds_ref[...]
    half = MXU_DIM
    o = jnp.concatenate(
        [_dot(xr[:, :half], bdc) + _dot(xi[:, :half], bds),
         _dot(xr[:, half:], bdc) + _dot(xi[:, half:], bds)], axis=-1)
    o_ref[0] = o.reshape(n2, kb, width)


def _fft_cols(gbig, y5, bdc, bds):
    batch, _, n1, n2, width = y5.shape
    kb = FFT_KB
    gr = gbig.shape[1]
    return pl.pallas_call(
        _fft_cols_kernel,
        grid=(n1 // kb, batch),
        in_specs=[
            pl.BlockSpec((1, gr, gr), lambda j, b: (j, 0, 0)),
            pl.BlockSpec((1, 2, kb, n2, width), lambda j, b: (b, 0, j, 0, 0)),
            _const_spec(bdc.shape), _const_spec(bds.shape),
        ],
        out_specs=pl.BlockSpec((1, n2, kb, width), lambda j, b: (b, 0, j, 0)),
        out_shape=jax.ShapeDtypeStruct((batch, n2, n1, width), F32),
        compiler_params=_params(("parallel", "arbitrary")),
        name="fft_cols",
    )(gbig, y5, bdc, bds)


@functools.lru_cache(maxsize=None)
def _fft_tables(n):
    n1 = FFT_N1
    n2 = n // n1
    kb = FFT_KB
    k1 = np.arange(n1)
    ang = 2.0 * np.pi * np.outer(k1, k1) / n1
    f1cs = np.concatenate([np.cos(ang), -np.sin(ang)], axis=0)
    scale = 1.0 / math.sqrt(n * F_CH)
    nblk = n1 // kb
    rows = n2 * kb
    gbig = np.zeros((nblk, 2, n2, kb, 2, kb, n2), np.float32)
    n2i = np.arange(n2)
    for j in range(kb):
        kk = (np.arange(nblk)[:, None] * kb + j) + n1 * n2i[None, :]
        a = 2.0 * np.pi * (kk[:, :, None] * n2i[None, None, :] % n) / n
        gr = np.cos(a) * scale
        gi = -np.sin(a) * scale
        gbig[:, 0, :, j, 0, j, :] = gr
        gbig[:, 0, :, j, 1, j, :] = -gi
        gbig[:, 1, :, j, 0, j, :] = gi
        gbig[:, 1, :, j, 1, j, :] = gr
    gbig = gbig.reshape(nblk, 2 * rows, 2 * rows)
    c = np.arange(F_CH)
    angc = 2.0 * np.pi * np.outer(c, c) / F_CH
    eye2 = np.eye(MXU_DIM // F_CH)
    bdc = np.kron(eye2, np.cos(angc))
    bds = np.kron(eye2, np.sin(angc))
    to_bf16 = lambda a: jnp.asarray(a, dtype=F32).astype(BF16)
    return to_bf16(f1cs), to_bf16(gbig), to_bf16(bdc), to_bf16(bds)


def _fourier_mix(u, batch, n):
    f1cs, gbig, bdc, bds = _fft_tables(n)
    n1 = FFT_N1
    n2 = n // n1
    y = _fft_rows(f1cs, u.reshape(batch, n1, n2 * F_WIDTH))
    fo = _fft_cols(gbig, y.reshape(batch, 2, n1, n2, F_WIDTH), bdc, bds)
    return fo.reshape(batch * n, F_WIDTH)


def _mid_kernel(x_ref, fo_ref, ao_ref, gs_ref, gate_ref, wo_ref, g1_ref, shift_ref, scale_ref, wi_ref,
                x1_ref, u_ref, bgs_ref):
    gs = gs_ref[...].astype(F32)
    yf = (fo_ref[...] * gs[:, :F_WIDTH]).astype(BF16)
    ya = (ao_ref[...].astype(F32) * gs[:, F_WIDTH:]).astype(BF16)
    y = _dot(yf, wo_ref[:F_WIDTH, :]) + _dot(ya, wo_ref[F_WIDTH:, :])
    x1 = x_ref[...] + gate_ref[0] * y
    x1_ref[...] = x1
    h = _modulated(x1, g1_ref[...], shift_ref[0], scale_ref[0]).astype(BF16)
    w = x1.shape[-1]
    bg = _dot(h, wi_ref[:, 0 * w:1 * w])
    gg = _dot(h, wi_ref[:, 3 * w:4 * w])
    bgs_ref[...] = (bg * (gg * _sigmoid(gg))).astype(BF16)
    cg = _dot(h, wi_ref[:, 1 * w:2 * w])
    xt = _dot(h, wi_ref[:, 2 * w:3 * w])
    u_ref[...] = (cg * xt).astype(BF16)


def _mid(x2, fo, ao, gs, gate0, wo, g1, shift1, scale1, wi, n):
    rows, d = x2.shape
    tm = ROW_BLOCK
    bpb = n // tm
    row_spec = lambda w: pl.BlockSpec((tm, w), lambda i: (i, 0))
    per_batch = pl.BlockSpec((1, 1, d), lambda i: (i // bpb, 0, 0))
    return pl.pallas_call(
        _mid_kernel,
        grid=(rows // tm,),
        in_specs=[
            row_spec(d), row_spec(F_WIDTH), row_spec(ATT_WIDTH), row_spec(d), per_batch,
            _const_spec(wo.shape), _const_spec((1, d)), per_batch, per_batch, _const_spec(wi.shape),
        ],
        out_specs=[row_spec(d), row_spec(d), row_spec(d)],
        out_shape=[
            jax.ShapeDtypeStruct((rows, d), F32),
            jax.ShapeDtypeStruct((rows, d), BF16),
            jax.ShapeDtypeStruct((rows, d), BF16),
        ],
        compiler_params=_params(("parallel",)),
        name="mid",
    )(x2, fo, ao, gs, gate0, wo, g1, shift1, scale1, wi)


def _conv_out_kernel(x1_ref, u_ref, up_ref, un_ref, bgs_ref, cw_ref, gate_ref, wo_ref, o_ref, *, bpb):
    i = pl.program_id(0)
    tm = u_ref.shape[0]
    halo = up_ref.shape[0]
    u = u_ref[...].astype(F32)
    first = (i % bpb) == 0
    last = (i % bpb) == (bpb - 1)
    prev_row = jnp.where(first, 0.0, up_ref[halo - 1:halo, :].astype(F32))
    next_row = jnp.where(last, 0.0, un_ref[0:1, :].astype(F32))
    row = lax.broadcasted_iota(jnp.int32, u.shape, 0)
    u_prev = jnp.where(row == 0, prev_row, pltpu.roll(u, 1, axis=0))
    u_next = jnp.where(row == tm - 1, next_row, pltpu.roll(u, tm - 1, axis=0))
    cw = cw_ref[...]
    conv = cw[0:1] * u_prev + cw[1:2] * u + cw[2:3] * u_next
    z = (conv * bgs_ref[...].astype(F32)).astype(BF16)
    o_ref[...] = x1_ref[...] + gate_ref[0] * _dot(z, wo_ref[...])


def _conv_out(x1, u, bgs, cw, gate1, wo, n):
    rows, d = x1.shape
    tm = ROW_BLOCK
    bpb = n // tm
    halo = 16
    hb = tm // halo
    nhalo = rows // halo
    row_spec = lambda w: pl.BlockSpec((tm, w), lambda i: (i, 0))
    kern = functools.partial(_conv_out_kernel, bpb=bpb)
    return pl.pallas_call(
        kern,
        grid=(rows // tm,),
        in_specs=[
            row_spec(d), row_spec(d),
            pl.BlockSpec((halo, d), lambda i: (jnp.maximum(i * hb - 1, 0), 0)),
            pl.BlockSpec((halo, d), lambda i: (jnp.minimum((i + 1) * hb, nhalo - 1), 0)),
            row_spec(d),
            _const_spec(cw.shape),
            pl.BlockSpec((1, 1, d), lambda i: (i // bpb, 0, 0)),
            _const_spec(wo.shape),
        ],
        out_specs=row_spec(d),
        out_shape=jax.ShapeDtypeStruct((rows, d), F32),
        compiler_params=_params(("parallel",)),
        name="conv_out",
    )(x1, u, u, u, bgs, cw, gate1, wo)


@functools.lru_cache(maxsize=None)
def _rope_tables(n):
    rows = n // GRID_W
    pos_r = np.repeat(np.arange(rows, dtype=np.float64), GRID_W)
    pos_c = np.tile(np.arange(GRID_W, dtype=np.float64), rows)
    half = DH // 2
    freqs = ROPE_BASE ** (-np.arange(0, half, 2, dtype=np.float64) / half)
    ang_r = pos_r[:, None] * freqs
    ang_c = pos_c[:, None] * freqs
    cos64 = np.concatenate([np.cos(ang_r), np.cos(ang_r), np.cos(ang_c), np.cos(ang_c)], axis=-1)
    sin64 = np.concatenate([-np.sin(ang_r), np.sin(ang_r), -np.sin(ang_c), np.sin(ang_c)], axis=-1)
    cos = np.concatenate([cos64, cos64], axis=-1).astype(np.float32)
    sin = np.concatenate([sin64, sin64], axis=-1).astype(np.float32)
    return jnp.asarray(cos), jnp.asarray(sin)


@functools.lru_cache(maxsize=None)
def _group_sum_matrix():
    return jnp.asarray(np.kron(np.eye(MXU_DIM // DH), np.ones((DH, DH))), dtype=BF16)


def _lambda_init(layer_idx):
    return 0.8 - 0.6 * math.exp(-0.3 * layer_idx)


def kernel(x, c, ctx, c_ctx, norm_g, ada_w, ada_b, even_w_in, even_q_norm, even_k_norm, even_lambda_q1,
           even_lambda_k1, even_lambda_q2, even_lambda_k2, even_subln, even_w_out, odd_w_in, odd_conv_w,
           odd_w_out):
    batch, n, d = x.shape
    m_ctx = ctx.shape[1]
    assert d == D_MODEL and n % (FFT_N1 * FFT_KB) == 0 and n % ROW_BLOCK == 0 and n % GRID_W == 0
    assert batch < 8 and norm_g.shape[0] == 2

    cvec = jnp.zeros((8, d), F32).at[:batch].set(c).at[batch].set(c_ctx)
    mods = _mod_params(cvec, ada_w, ada_b)
    shift0, scale0, gate0 = [mods[0, :, j * d:(j + 1) * d].reshape(8, 1, d) for j in range(3)]
    shift1, scale1, gate1 = [mods[1, :, j * d:(j + 1) * d].reshape(8, 1, d) for j in range(3)]

    w_in = even_w_in[0].astype(BF16)
    wf = w_in[:, 0:F_WIDTH]
    wq = w_in[:, F_WIDTH:F_WIDTH + QK_WIDTH]
    wk = w_in[:, F_WIDTH + QK_WIDTH:F_WIDTH + 2 * QK_WIDTH]
    wv = w_in[:, F_WIDTH + 2 * QK_WIDTH:F_WIDTH + 2 * QK_WIDTH + ATT_WIDTH]
    wg = w_in[:, F_WIDTH + 2 * QK_WIDTH + ATT_WIDTH:]
    reps = QK_WIDTH // DH
    qg = (jnp.tile(even_q_norm[0].astype(F32), reps) * (ATTN_SCALE * LOG2E)).reshape(1, QK_WIDTH)
    kg = jnp.tile(even_k_norm[0].astype(F32), reps).reshape(1, QK_WIDTH)
    bd = _group_sum_matrix()
    cos, sin = _rope_tables(n)
    g0 = norm_g[0].reshape(1, d)
    g1 = norm_g[1].reshape(1, d)

    x2 = x.reshape(batch * n, d)
    u, q, k, vt, gs = _inproj0(x2, shift0, scale0, g0, cos, sin, qg, kg, bd, wf, wq, wk, wv, wg, batch, n)
    kc, vtc = _ctx_kv(ctx.reshape(batch * m_ctx, d), shift0[batch], scale0[batch], g0, kg, bd, wk, wv,
                      batch, m_ctx)

    k_all = jnp.concatenate([k.reshape(batch, n, QK_WIDTH), kc.reshape(batch, m_ctx, QK_WIDTH)], axis=1)
    vt_all = jnp.concatenate([vt, vtc], axis=-1)
    lam_p = jnp.stack([even_lambda_q1[0], even_lambda_k1[0], even_lambda_q2[0], even_lambda_k2[0]]).astype(F32)
    sg = even_subln[0].astype(F32).reshape(1, DV)
    ao = _diff_attention(lam_p, q.reshape(batch, n, QK_WIDTH), k_all, vt_all, sg, _lambda_init(0))

    fo = _fourier_mix(u, batch, n)

    x1, u1, bgs = _mid(x2, fo, ao.reshape(batch * n, ATT_WIDTH), gs, gate0, even_w_out[0].astype(BF16), g1,
                       shift1, scale1, odd_w_in[0].astype(BF16), n)
    out = _conv_out(x1, u1, bgs, odd_conv_w[0].astype(F32), gate1, odd_w_out[0].astype(BF16), n)
    return out.reshape(batch, n, d)
```

```python
import functools
import math

import numpy as np
import jax
import jax.numpy as jnp
from jax import lax
from jax.experimental import pallas as pl
from jax.experimental.pallas import tpu as pltpu

F32 = jnp.float32
BF16 = jnp.bfloat16
F8 = jnp.float8_e4m3fn
F8_MAX = 448.0

D_MODEL = 1024
GRID_W = 64
F_GROUPS = 4
F_CH = 128
F_WIDTH = F_GROUPS * F_CH
H_DIFF = 4
DH = 64
DV = 128
QK_WIDTH = H_DIFF * 2 * DH
ATT_WIDTH = H_DIFF * DV
QK8_WIDTH = H_DIFF * 2 * 4 * DH
CONV_K = 3
ROPE_BASE = 10000.0
EPS = 1e-6
ATTN_SCALE = DH ** -0.5
LOG2E = 1.4426950408889634

LANES = 128
MXU_DIM = 256
VMEM_LIMIT = 56 * 1024 * 1024

ROW_BLOCK = 512
HALO = 16
FFT_N1 = 128
FFT_KB = 8
Q_BLOCK_ATT = 512
KEY_CHUNK_MAX = 512
ATTN_SCHED_FLAGS = None
NEG_BIG = -1e30
SAFE_SUM_LO = 2.0 ** -60
SAFE_SUM_HI = 2.0 ** 100
SUM_ROWS = 16


def _dot(a, b):
    return jnp.dot(a, b, preferred_element_type=F32)


def _dot_nt(a, b):
    return lax.dot_general(a, b, (((1,), (1,)), ((), ())), preferred_element_type=F32)


def _sigmoid(x):
    return 1.0 / (1.0 + jnp.exp(-x))


def _params(sem, flags=None):
    return pltpu.CompilerParams(dimension_semantics=sem, vmem_limit_bytes=VMEM_LIMIT, flags=flags)


def _mod_kernel(c_ref, w_ref, b_ref, o_ref):
    cv = c_ref[...]
    a = cv * _sigmoid(cv)
    w = w_ref[0]
    a_hi = a.astype(BF16)
    a_lo = (a - a_hi.astype(F32)).astype(BF16)
    w_hi = w.astype(BF16)
    w_lo = (w - w_hi.astype(F32)).astype(BF16)
    acc = _dot(a_hi, w_hi) + _dot(a_hi, w_lo) + _dot(a_lo, w_hi)
    o_ref[0] = acc + b_ref[0]


def _mod_params(cvec, ada_w, ada_b):
    depth, d, d3 = ada_w.shape
    tn = 1024
    return pl.pallas_call(
        _mod_kernel,
        grid=(depth, d3 // tn),
        in_specs=[
            pl.BlockSpec((8, d), lambda l, j: (0, 0)),
            pl.BlockSpec((1, d, tn), lambda l, j: (l, 0, j)),
            pl.BlockSpec((1, 1, tn), lambda l, j: (l, 0, j)),
        ],
        out_specs=pl.BlockSpec((1, 8, tn), lambda l, j: (l, 0, j)),
        out_shape=jax.ShapeDtypeStruct((depth, 8, d3), F32),
        compiler_params=_params(("parallel", "parallel")),
        name="mod_params",
    )(cvec, ada_w, ada_b.reshape(depth, 1, d3))


def _modulated(x, g, shift, scale):
    ms = jnp.mean(x * x, axis=-1, keepdims=True)
    y = x * lax.rsqrt(ms + EPS) * g
    return y * (1.0 + scale) + shift


def _group_norm(t, gain, bd):
    sq = (t * t).astype(BF16)
    half = MXU_DIM
    ss = jnp.concatenate([_dot(sq[:, :half], bd), _dot(sq[:, half:], bd)], axis=-1)
    return t * lax.rsqrt(ss * (1.0 / DH) + EPS) * gain


def _rope(t, cos, sin):
    width = t.shape[-1]
    quarter = DH // 4
    lane = lax.broadcasted_iota(jnp.int32, t.shape, 1)
    from_above = pltpu.roll(t, width - quarter, axis=1)
    from_below = pltpu.roll(t, quarter, axis=1)
    partner = jnp.where((lane & (2 * quarter - 1)) < quarter, from_above, from_below)
    reps = width // cos.shape[-1]
    cosf = jnp.concatenate([cos] * reps, axis=-1)
    sinf = jnp.concatenate([sin] * reps, axis=-1)
    return t * cosf + partner * sinf


def _store_split8(t, out_ref, query_side):
    for hh in range(H_DIFF):
        x = t[:, hh * 2 * DH:(hh + 1) * 2 * DH]
        hi = x.astype(F8).astype(F32)
        lo = x - hi
        hi_sw = pltpu.roll(hi, DH, axis=1)
        lo_sw = pltpu.roll(lo, DH, axis=1)
        first = lax.broadcasted_iota(jnp.int32, x.shape, 1) < DH
        if query_side:
            pieces = (jnp.where(first, hi, hi_sw), jnp.where(first, lo, lo_sw),
                      jnp.where(first, hi_sw, hi), jnp.where(first, lo_sw, lo))
        else:
            m1 = jnp.where(first, hi, lo_sw)
            m2 = jnp.where(first, hi_sw, lo)
            pieces = (m1, m1, m2, m2)
        base = hh * 4 * LANES
        for j, piece in enumerate(pieces):
            out_ref[:, base + j * LANES:base + (j + 1) * LANES] = piece.astype(F8)


def _inproj_kernel(x_ref, shift_ref, scale_ref, g_ref, cos_ref, sin_ref, qg_ref, kg_ref, bd_ref,
                   wf_ref, wq_ref, wk_ref, wv_ref, wg_ref,
                   u_ref, q_ref, k_ref, vt_ref, gs_ref):
    h = _modulated(x_ref[...], g_ref[...], shift_ref[0], scale_ref[0]).astype(BF16)
    cos = cos_ref[...]
    sin = sin_ref[...]
    bd = bd_ref[...]
    u_ref[...] = _dot(h, wf_ref[...]).astype(BF16)
    q = _rope(_group_norm(_dot(h, wq_ref[...]), qg_ref[...], bd), cos, sin)
    _store_split8(q, q_ref, True)
    k = _rope(_group_norm(_dot(h, wk_ref[...]), kg_ref[...], bd), cos, sin)
    _store_split8(k, k_ref, False)
    v = _dot(h, wv_ref[...])
    for hh in range(H_DIFF):
        vt_ref[0, hh] = v[:, hh * DV:(hh + 1) * DV].T.astype(BF16)
    g = _dot(h, wg_ref[...])
    gs_ref[...] = (g * _sigmoid(g)).astype(BF16)


def _ctx_kv_kernel(x_ref, shift_ref, scale_ref, g_ref, kg_ref, bd_ref, wk_ref, wv_ref, k_ref, vt_ref):
    h = _modulated(x_ref[...], g_ref[...], shift_ref[...], scale_ref[...]).astype(BF16)
    k = _group_norm(_dot(h, wk_ref[...]), kg_ref[...], bd_ref[...])
    _store_split8(k, k_ref, False)
    v = _dot(h, wv_ref[...])
    for hh in range(H_DIFF):
        vt_ref[0, hh] = v[:, hh * DV:(hh + 1) * DV].T.astype(BF16)


def _const_spec(shape):
    zeros = (0,) * len(shape)
    return pl.BlockSpec(shape, lambda *_: zeros)


def _inproj0(x2, shift, scale, g, cos, sin, qg, kg, bd, wf, wq, wk, wv, wg, batch, n):
    rows, d = x2.shape
    tm = ROW_BLOCK
    bpb = n // tm
    row_spec = lambda w: pl.BlockSpec((tm, w), lambda i: (i, 0))
    return pl.pallas_call(
        _inproj_kernel,
        grid=(rows // tm,),
        in_specs=[
            row_spec(d),
            pl.BlockSpec((1, 1, d), lambda i: (i // bpb, 0, 0)),
            pl.BlockSpec((1, 1, d), lambda i: (i // bpb, 0, 0)),
            _const_spec((1, d)),
            pl.BlockSpec((tm, LANES), lambda i: (i % bpb, 0)),
            pl.BlockSpec((tm, LANES), lambda i: (i % bpb, 0)),
            _const_spec((1, QK_WIDTH)),
            _const_spec((1, QK_WIDTH)),
            _const_spec((MXU_DIM, MXU_DIM)),
            _const_spec(wf.shape), _const_spec(wq.shape), _const_spec(wk.shape),
            _const_spec(wv.shape), _const_spec(wg.shape),
        ],
        out_specs=[
            row_spec(F_WIDTH), row_spec(QK8_WIDTH), row_spec(QK8_WIDTH),
            pl.BlockSpec((1, H_DIFF, DV, tm), lambda i: (i // bpb, 0, 0, i % bpb)),
            row_spec(d),
        ],
        out_shape=[
            jax.ShapeDtypeStruct((rows, F_WIDTH), BF16),
            jax.ShapeDtypeStruct((rows, QK8_WIDTH), F8),
            jax.ShapeDtypeStruct((rows, QK8_WIDTH), F8),
            jax.ShapeDtypeStruct((batch, H_DIFF, DV, n), BF16),
            jax.ShapeDtypeStruct((rows, d), BF16),
        ],
        compiler_params=_params(("parallel",)),
        name="inproj0",
    )(x2, shift, scale, g, cos, sin, qg, kg, bd, wf, wq, wk, wv, wg)


def _ctx_kv(ctx2, shift, scale, g, kg, bd, wk, wv, batch, m_ctx):
    rows, d = ctx2.shape
    return pl.pallas_call(
        _ctx_kv_kernel,
        grid=(batch,),
        in_specs=[
            pl.BlockSpec((m_ctx, d), lambda b: (b, 0)),
            _const_spec((1, d)), _const_spec((1, d)), _const_spec((1, d)),
            _const_spec((1, QK_WIDTH)), _const_spec((MXU_DIM, MXU_DIM)),
            _const_spec(wk.shape), _const_spec(wv.shape),
        ],
        out_specs=[
            pl.BlockSpec((m_ctx, QK8_WIDTH), lambda b: (b, 0)),
            pl.BlockSpec((1, H_DIFF, DV, m_ctx), lambda b: (b, 0, 0, 0)),
        ],
        out_shape=[
            jax.ShapeDtypeStruct((rows, QK8_WIDTH), F8),
            jax.ShapeDtypeStruct((batch, H_DIFF, DV, m_ctx), BF16),
        ],
        compiler_params=_params(("parallel",)),
        name="ctx_kv",
    )(ctx2, shift, scale, g, kg, bd, wk, wv)


def _attn_kernel(cs_ref, lam_ref, q_ref, k_ref, kc_ref, vt_ref, vtc_ref, sg_ref, o_ref,
                 acc1_ref, acc2_ref, b1a_ref, b1b_ref, b1c_ref, b2a_ref, b2b_ref, b2c_ref, *, tk, lam_init):
    tq = q_ref.shape[1]
    n_lat = k_ref.shape[1] // tk
    m_ctx = kc_ref.shape[1]
    kw = 4 * DH
    qm = (q_ref[0, :, :kw], q_ref[0, :, kw:])
    to_log2 = cs_ref[0]
    acc_refs = (acc1_ref, acc2_ref)
    buf_refs = ((b1a_ref, b1b_ref, b1c_ref), (b2a_ref, b2b_ref, b2c_ref))
    chunks = [(k_ref, vt_ref, j * tk, tk) for j in range(n_lat)] + [(kc_ref, vtc_ref, 0, m_ctx)]

    def finish(l1, l2):
        lp = lam_ref[...]
        lam = (jnp.exp(jnp.sum(lp[0:1] * lp[1:2], axis=-1, keepdims=True))
               - jnp.exp(jnp.sum(lp[2:3] * lp[3:4], axis=-1, keepdims=True)) + lam_init)
        ot = acc1_ref[0:DV, :] / l1 - lam * (acc2_ref[0:DV, :] / l2)
        o = ot.T
        ms = jnp.mean(o * o, axis=-1, keepdims=True)
        o = o * lax.rsqrt(ms + EPS) * sg_ref[...] * (1.0 - lam_init)
        o_ref[0] = o.astype(BF16)

    def scores(kr, k0, size, mp):
        kc = kr[0, pl.ds(k0, size), mp * kw:(mp + 1) * kw]
        return _dot_nt(kc, qm[mp]) * to_log2

    def store_scores(c):
        kr, _, k0, size = chunks[c]
        for mp in range(2):
            buf_refs[mp][c % 3][0:size, :] = scores(kr, k0, size, mp).astype(BF16)

    def exponentiate(c):
        size = chunks[c][3]
        for mp in range(2):
            buf_refs[mp][c % 3][0:size, :] = jnp.exp2(buf_refs[mp][c % 3][0:size, :])

    def pv(c):
        _, vr, k0, size = chunks[c]
        ones_rows = (lax.broadcasted_iota(jnp.int32, (SUM_ROWS, size), 0) == 0).astype(BF16)
        vc = jnp.concatenate([vr[0, 0, :, k0:k0 + size], ones_rows], axis=0)
        for mp in range(2):
            upd = _dot(vc, buf_refs[mp][c % 3][0:size, :])
            acc_refs[mp][...] = upd if c == 0 else acc_refs[mp][...] + upd

    n_chunks = len(chunks)
    for c in range(-1, n_chunks + 1):
        if 0 <= c + 1 < n_chunks:
            store_scores(c + 1)
        if 0 <= c < n_chunks:
            exponentiate(c)
        if 0 <= c - 1 < n_chunks:
            pv(c - 1)
    l = (acc1_ref[DV:DV + 1, :], acc2_ref[DV:DV + 1, :])
    finish(l[0], l[1])

    in_range = jnp.logical_and(jnp.minimum(l[0], l[1]) > SAFE_SUM_LO, jnp.maximum(l[0], l[1]) < SAFE_SUM_HI)
    all_safe = jnp.min(jnp.where(in_range, 1.0, 0.0)) > 0.5

    @pl.when(jnp.logical_not(all_safe))
    def _():
        def online_step(kr, vc, k0, size, carry):
            out = []
            for mp in range(2):
                m_old, l_old = carry[mp]
                s = scores(kr, k0, size, mp)
                m_new = jnp.maximum(m_old, jnp.max(s, axis=0, keepdims=True))
                alpha = jnp.exp2(m_old - m_new)
                p = jnp.exp2(s - m_new)
                acc_refs[mp][0:DV, :] = alpha * acc_refs[mp][0:DV, :] + _dot(vc, p.astype(BF16))
                out.append((m_new, alpha * l_old + jnp.sum(p, axis=0, keepdims=True)))
            return tuple(out)

        def body(c, carry):
            k0 = pl.multiple_of(c * tk, tk)
            return online_step(k_ref, vt_ref[0, 0, :, pl.ds(k0, tk)], k0, tk, carry)

        acc1_ref[...] = jnp.zeros_like(acc1_ref)
        acc2_ref[...] = jnp.zeros_like(acc2_ref)
        start = (jnp.full((1, tq), NEG_BIG, F32), jnp.zeros((1, tq), F32))
        carry = lax.fori_loop(0, n_lat, body, (start, start))
        carry = online_step(kc_ref, vtc_ref[0, 0], 0, m_ctx, carry)
        finish(carry[0][1], carry[1][1])


def _pick_key_chunk(n_keys):
    best = LANES
    for cand in range(LANES, KEY_CHUNK_MAX + 1, LANES):
        if n_keys % cand == 0:
            best = cand
    return best


def _diff_attention(to_log2, lam_p, q, k, kc, vt, vtc, sg, lam_init):
    batch, n, _ = q.shape
    m_ctx = kc.shape[1]
    tq = Q_BLOCK_ATT
    tk = _pick_key_chunk(n)
    hw = QK8_WIDTH // H_DIFF
    kern = functools.partial(_attn_kernel, tk=tk, lam_init=lam_init)
    return pl.pallas_call(
        kern,
        grid=(batch, H_DIFF, n // tq),
        in_specs=[
            pl.BlockSpec(memory_space=pltpu.SMEM),
            _const_spec((4, DH)),
            pl.BlockSpec((1, tq, hw), lambda b, h, i: (b, i, h)),
            pl.BlockSpec((1, n, hw), lambda b, h, i: (b, 0, h)),
            pl.BlockSpec((1, m_ctx, hw), lambda b, h, i: (b, 0, h)),
            pl.BlockSpec((1, 1, DV, n), lambda b, h, i: (b, h, 0, 0)),
            pl.BlockSpec((1, 1, DV, m_ctx), lambda b, h, i: (b, h, 0, 0)),
            _const_spec((1, DV)),
        ],
        out_specs=pl.BlockSpec((1, tq, DV), lambda b, h, i: (b, i, h)),
        out_shape=jax.ShapeDtypeStruct((batch, n, ATT_WIDTH), BF16),
        scratch_shapes=([pltpu.VMEM((DV + SUM_ROWS, tq), F32)] * 2
                        + [pltpu.VMEM((max(tk, m_ctx), tq), BF16)] * 6),
        compiler_params=_params(("parallel", "parallel", "arbitrary"), ATTN_SCHED_FLAGS),
        name="diff_attn",
    )(to_log2, lam_p, q, k, kc, vt, vtc, sg)


def _fft_rows_kernel(f_ref, u_ref, y_ref):
    y_ref[0] = _dot(f_ref[...], u_ref[0]).astype(BF16)


def _fft_rows(f1cs, u_view):
    batch, n1, cols = u_view.shape
    ch = min(cols, 4096)
    return pl.pallas_call(
        _fft_rows_kernel,
        grid=(batch, cols // ch),
        in_specs=[
            _const_spec(f1cs.shape),
            pl.BlockSpec((1, n1, ch), lambda b, j: (b, 0, j)),
        ],
        out_specs=pl.BlockSpec((1, 2 * n1, ch), lambda b, j: (b, 0, j)),
        out_shape=jax.ShapeDtypeStruct((batch, 2 * n1, cols), BF16),
        compiler_params=_params(("parallel", "parallel")),
        name="fft_rows",
    )(f1cs, u_view)


def _fft_cols_kernel(g_ref, y_ref, bdc_ref, bds_ref, o_ref):
    _, _, kb, n2, width = y_ref.shape
    rows = kb * n2
    y = y_ref[0].reshape(2 * rows, width)
    z = _dot(g_ref[0], y)
    xr = z[:rows].astype(BF16)
    xi = z[rows:].astype(BF16)
    bdc = bdc_ref[...]
    bds = bds_ref[...]
    half = MXU_DIM
    o = jnp.concatenate(
        [_dot(xr[:, :half], bdc) + _dot(xi[:, :half], bds),
         _dot(xr[:, half:], bdc) + _dot(xi[:, half:], bds)], axis=-1)
    o_ref[0] = o.reshape(n2, kb, width)


def _fft_cols(gbig, y5, bdc, bds):
    batch, _, n1, n2, width = y5.shape
    kb = FFT_KB
    gr = gbig.shape[1]
    return pl.pallas_call(
        _fft_cols_kernel,
        grid=(n1 // kb, batch),
        in_specs=[
            pl.BlockSpec((1, gr, gr), lambda j, b: (j, 0, 0)),
            pl.BlockSpec((1, 2, kb, n2, width), lambda j, b: (b, 0, j, 0, 0)),
            _const_spec(bdc.shape), _const_spec(bds.shape),
        ],
        out_specs=pl.BlockSpec((1, n2, kb, width), lambda j, b: (b, 0, j, 0)),
        out_shape=jax.ShapeDtypeStruct((batch, n2, n1, width), F32),
        compiler_params=_params(("parallel", "arbitrary")),
        name="fft_cols",
    )(gbig, y5, bdc, bds)


@functools.lru_cache(maxsize=None)
def _fft_tables(n):
    n1 = FFT_N1
    n2 = n // n1
    kb = FFT_KB
    k1 = np.arange(n1)
    ang = 2.0 * np.pi * np.outer(k1, k1) / n1
    f1cs = np.concatenate([np.cos(ang), -np.sin(ang)], axis=0)
    scale = 1.0 / math.sqrt(n * F_CH)
    nblk = n1 // kb
    rows = n2 * kb
    gbig = np.zeros((nblk, 2, n2, kb, 2, kb, n2), np.float32)
    n2i = np.arange(n2)
    for j in range(kb):
        kk = (np.arange(nblk)[:, None] * kb + j) + n1 * n2i[None, :]
        a = 2.0 * np.pi * (kk[:, :, None] * n2i[None, None, :] % n) / n
        gr = np.cos(a) * scale
        gi = -np.sin(a) * scale
        gbig[:, 0, :, j, 0, j, :] = gr
        gbig[:, 0, :, j, 1, j, :] = -gi
        gbig[:, 1, :, j, 0, j, :] = gi
        gbig[:, 1, :, j, 1, j, :] = gr
    gbig = gbig.reshape(nblk, 2 * rows, 2 * rows)
    c = np.arange(F_CH)
    angc = 2.0 * np.pi * np.outer(c, c) / F_CH
    eye2 = np.eye(MXU_DIM // F_CH)
    bdc = np.kron(eye2, np.cos(angc))
    bds = np.kron(eye2, np.sin(angc))
    to_bf16 = lambda a: jnp.asarray(a, dtype=F32).astype(BF16)
    return to_bf16(f1cs), to_bf16(gbig), to_bf16(bdc), to_bf16(bds)


def _fourier_mix(u, batch, n):
    f1cs, gbig, bdc, bds = _fft_tables(n)
    n1 = FFT_N1
    n2 = n // n1
    y = _fft_rows(f1cs, u.reshape(batch, n1, n2 * F_WIDTH))
    fo = _fft_cols(gbig, y.reshape(batch, 2, n1, n2, F_WIDTH), bdc, bds)
    return fo.reshape(batch * n, F_WIDTH)


def _tail_kernel(xp_ref, x_ref, xn_ref, fp_ref, f_ref, fn_ref, ap_ref, a_ref, an_ref, gp_ref, g_ref, gn_ref,
                 gate0_ref, wo0_ref, g1_ref, shift_ref, scale_ref, wi_ref, cw_ref, gate1_ref, wo1_ref,
                 o_ref, *, bpb):
    i = pl.program_id(0)
    tm = x_ref.shape[0]
    halo = xp_ref.shape[0]
    ext = tm + 2 * halo
    cat = lambda p, m, nx: jnp.concatenate([p[...], m[...], nx[...]], axis=0)
    x = cat(xp_ref, x_ref, xn_ref)
    gs = cat(gp_ref, g_ref, gn_ref).astype(F32)
    yf = (cat(fp_ref, f_ref, fn_ref) * gs[:, :F_WIDTH]).astype(BF16)
    ya = (cat(ap_ref, a_ref, an_ref).astype(F32) * gs[:, F_WIDTH:]).astype(BF16)
    y = _dot(yf, wo0_ref[:F_WIDTH, :]) + _dot(ya, wo0_ref[F_WIDTH:, :])
    x1 = x + gate0_ref[0] * y
    h = _modulated(x1, g1_ref[...], shift_ref[0], scale_ref[0]).astype(BF16)
    w = x.shape[-1]
    u = _dot(h, wi_ref[:, 1 * w:2 * w]) * _dot(h, wi_ref[:, 2 * w:3 * w])
    row = lax.broadcasted_iota(jnp.int32, u.shape, 0)
    lo = jnp.where((i % bpb) == 0, halo, 0)
    hi = jnp.where((i % bpb) == bpb - 1, halo + tm, ext)
    u = jnp.where(jnp.logical_and(row >= lo, row < hi), u, 0.0)
    cw = cw_ref[...]
    conv = (cw[0:1] * pltpu.roll(u, 1, axis=0)[halo:halo + tm]
            + cw[1:2] * u[halo:halo + tm]
            + cw[2:3] * pltpu.roll(u, ext - 1, axis=0)[halo:halo + tm])
    hm = h[halo:halo + tm]
    bg = _dot(hm, wi_ref[:, 0 * w:1 * w])
    gg = _dot(hm, wi_ref[:, 3 * w:4 * w])
    z = (bg * conv * (gg * _sigmoid(gg))).astype(BF16)
    o_ref[...] = x1[halo:halo + tm] + gate1_ref[0] * _dot(z, wo1_ref[...])


def _tail(x2, fo, ao, gs, gate0, wo0, g1, shift1, scale1, wi, cw, gate1, wo1, n):
    rows, d = x2.shape
    tm = ROW_BLOCK
    bpb = n // tm
    hb = tm // HALO
    nhalo = rows // HALO
    main = lambda w: pl.BlockSpec((tm, w), lambda i: (i, 0))
    prev = lambda w: pl.BlockSpec((HALO, w), lambda i: (jnp.maximum(i * hb - 1, 0), 0))
    nxt = lambda w: pl.BlockSpec((HALO, w), lambda i: (jnp.minimum((i + 1) * hb, nhalo - 1), 0))
    with_halo = lambda w: [prev(w), main(w), nxt(w)]
    per_batch = pl.BlockSpec((1, 1, d), lambda i: (i // bpb, 0, 0))
    kern = functools.partial(_tail_kernel, bpb=bpb)
    return pl.pallas_call(
        kern,
        grid=(rows // tm,),
        in_specs=(with_halo(d) + with_halo(F_WIDTH) + with_halo(ATT_WIDTH) + with_halo(d)
                  + [per_batch, _const_spec(wo0.shape), _const_spec((1, d)), per_batch, per_batch,
                     _const_spec(wi.shape), _const_spec(cw.shape), per_batch, _const_spec(wo1.shape)]),
        out_specs=main(d),
        out_shape=jax.ShapeDtypeStruct((rows, d), F32),
        compiler_params=_params(("parallel",)),
        name="tail",
    )(x2, x2, x2, fo, fo, fo, ao, ao, ao, gs, gs, gs, gate0, wo0, g1, shift1, scale1, wi, cw, gate1, wo1)


@functools.lru_cache(maxsize=None)
def _rope_tables(n):
    rows = n // GRID_W
    pos_r = np.repeat(np.arange(rows, dtype=np.float64), GRID_W)
    pos_c = np.tile(np.arange(GRID_W, dtype=np.float64), rows)
    half = DH // 2
    freqs = ROPE_BASE ** (-np.arange(0, half, 2, dtype=np.float64) / half)
    ang_r = pos_r[:, None] * freqs
    ang_c = pos_c[:, None] * freqs
    cos64 = np.concatenate([np.cos(ang_r), np.cos(ang_r), np.cos(ang_c), np.cos(ang_c)], axis=-1)
    sin64 = np.concatenate([-np.sin(ang_r), np.sin(ang_r), -np.sin(ang_c), np.sin(ang_c)], axis=-1)
    cos = np.concatenate([cos64, cos64], axis=-1).astype(np.float32)
    sin = np.concatenate([sin64, sin64], axis=-1).astype(np.float32)
    return jnp.asarray(cos), jnp.asarray(sin)


@functools.lru_cache(maxsize=None)
def _group_sum_matrix():
    return jnp.asarray(np.kron(np.eye(MXU_DIM // DH), np.ones((DH, DH))), dtype=BF16)


def _fp8_range_scale(gain):
    bound = math.sqrt(DH) * jnp.max(jnp.abs(gain))
    return jnp.exp2(jnp.clip(jnp.floor(jnp.log2(0.5 * F8_MAX / bound)), -60.0, 60.0))


def _lambda_init(layer_idx):
    return 0.8 - 0.6 * math.exp(-0.3 * layer_idx)


def kernel(x, c, ctx, c_ctx, norm_g, ada_w, ada_b, even_w_in, even_q_norm, even_k_norm, even_lambda_q1,
           even_lambda_k1, even_lambda_q2, even_lambda_k2, even_subln, even_w_out, odd_w_in, odd_conv_w,
           odd_w_out):
    batch, n, d = x.shape
    m_ctx = ctx.shape[1]
    assert d == D_MODEL and n % (FFT_N1 * FFT_KB) == 0 and n % ROW_BLOCK == 0 and n % GRID_W == 0
    assert batch < 8 and norm_g.shape[0] == 2

    cvec = jnp.zeros((8, d), F32).at[:batch].set(c).at[batch].set(c_ctx)
    mods = _mod_params(cvec, ada_w, ada_b)
    shift0, scale0, gate0 = [mods[0, :, j * d:(j + 1) * d].reshape(8, 1, d) for j in range(3)]
    shift1, scale1, gate1 = [mods[1, :, j * d:(j + 1) * d].reshape(8, 1, d) for j in range(3)]

    w_in = even_w_in[0].astype(BF16)
    wf = w_in[:, 0:F_WIDTH]
    wq = w_in[:, F_WIDTH:F_WIDTH + QK_WIDTH]
    wk = w_in[:, F_WIDTH + QK_WIDTH:F_WIDTH + 2 * QK_WIDTH]
    wv = w_in[:, F_WIDTH + 2 * QK_WIDTH:F_WIDTH + 2 * QK_WIDTH + ATT_WIDTH]
    wg = w_in[:, F_WIDTH + 2 * QK_WIDTH + ATT_WIDTH:]
    reps = QK_WIDTH // DH
    q_gain = even_q_norm[0].astype(F32)
    k_gain = even_k_norm[0].astype(F32)
    sq = _fp8_range_scale(q_gain)
    sk = _fp8_range_scale(k_gain)
    qg = (jnp.tile(q_gain, reps) * sq).reshape(1, QK_WIDTH)
    kg = (jnp.tile(k_gain, reps) * sk).reshape(1, QK_WIDTH)
    to_log2 = ((ATTN_SCALE * LOG2E) / (sq * sk)).reshape(1)
    bd = _group_sum_matrix()
    cos, sin = _rope_tables(n)
    g0 = norm_g[0].reshape(1, d)
    g1 = norm_g[1].reshape(1, d)

    x2 = x.reshape(batch * n, d)
    u, q, k, vt, gs = _inproj0(x2, shift0, scale0, g0, cos, sin, qg, kg, bd, wf, wq, wk, wv, wg, batch, n)
    kc, vtc = _ctx_kv(ctx.reshape(batch * m_ctx, d), shift0[batch], scale0[batch], g0, kg, bd, wk, wv,
                      batch, m_ctx)

    lam_p = jnp.stack([even_lambda_q1[0], even_lambda_k1[0], even_lambda_q2[0], even_lambda_k2[0]]).astype(F32)
    sg = even_subln[0].astype(F32).reshape(1, DV)
    ao = _diff_attention(to_log2, lam_p, q.reshape(batch, n, QK8_WIDTH), k.reshape(batch, n, QK8_WIDTH),
                         kc.reshape(batch, m_ctx, QK8_WIDTH), vt, vtc, sg, _lambda_init(0))

    fo = _fourier_mix(u, batch, n)

    out = _tail(x2, fo, ao.reshape(batch * n, ATT_WIDTH), gs, gate0, even_w_out[0].astype(BF16), g1, shift1,
                scale1, odd_w_in[0].astype(BF16), odd_conv_w[0].astype(F32), gate1, odd_w_out[0].astype(BF16), n)
    return out.reshape(batch, n, d)
```

```python
import functools
import math

import numpy as np
import jax
import jax.numpy as jnp
from jax import lax
from jax.experimental import pallas as pl
from jax.experimental.pallas import tpu as pltpu

F32 = jnp.float32
BF16 = jnp.bfloat16

D_MODEL = 1024
GRID_W = 64
F_GROUPS = 4
F_CH = 128
F_WIDTH = F_GROUPS * F_CH
H_DIFF = 4
DH = 64
DV = 128
QK_WIDTH = H_DIFF * 2 * DH
ATT_WIDTH = H_DIFF * DV
CONV_K = 3
ROPE_BASE = 10000.0
EPS = 1e-6
ATTN_SCALE = DH ** -0.5
LOG2E = 1.4426950408889634

LANES = 128
MXU_DIM = 256
VMEM_LIMIT = 56 * 1024 * 1024

ROW_BLOCK = 512
HALO = 16
FFT_N1 = 128
FFT_KB = 8
Q_BLOCK_ATT = 512
KEY_CHUNK_MAX = 1024
NEG_BIG = -1e30
SAFE_SUM_LO = 2.0 ** -60
SAFE_SUM_HI = 2.0 ** 100


def _dot(a, b):
    return jnp.dot(a, b, preferred_element_type=F32)


def _dot_nt(a, b):
    return lax.dot_general(a, b, (((1,), (1,)), ((), ())), preferred_element_type=F32)


def _sigmoid(x):
    return 1.0 / (1.0 + jnp.exp(-x))


def _params(sem, flags=None):
    return pltpu.CompilerParams(dimension_semantics=sem, vmem_limit_bytes=VMEM_LIMIT, flags=flags)


def _mod_kernel(c_ref, w_ref, b_ref, o_ref):
    cv = c_ref[...]
    a = cv * _sigmoid(cv)
    w = w_ref[0]
    a_hi = a.astype(BF16)
    a_lo = (a - a_hi.astype(F32)).astype(BF16)
    w_hi = w.astype(BF16)
    w_lo = (w - w_hi.astype(F32)).astype(BF16)
    acc = _dot(a_hi, w_hi) + _dot(a_hi, w_lo) + _dot(a_lo, w_hi)
    o_ref[0] = acc + b_ref[0]


def _mod_params(cvec, ada_w, ada_b):
    depth, d, d3 = ada_w.shape
    tn = 1024
    return pl.pallas_call(
        _mod_kernel,
        grid=(depth, d3 // tn),
        in_specs=[
            pl.BlockSpec((8, d), lambda l, j: (0, 0)),
            pl.BlockSpec((1, d, tn), lambda l, j: (l, 0, j)),
            pl.BlockSpec((1, 1, tn), lambda l, j: (l, 0, j)),
        ],
        out_specs=pl.BlockSpec((1, 8, tn), lambda l, j: (l, 0, j)),
        out_shape=jax.ShapeDtypeStruct((depth, 8, d3), F32),
        compiler_params=_params(("parallel", "parallel")),
        name="mod_params",
    )(cvec, ada_w, ada_b.reshape(depth, 1, d3))


def _modulated(x, g, shift, scale):
    ms = jnp.mean(x * x, axis=-1, keepdims=True)
    y = x * lax.rsqrt(ms + EPS) * g
    return y * (1.0 + scale) + shift


def _group_norm(t, gain, bd):
    sq = (t * t).astype(BF16)
    half = MXU_DIM
    ss = jnp.concatenate([_dot(sq[:, :half], bd), _dot(sq[:, half:], bd)], axis=-1)
    return t * lax.rsqrt(ss * (1.0 / DH) + EPS) * gain


def _rope(t, cos, sin):
    width = t.shape[-1]
    quarter = DH // 4
    lane = lax.broadcasted_iota(jnp.int32, t.shape, 1)
    from_above = pltpu.roll(t, width - quarter, axis=1)
    from_below = pltpu.roll(t, quarter, axis=1)
    partner = jnp.where((lane & (2 * quarter - 1)) < quarter, from_above, from_below)
    reps = width // cos.shape[-1]
    cosf = jnp.concatenate([cos] * reps, axis=-1)
    sinf = jnp.concatenate([sin] * reps, axis=-1)
    return t * cosf + partner * sinf


def _inproj_kernel(x_ref, shift_ref, scale_ref, g_ref, cos_ref, sin_ref, qg_ref, kg_ref, bd_ref,
                   wf_ref, wq_ref, wk_ref, wv_ref, wg_ref,
                   u_ref, q_ref, k_ref, vt_ref, gs_ref):
    h = _modulated(x_ref[...], g_ref[...], shift_ref[0], scale_ref[0]).astype(BF16)
    cos = cos_ref[...]
    sin = sin_ref[...]
    bd = bd_ref[...]
    u_ref[...] = _dot(h, wf_ref[...]).astype(BF16)
    q = _rope(_group_norm(_dot(h, wq_ref[...]), qg_ref[...], bd), cos, sin)
    q_ref[...] = q.astype(BF16)
    k = _rope(_group_norm(_dot(h, wk_ref[...]), kg_ref[...], bd), cos, sin)
    k_ref[...] = k.astype(BF16)
    v = _dot(h, wv_ref[...])
    for hh in range(H_DIFF):
        vt_ref[0, hh] = v[:, hh * DV:(hh + 1) * DV].T.astype(BF16)
    g = _dot(h, wg_ref[...])
    gs_ref[...] = (g * _sigmoid(g)).astype(BF16)


def _ctx_kv_kernel(x_ref, shift_ref, scale_ref, g_ref, kg_ref, bd_ref, wk_ref, wv_ref, k_ref, vt_ref):
    h = _modulated(x_ref[...], g_ref[...], shift_ref[...], scale_ref[...]).astype(BF16)
    k = _group_norm(_dot(h, wk_ref[...]), kg_ref[...], bd_ref[...])
    k_ref[...] = k.astype(BF16)
    v = _dot(h, wv_ref[...])
    for hh in range(H_DIFF):
        vt_ref[0, hh] = v[:, hh * DV:(hh + 1) * DV].T.astype(BF16)


def _const_spec(shape):
    zeros = (0,) * len(shape)
    return pl.BlockSpec(shape, lambda *_: zeros)


def _inproj0(x2, shift, scale, g, cos, sin, qg, kg, bd, wf, wq, wk, wv, wg, batch, n):
    rows, d = x2.shape
    tm = ROW_BLOCK
    bpb = n // tm
    row_spec = lambda w: pl.BlockSpec((tm, w), lambda i: (i, 0))
    return pl.pallas_call(
        _inproj_kernel,
        grid=(rows // tm,),
        in_specs=[
            row_spec(d),
            pl.BlockSpec((1, 1, d), lambda i: (i // bpb, 0, 0)),
            pl.BlockSpec((1, 1, d), lambda i: (i // bpb, 0, 0)),
            _const_spec((1, d)),
            pl.BlockSpec((tm, LANES), lambda i: (i % bpb, 0)),
            pl.BlockSpec((tm, LANES), lambda i: (i % bpb, 0)),
            _const_spec((1, QK_WIDTH)),
            _const_spec((1, QK_WIDTH)),
            _const_spec((MXU_DIM, MXU_DIM)),
            _const_spec(wf.shape), _const_spec(wq.shape), _const_spec(wk.shape),
            _const_spec(wv.shape), _const_spec(wg.shape),
        ],
        out_specs=[
            row_spec(F_WIDTH), row_spec(QK_WIDTH), row_spec(QK_WIDTH),
            pl.BlockSpec((1, H_DIFF, DV, tm), lambda i: (i // bpb, 0, 0, i % bpb)),
            row_spec(d),
        ],
        out_shape=[
            jax.ShapeDtypeStruct((rows, F_WIDTH), BF16),
            jax.ShapeDtypeStruct((rows, QK_WIDTH), BF16),
            jax.ShapeDtypeStruct((rows, QK_WIDTH), BF16),
            jax.ShapeDtypeStruct((batch, H_DIFF, DV, n), BF16),
            jax.ShapeDtypeStruct((rows, d), BF16),
        ],
        compiler_params=_params(("parallel",)),
        name="inproj0",
    )(x2, shift, scale, g, cos, sin, qg, kg, bd, wf, wq, wk, wv, wg)


def _ctx_kv(ctx2, shift, scale, g, kg, bd, wk, wv, batch, m_ctx):
    rows, d = ctx2.shape
    return pl.pallas_call(
        _ctx_kv_kernel,
        grid=(batch,),
        in_specs=[
            pl.BlockSpec((m_ctx, d), lambda b: (b, 0)),
            _const_spec((1, d)), _const_spec((1, d)), _const_spec((1, d)),
            _const_spec((1, QK_WIDTH)), _const_spec((MXU_DIM, MXU_DIM)),
            _const_spec(wk.shape), _const_spec(wv.shape),
        ],
        out_specs=[
            pl.BlockSpec((m_ctx, QK_WIDTH), lambda b: (b, 0)),
            pl.BlockSpec((1, H_DIFF, DV, m_ctx), lambda b: (b, 0, 0, 0)),
        ],
        out_shape=[
            jax.ShapeDtypeStruct((rows, QK_WIDTH), BF16),
            jax.ShapeDtypeStruct((batch, H_DIFF, DV, m_ctx), BF16),
        ],
        compiler_params=_params(("parallel",)),
        name="ctx_kv",
    )(ctx2, shift, scale, g, kg, bd, wk, wv)


def _attn_kernel(lam_ref, q_ref, k_ref, kc_ref, vt_ref, vtc_ref, sg_ref, o_ref,
                 acc1_ref, acc2_ref, p1a_ref, p1b_ref, p2a_ref, p2b_ref, *, tk, lam_init):
    tq = q_ref.shape[1]
    n_lat = k_ref.shape[1] // tk
    m_ctx = kc_ref.shape[1]
    q = q_ref[0]
    lane = lax.broadcasted_iota(jnp.int32, q.shape, 1)
    zero = jnp.zeros_like(q)
    qz = (jnp.where(lane < DH, q, zero), jnp.where(lane >= DH, q, zero))
    acc_refs = (acc1_ref, acc2_ref)
    p_refs = ((p1a_ref, p1b_ref), (p2a_ref, p2b_ref))
    chunks = [(k_ref, vt_ref, j * tk, tk) for j in range(n_lat)] + [(kc_ref, vtc_ref, 0, m_ctx)]

    def finish(l1, l2):
        lp = lam_ref[...]
        lam = (jnp.exp(jnp.sum(lp[0:1] * lp[1:2], axis=-1, keepdims=True))
               - jnp.exp(jnp.sum(lp[2:3] * lp[3:4], axis=-1, keepdims=True)) + lam_init)
        ot = acc1_ref[...] / l1 - lam * (acc2_ref[...] / l2)
        o = ot.T
        ms = jnp.mean(o * o, axis=-1, keepdims=True)
        o = o * lax.rsqrt(ms + EPS) * sg_ref[...] * (1.0 - lam_init)
        o_ref[0] = o.astype(BF16)

    def scores(kr, k0, size, mp):
        return _dot_nt(kr[0, pl.ds(k0, size), :], qz[mp])

    def exp_scores(c):
        kr, _, k0, size = chunks[c]
        sums = []
        for mp in range(2):
            p = jnp.exp2(scores(kr, k0, size, mp))
            p_refs[mp][c % 2][0:size, :] = p.astype(BF16)
            sums.append(jnp.sum(p, axis=0, keepdims=True))
        return sums

    def pv(c):
        _, vr, k0, size = chunks[c]
        vc = vr[0, 0, :, k0:k0 + size]
        for mp in range(2):
            upd = _dot(vc, p_refs[mp][c % 2][0:size, :])
            acc_refs[mp][...] = upd if c == 0 else acc_refs[mp][...] + upd

    l = exp_scores(0)
    for c in range(1, len(chunks)):
        sums = exp_scores(c)
        pv(c - 1)
        l = [l[mp] + sums[mp] for mp in range(2)]
    pv(len(chunks) - 1)
    finish(l[0], l[1])

    in_range = jnp.logical_and(jnp.minimum(l[0], l[1]) > SAFE_SUM_LO, jnp.maximum(l[0], l[1]) < SAFE_SUM_HI)
    all_safe = jnp.min(jnp.where(in_range, 1.0, 0.0)) > 0.5

    @pl.when(jnp.logical_not(all_safe))
    def _():
        def online_step(kr, vc, k0, size, carry):
            out = []
            for mp in range(2):
                m_old, l_old = carry[mp]
                s = scores(kr, k0, size, mp)
                m_new = jnp.maximum(m_old, jnp.max(s, axis=0, keepdims=True))
                alpha = jnp.exp2(m_old - m_new)
                p = jnp.exp2(s - m_new)
                acc_refs[mp][...] = alpha * acc_refs[mp][...] + _dot(vc, p.astype(BF16))
                out.append((m_new, alpha * l_old + jnp.sum(p, axis=0, keepdims=True)))
            return tuple(out)

        def body(c, carry):
            k0 = pl.multiple_of(c * tk, tk)
            return online_step(k_ref, vt_ref[0, 0, :, pl.ds(k0, tk)], k0, tk, carry)

        acc1_ref[...] = jnp.zeros_like(acc1_ref)
        acc2_ref[...] = jnp.zeros_like(acc2_ref)
        start = (jnp.full((1, tq), NEG_BIG, F32), jnp.zeros((1, tq), F32))
        carry = lax.fori_loop(0, n_lat, body, (start, start))
        carry = online_step(kc_ref, vtc_ref[0, 0], 0, m_ctx, carry)
        finish(carry[0][1], carry[1][1])


def _pick_key_chunk(n_keys):
    best = LANES
    for cand in range(LANES, KEY_CHUNK_MAX + 1, LANES):
        if n_keys % cand == 0:
            best = cand
    return best


def _diff_attention(lam_p, q, k, kc, vt, vtc, sg, lam_init):
    batch, n, _ = q.shape
    m_ctx = kc.shape[1]
    tq = Q_BLOCK_ATT
    tk = _pick_key_chunk(n)
    hw = 2 * DH
    kern = functools.partial(_attn_kernel, tk=tk, lam_init=lam_init)
    return pl.pallas_call(
        kern,
        grid=(batch, H_DIFF, n // tq),
        in_specs=[
            _const_spec((4, DH)),
            pl.BlockSpec((1, tq, hw), lambda b, h, i: (b, i, h)),
            pl.BlockSpec((1, n, hw), lambda b, h, i: (b, 0, h)),
            pl.BlockSpec((1, m_ctx, hw), lambda b, h, i: (b, 0, h)),
            pl.BlockSpec((1, 1, DV, n), lambda b, h, i: (b, h, 0, 0)),
            pl.BlockSpec((1, 1, DV, m_ctx), lambda b, h, i: (b, h, 0, 0)),
            _const_spec((1, DV)),
        ],
        out_specs=pl.BlockSpec((1, tq, DV), lambda b, h, i: (b, i, h)),
        out_shape=jax.ShapeDtypeStruct((batch, n, ATT_WIDTH), BF16),
        scratch_shapes=[pltpu.VMEM((DV, tq), F32)] * 2 + [pltpu.VMEM((max(tk, m_ctx), tq), BF16)] * 4,
        compiler_params=_params(("parallel", "parallel", "arbitrary")),
        name="diff_attn",
    )(lam_p, q, k, kc, vt, vtc, sg)


def _fft_rows_kernel(f_ref, u_ref, y_ref):
    y_ref[0] = _dot(f_ref[...], u_ref[0]).astype(BF16)


def _fft_rows(f1cs, u_view):
    batch, n1, cols = u_view.shape
    ch = min(cols, 4096)
    return pl.pallas_call(
        _fft_rows_kernel,
        grid=(batch, cols // ch),
        in_specs=[
            _const_spec(f1cs.shape),
            pl.BlockSpec((1, n1, ch), lambda b, j: (b, 0, j)),
        ],
        out_specs=pl.BlockSpec((1, 2 * n1, ch), lambda b, j: (b, 0, j)),
        out_shape=jax.ShapeDtypeStruct((batch, 2 * n1, cols), BF16),
        compiler_params=_params(("parallel", "parallel")),
        name="fft_rows",
    )(f1cs, u_view)


def _fft_cols_kernel(g_ref, y_ref, bdc_ref, bds_ref, o_ref):
    _, _, kb, n2, width = y_ref.shape
    rows = kb * n2
    y = y_ref[0].reshape(2 * rows, width)
    z = _dot(g_ref[0], y)
    xr = z[:rows].astype(BF16)
    xi = z[rows:].astype(BF16)
    bdc = bdc_ref[...]
    bds = bds_ref[...]
    half = MXU_DIM
    o = jnp.concatenate(
        [_dot(xr[:, :half], bdc) + _dot(xi[:, :half], bds),
         _dot(xr[:, half:], bdc) + _dot(xi[:, half:], bds)], axis=-1)
    o_ref[0] = o.reshape(n2, kb, width)


def _fft_cols(gbig, y5, bdc, bds):
    batch, _, n1, n2, width = y5.shape
    kb = FFT_KB
    gr = gbig.shape[1]
    return pl.pallas_call(
        _fft_cols_kernel,
        grid=(n1 // kb, batch),
        in_specs=[
            pl.BlockSpec((1, gr, gr), lambda j, b: (j, 0, 0)),
            pl.BlockSpec((1, 2, kb, n2, width), lambda j, b: (b, 0, j, 0, 0)),
            _const_spec(bdc.shape), _const_spec(bds.shape),
        ],
        out_specs=pl.BlockSpec((1, n2, kb, width), lambda j, b: (b, 0, j, 0)),
        out_shape=jax.ShapeDtypeStruct((batch, n2, n1, width), F32),
        compiler_params=_params(("parallel", "arbitrary")),
        name="fft_cols",
    )(gbig, y5, bdc, bds)


@functools.lru_cache(maxsize=None)
def _fft_tables(n):
    n1 = FFT_N1
    n2 = n // n1
    kb = FFT_KB
    k1 = np.arange(n1)
    ang = 2.0 * np.pi * np.outer(k1, k1) / n1
    f1cs = np.concatenate([np.cos(ang), -np.sin(ang)], axis=0)
    scale = 1.0 / math.sqrt(n * F_CH)
    nblk = n1 // kb
    rows = n2 * kb
    gbig = np.zeros((nblk, 2, n2, kb, 2, kb, n2), np.float32)
    n2i = np.arange(n2)
    for j in range(kb):
        kk = (np.arange(nblk)[:, None] * kb + j) + n1 * n2i[None, :]
        a = 2.0 * np.pi * (kk[:, :, None] * n2i[None, None, :] % n) / n
        gr = np.cos(a) * scale
        gi = -np.sin(a) * scale
        gbig[:, 0, :, j, 0, j, :] = gr
        gbig[:, 0, :, j, 1, j, :] = -gi
        gbig[:, 1, :, j, 0, j, :] = gi
        gbig[:, 1, :, j, 1, j, :] = gr
    gbig = gbig.reshape(nblk, 2 * rows, 2 * rows)
    c = np.arange(F_CH)
    angc = 2.0 * np.pi * np.outer(c, c) / F_CH
    eye2 = np.eye(MXU_DIM // F_CH)
    bdc = np.kron(eye2, np.cos(angc))
    bds = np.kron(eye2, np.sin(angc))
    to_bf16 = lambda a: jnp.asarray(a, dtype=F32).astype(BF16)
    return to_bf16(f1cs), to_bf16(gbig), to_bf16(bdc), to_bf16(bds)


def _fourier_mix(u, batch, n):
    f1cs, gbig, bdc, bds = _fft_tables(n)
    n1 = FFT_N1
    n2 = n // n1
    y = _fft_rows(f1cs, u.reshape(batch, n1, n2 * F_WIDTH))
    fo = _fft_cols(gbig, y.reshape(batch, 2, n1, n2, F_WIDTH), bdc, bds)
    return fo.reshape(batch * n, F_WIDTH)


def _tail_kernel(xp_ref, x_ref, xn_ref, fp_ref, f_ref, fn_ref, ap_ref, a_ref, an_ref, gp_ref, g_ref, gn_ref,
                 gate0_ref, wo0_ref, g1_ref, shift_ref, scale_ref, wi_ref, cw_ref, gate1_ref, wo1_ref,
                 o_ref, *, bpb):
    i = pl.program_id(0)
    tm = x_ref.shape[0]
    halo = xp_ref.shape[0]
    ext = tm + 2 * halo
    cat = lambda p, m, nx: jnp.concatenate([p[...], m[...], nx[...]], axis=0)
    x = cat(xp_ref, x_ref, xn_ref)
    gs = cat(gp_ref, g_ref, gn_ref).astype(F32)
    yf = (cat(fp_ref, f_ref, fn_ref) * gs[:, :F_WIDTH]).astype(BF16)
    ya = (cat(ap_ref, a_ref, an_ref).astype(F32) * gs[:, F_WIDTH:]).astype(BF16)
    y = _dot(yf, wo0_ref[:F_WIDTH, :]) + _dot(ya, wo0_ref[F_WIDTH:, :])
    x1 = x + gate0_ref[0] * y
    h = _modulated(x1, g1_ref[...], shift_ref[0], scale_ref[0]).astype(BF16)
    w = x.shape[-1]
    u = _dot(h, wi_ref[:, 1 * w:2 * w]) * _dot(h, wi_ref[:, 2 * w:3 * w])
    row = lax.broadcasted_iota(jnp.int32, u.shape, 0)
    lo = jnp.where((i % bpb) == 0, halo, 0)
    hi = jnp.where((i % bpb) == bpb - 1, halo + tm, ext)
    u = jnp.where(jnp.logical_and(row >= lo, row < hi), u, 0.0)
    cw = cw_ref[...]
    conv = (cw[0:1] * pltpu.roll(u, 1, axis=0)[halo:halo + tm]
            + cw[1:2] * u[halo:halo + tm]
            + cw[2:3] * pltpu.roll(u, ext - 1, axis=0)[halo:halo + tm])
    hm = h[halo:halo + tm]
    bg = _dot(hm, wi_ref[:, 0 * w:1 * w])
    gg = _dot(hm, wi_ref[:, 3 * w:4 * w])
    z = (bg * conv * (gg * _sigmoid(gg))).astype(BF16)
    o_ref[...] = x1[halo:halo + tm] + gate1_ref[0] * _dot(z, wo1_ref[...])


def _tail(x2, fo, ao, gs, gate0, wo0, g1, shift1, scale1, wi, cw, gate1, wo1, n):
    rows, d = x2.shape
    tm = ROW_BLOCK
    bpb = n // tm
    hb = tm // HALO
    nhalo = rows // HALO
    main = lambda w: pl.BlockSpec((tm, w), lambda i: (i, 0))
    prev = lambda w: pl.BlockSpec((HALO, w), lambda i: (jnp.maximum(i * hb - 1, 0), 0))
    nxt = lambda w: pl.BlockSpec((HALO, w), lambda i: (jnp.minimum((i + 1) * hb, nhalo - 1), 0))
    with_halo = lambda w: [prev(w), main(w), nxt(w)]
    per_batch = pl.BlockSpec((1, 1, d), lambda i: (i // bpb, 0, 0))
    kern = functools.partial(_tail_kernel, bpb=bpb)
    return pl.pallas_call(
        kern,
        grid=(rows // tm,),
        in_specs=(with_halo(d) + with_halo(F_WIDTH) + with_halo(ATT_WIDTH) + with_halo(d)
                  + [per_batch, _const_spec(wo0.shape), _const_spec((1, d)), per_batch, per_batch,
                     _const_spec(wi.shape), _const_spec(cw.shape), per_batch, _const_spec(wo1.shape)]),
        out_specs=main(d),
        out_shape=jax.ShapeDtypeStruct((rows, d), F32),
        compiler_params=_params(("parallel",)),
        name="tail",
    )(x2, x2, x2, fo, fo, fo, ao, ao, ao, gs, gs, gs, gate0, wo0, g1, shift1, scale1, wi, cw, gate1, wo1)


@functools.lru_cache(maxsize=None)
def _rope_tables(n):
    rows = n // GRID_W
    pos_r = np.repeat(np.arange(rows, dtype=np.float64), GRID_W)
    pos_c = np.tile(np.arange(GRID_W, dtype=np.float64), rows)
    half = DH // 2
    freqs = ROPE_BASE ** (-np.arange(0, half, 2, dtype=np.float64) / half)
    ang_r = pos_r[:, None] * freqs
    ang_c = pos_c[:, None] * freqs
    cos64 = np.concatenate([np.cos(ang_r), np.cos(ang_r), np.cos(ang_c), np.cos(ang_c)], axis=-1)
    sin64 = np.concatenate([-np.sin(ang_r), np.sin(ang_r), -np.sin(ang_c), np.sin(ang_c)], axis=-1)
    cos = np.concatenate([cos64, cos64], axis=-1).astype(np.float32)
    sin = np.concatenate([sin64, sin64], axis=-1).astype(np.float32)
    return jnp.asarray(cos), jnp.asarray(sin)


@functools.lru_cache(maxsize=None)
def _group_sum_matrix():
    return jnp.asarray(np.kron(np.eye(MXU_DIM // DH), np.ones((DH, DH))), dtype=BF16)


def _lambda_init(layer_idx):
    return 0.8 - 0.6 * math.exp(-0.3 * layer_idx)


def kernel(x, c, ctx, c_ctx, norm_g, ada_w, ada_b, even_w_in, even_q_norm, even_k_norm, even_lambda_q1,
           even_lambda_k1, even_lambda_q2, even_lambda_k2, even_subln, even_w_out, odd_w_in, odd_conv_w,
           odd_w_out):
    batch, n, d = x.shape
    m_ctx = ctx.shape[1]
    assert d == D_MODEL and n % (FFT_N1 * FFT_KB) == 0 and n % ROW_BLOCK == 0 and n % GRID_W == 0
    assert batch < 8 and norm_g.shape[0] == 2

    cvec = jnp.zeros((8, d), F32).at[:batch].set(c).at[batch].set(c_ctx)
    mods = _mod_params(cvec, ada_w, ada_b)
    shift0, scale0, gate0 = [mods[0, :, j * d:(j + 1) * d].reshape(8, 1, d) for j in range(3)]
    shift1, scale1, gate1 = [mods[1, :, j * d:(j + 1) * d].reshape(8, 1, d) for j in range(3)]

    w_in = even_w_in[0].astype(BF16)
    wf = w_in[:, 0:F_WIDTH]
    wq = w_in[:, F_WIDTH:F_WIDTH + QK_WIDTH]
    wk = w_in[:, F_WIDTH + QK_WIDTH:F_WIDTH + 2 * QK_WIDTH]
    wv = w_in[:, F_WIDTH + 2 * QK_WIDTH:F_WIDTH + 2 * QK_WIDTH + ATT_WIDTH]
    wg = w_in[:, F_WIDTH + 2 * QK_WIDTH + ATT_WIDTH:]
    reps = QK_WIDTH // DH
    qg = (jnp.tile(even_q_norm[0].astype(F32), reps) * (ATTN_SCALE * LOG2E)).reshape(1, QK_WIDTH)
    kg = jnp.tile(even_k_norm[0].astype(F32), reps).reshape(1, QK_WIDTH)
    bd = _group_sum_matrix()
    cos, sin = _rope_tables(n)
    g0 = norm_g[0].reshape(1, d)
    g1 = norm_g[1].reshape(1, d)

    x2 = x.reshape(batch * n, d)
    u, q, k, vt, gs = _inproj0(x2, shift0, scale0, g0, cos, sin, qg, kg, bd, wf, wq, wk, wv, wg, batch, n)
    kc, vtc = _ctx_kv(ctx.reshape(batch * m_ctx, d), shift0[batch], scale0[batch], g0, kg, bd, wk, wv,
                      batch, m_ctx)

    lam_p = jnp.stack([even_lambda_q1[0], even_lambda_k1[0], even_lambda_q2[0], even_lambda_k2[0]]).astype(F32)
    sg = even_subln[0].astype(F32).reshape(1, DV)
    ao = _diff_attention(lam_p, q.reshape(batch, n, QK_WIDTH), k.reshape(batch, n, QK_WIDTH),
                         kc.reshape(batch, m_ctx, QK_WIDTH), vt, vtc, sg, _lambda_init(0))

    fo = _fourier_mix(u, batch, n)

    out = _tail(x2, fo, ao.reshape(batch * n, ATT_WIDTH), gs, gate0, even_w_out[0].astype(BF16), g1, shift1,
                scale1, odd_w_in[0].astype(BF16), odd_conv_w[0].astype(F32), gate1, odd_w_out[0].astype(BF16), n)
    return out.reshape(batch, n, d)
```

```python
import functools
import math

import numpy as np
import jax
import jax.numpy as jnp
from jax import lax
from jax.experimental import pallas as pl
from jax.experimental.pallas import tpu as pltpu

F32 = jnp.float32
BF16 = jnp.bfloat16

D_MODEL = 1024
GRID_W = 64
F_GROUPS = 4
F_CH = 128
F_WIDTH = F_GROUPS * F_CH
H_DIFF = 4
DH = 64
DV = 128
QK_WIDTH = H_DIFF * 2 * DH
ATT_WIDTH = H_DIFF * DV
CONV_K = 3
ROPE_BASE = 10000.0
EPS = 1e-6
ATTN_SCALE = DH ** -0.5
LOG2E = 1.4426950408889634

LANES = 128
MXU_DIM = 256
VMEM_LIMIT = 56 * 1024 * 1024

ROW_BLOCK = 1024
HALO = 16
FFT_N1 = 128
FFT_KB = 8
Q_BLOCK_ATT = 512
KEY_CHUNK_MAX = 1024
NEG_BIG = -1e30
SAFE_SUM_LO = 2.0 ** -60
SAFE_SUM_HI = 2.0 ** 100


def _dot(a, b):
    return jnp.dot(a, b, preferred_element_type=F32)


def _dot_nt(a, b):
    return lax.dot_general(a, b, (((1,), (1,)), ((), ())), preferred_element_type=F32)


def _sigmoid(x):
    return 1.0 / (1.0 + jnp.exp(-x))


def _params(sem, flags=None):
    return pltpu.CompilerParams(dimension_semantics=sem, vmem_limit_bytes=VMEM_LIMIT, flags=flags)


def _mod_kernel(c_ref, w_ref, b_ref, o_ref):
    cv = c_ref[...]
    a = cv * _sigmoid(cv)
    w = w_ref[0]
    a_hi = a.astype(BF16)
    a_lo = (a - a_hi.astype(F32)).astype(BF16)
    w_hi = w.astype(BF16)
    w_lo = (w - w_hi.astype(F32)).astype(BF16)
    acc = _dot(a_hi, w_hi) + _dot(a_hi, w_lo) + _dot(a_lo, w_hi)
    o_ref[0] = acc + b_ref[0]


def _mod_params(cvec, ada_w, ada_b):
    depth, d, d3 = ada_w.shape
    tn = 1024
    return pl.pallas_call(
        _mod_kernel,
        grid=(depth, d3 // tn),
        in_specs=[
            pl.BlockSpec((8, d), lambda l, j: (0, 0)),
            pl.BlockSpec((1, d, tn), lambda l, j: (l, 0, j)),
            pl.BlockSpec((1, 1, tn), lambda l, j: (l, 0, j)),
        ],
        out_specs=pl.BlockSpec((1, 8, tn), lambda l, j: (l, 0, j)),
        out_shape=jax.ShapeDtypeStruct((depth, 8, d3), F32),
        compiler_params=_params(("parallel", "parallel")),
        name="mod_params",
    )(cvec, ada_w, ada_b.reshape(depth, 1, d3))


def _modulated(x, g, shift, scale):
    ms = jnp.mean(x * x, axis=-1, keepdims=True)
    y = x * lax.rsqrt(ms + EPS) * g
    return y * (1.0 + scale) + shift


def _group_norm(t, gain, bd):
    sq = (t * t).astype(BF16)
    half = MXU_DIM
    ss = jnp.concatenate([_dot(sq[:, :half], bd), _dot(sq[:, half:], bd)], axis=-1)
    return t * lax.rsqrt(ss * (1.0 / DH) + EPS) * gain


def _rope(t, cos, sin):
    width = t.shape[-1]
    quarter = DH // 4
    lane = lax.broadcasted_iota(jnp.int32, t.shape, 1)
    from_above = pltpu.roll(t, width - quarter, axis=1)
    from_below = pltpu.roll(t, quarter, axis=1)
    partner = jnp.where((lane & (2 * quarter - 1)) < quarter, from_above, from_below)
    reps = width // cos.shape[-1]
    cosf = jnp.concatenate([cos] * reps, axis=-1)
    sinf = jnp.concatenate([sin] * reps, axis=-1)
    return t * cosf + partner * sinf


def _inproj_kernel(x_ref, shift_ref, scale_ref, g_ref, cos_ref, sin_ref, qg_ref, kg_ref, bd_ref,
                   wf_ref, wq_ref, wk_ref, wv_ref, wg_ref,
                   u_ref, q_ref, k_ref, vt_ref, gs_ref):
    h = _modulated(x_ref[...], g_ref[...], shift_ref[0], scale_ref[0]).astype(BF16)
    cos = cos_ref[...]
    sin = sin_ref[...]
    bd = bd_ref[...]
    u_ref[...] = _dot(h, wf_ref[...]).astype(BF16)
    q = _rope(_group_norm(_dot(h, wq_ref[...]), qg_ref[...], bd), cos, sin)
    q_ref[...] = q.astype(BF16)
    k = _rope(_group_norm(_dot(h, wk_ref[...]), kg_ref[...], bd), cos, sin)
    k_ref[...] = k.astype(BF16)
    v = _dot(h, wv_ref[...])
    for hh in range(H_DIFF):
        vt_ref[0, hh] = v[:, hh * DV:(hh + 1) * DV].T.astype(BF16)
    g = _dot(h, wg_ref[...])
    gs_ref[...] = (g * _sigmoid(g)).astype(BF16)


def _ctx_kv_kernel(x_ref, shift_ref, scale_ref, g_ref, kg_ref, bd_ref, wk_ref, wv_ref, k_ref, vt_ref):
    h = _modulated(x_ref[...], g_ref[...], shift_ref[...], scale_ref[...]).astype(BF16)
    k = _group_norm(_dot(h, wk_ref[...]), kg_ref[...], bd_ref[...])
    k_ref[...] = k.astype(BF16)
    v = _dot(h, wv_ref[...])
    for hh in range(H_DIFF):
        vt_ref[0, hh] = v[:, hh * DV:(hh + 1) * DV].T.astype(BF16)


def _const_spec(shape):
    zeros = (0,) * len(shape)
    return pl.BlockSpec(shape, lambda *_: zeros)


def _inproj0(x2, shift, scale, g, cos, sin, qg, kg, bd, wf, wq, wk, wv, wg, batch, n):
    rows, d = x2.shape
    tm = ROW_BLOCK
    bpb = n // tm
    row_spec = lambda w: pl.BlockSpec((tm, w), lambda i: (i, 0))
    return pl.pallas_call(
        _inproj_kernel,
        grid=(rows // tm,),
        in_specs=[
            row_spec(d),
            pl.BlockSpec((1, 1, d), lambda i: (i // bpb, 0, 0)),
            pl.BlockSpec((1, 1, d), lambda i: (i // bpb, 0, 0)),
            _const_spec((1, d)),
            pl.BlockSpec((tm, LANES), lambda i: (i % bpb, 0)),
            pl.BlockSpec((tm, LANES), lambda i: (i % bpb, 0)),
            _const_spec((1, QK_WIDTH)),
            _const_spec((1, QK_WIDTH)),
            _const_spec((MXU_DIM, MXU_DIM)),
            _const_spec(wf.shape), _const_spec(wq.shape), _const_spec(wk.shape),
            _const_spec(wv.shape), _const_spec(wg.shape),
        ],
        out_specs=[
            row_spec(F_WIDTH), row_spec(QK_WIDTH), row_spec(QK_WIDTH),
            pl.BlockSpec((1, H_DIFF, DV, tm), lambda i: (i // bpb, 0, 0, i % bpb)),
            row_spec(d),
        ],
        out_shape=[
            jax.ShapeDtypeStruct((rows, F_WIDTH), BF16),
            jax.ShapeDtypeStruct((rows, QK_WIDTH), BF16),
            jax.ShapeDtypeStruct((rows, QK_WIDTH), BF16),
            jax.ShapeDtypeStruct((batch, H_DIFF, DV, n), BF16),
            jax.ShapeDtypeStruct((rows, d), BF16),
        ],
        compiler_params=_params(("parallel",)),
        name="inproj0",
    )(x2, shift, scale, g, cos, sin, qg, kg, bd, wf, wq, wk, wv, wg)


def _ctx_kv(ctx2, shift, scale, g, kg, bd, wk, wv, batch, m_ctx):
    rows, d = ctx2.shape
    return pl.pallas_call(
        _ctx_kv_kernel,
        grid=(batch,),
        in_specs=[
            pl.BlockSpec((m_ctx, d), lambda b: (b, 0)),
            _const_spec((1, d)), _const_spec((1, d)), _const_spec((1, d)),
            _const_spec((1, QK_WIDTH)), _const_spec((MXU_DIM, MXU_DIM)),
            _const_spec(wk.shape), _const_spec(wv.shape),
        ],
        out_specs=[
            pl.BlockSpec((m_ctx, QK_WIDTH), lambda b: (b, 0)),
            pl.BlockSpec((1, H_DIFF, DV, m_ctx), lambda b: (b, 0, 0, 0)),
        ],
        out_shape=[
            jax.ShapeDtypeStruct((rows, QK_WIDTH), BF16),
            jax.ShapeDtypeStruct((batch, H_DIFF, DV, m_ctx), BF16),
        ],
        compiler_params=_params(("parallel",)),
        name="ctx_kv",
    )(ctx2, shift, scale, g, kg, bd, wk, wv)


def _attn_kernel(lam_ref, q_ref, k_ref, kc_ref, vt_ref, vtc_ref, sg_ref, o_ref,
                 acc1_ref, acc2_ref, p1a_ref, p1b_ref, p2a_ref, p2b_ref, *, tk, lam_init):
    tq = q_ref.shape[1]
    n_lat = k_ref.shape[1] // tk
    m_ctx = kc_ref.shape[1]
    q = q_ref[0]
    lane = lax.broadcasted_iota(jnp.int32, q.shape, 1)
    zero = jnp.zeros_like(q)
    qz = (jnp.where(lane < DH, q, zero), jnp.where(lane >= DH, q, zero))
    acc_refs = (acc1_ref, acc2_ref)
    p_refs = ((p1a_ref, p1b_ref), (p2a_ref, p2b_ref))
    chunks = [(k_ref, vt_ref, j * tk, tk) for j in range(n_lat)] + [(kc_ref, vtc_ref, 0, m_ctx)]

    def finish(l1, l2):
        lp = lam_ref[...]
        lam = (jnp.exp(jnp.sum(lp[0:1] * lp[1:2], axis=-1, keepdims=True))
               - jnp.exp(jnp.sum(lp[2:3] * lp[3:4], axis=-1, keepdims=True)) + lam_init)
        ot = acc1_ref[...] / l1 - lam * (acc2_ref[...] / l2)
        o = ot.T
        ms = jnp.mean(o * o, axis=-1, keepdims=True)
        o = o * lax.rsqrt(ms + EPS) * sg_ref[...] * (1.0 - lam_init)
        o_ref[0] = o.astype(BF16)

    def scores(kr, k0, size, mp):
        return _dot_nt(kr[0, pl.ds(k0, size), :], qz[mp])

    def exp_scores(c):
        kr, _, k0, size = chunks[c]
        sums = []
        for mp in range(2):
            p = jnp.exp2(scores(kr, k0, size, mp))
            p_refs[mp][c % 2][0:size, :] = p.astype(BF16)
            sums.append(jnp.sum(p, axis=0, keepdims=True))
        return sums

    def pv(c):
        _, vr, k0, size = chunks[c]
        vc = vr[0, 0, :, k0:k0 + size]
        for mp in range(2):
            upd = _dot(vc, p_refs[mp][c % 2][0:size, :])
            acc_refs[mp][...] = upd if c == 0 else acc_refs[mp][...] + upd

    l = exp_scores(0)
    for c in range(1, len(chunks)):
        sums = exp_scores(c)
        pv(c - 1)
        l = [l[mp] + sums[mp] for mp in range(2)]
    pv(len(chunks) - 1)
    finish(l[0], l[1])

    in_range = jnp.logical_and(jnp.minimum(l[0], l[1]) > SAFE_SUM_LO, jnp.maximum(l[0], l[1]) < SAFE_SUM_HI)
    all_safe = jnp.min(jnp.where(in_range, 1.0, 0.0)) > 0.5

    @pl.when(jnp.logical_not(all_safe))
    def _():
        def online_step(kr, vc, k0, size, carry):
            out = []
            for mp in range(2):
                m_old, l_old = carry[mp]
                s = scores(kr, k0, size, mp)
                m_new = jnp.maximum(m_old, jnp.max(s, axis=0, keepdims=True))
                alpha = jnp.exp2(m_old - m_new)
                p = jnp.exp2(s - m_new)
                acc_refs[mp][...] = alpha * acc_refs[mp][...] + _dot(vc, p.astype(BF16))
                out.append((m_new, alpha * l_old + jnp.sum(p, axis=0, keepdims=True)))
            return tuple(out)

        def body(c, carry):
            k0 = pl.multiple_of(c * tk, tk)
            return online_step(k_ref, vt_ref[0, 0, :, pl.ds(k0, tk)], k0, tk, carry)

        acc1_ref[...] = jnp.zeros_like(acc1_ref)
        acc2_ref[...] = jnp.zeros_like(acc2_ref)
        start = (jnp.full((1, tq), NEG_BIG, F32), jnp.zeros((1, tq), F32))
        carry = lax.fori_loop(0, n_lat, body, (start, start))
        carry = online_step(kc_ref, vtc_ref[0, 0], 0, m_ctx, carry)
        finish(carry[0][1], carry[1][1])


def _pick_key_chunk(n_keys):
    best = LANES
    for cand in range(LANES, KEY_CHUNK_MAX + 1, LANES):
        if n_keys % cand == 0:
            best = cand
    return best


def _diff_attention(lam_p, q, k, kc, vt, vtc, sg, lam_init):
    batch, n, _ = q.shape
    m_ctx = kc.shape[1]
    tq = Q_BLOCK_ATT
    tk = _pick_key_chunk(n)
    hw = 2 * DH
    kern = functools.partial(_attn_kernel, tk=tk, lam_init=lam_init)
    return pl.pallas_call(
        kern,
        grid=(batch, H_DIFF, n // tq),
        in_specs=[
            _const_spec((4, DH)),
            pl.BlockSpec((1, tq, hw), lambda b, h, i: (b, i, h)),
            pl.BlockSpec((1, n, hw), lambda b, h, i: (b, 0, h)),
            pl.BlockSpec((1, m_ctx, hw), lambda b, h, i: (b, 0, h)),
            pl.BlockSpec((1, 1, DV, n), lambda b, h, i: (b, h, 0, 0)),
            pl.BlockSpec((1, 1, DV, m_ctx), lambda b, h, i: (b, h, 0, 0)),
            _const_spec((1, DV)),
        ],
        out_specs=pl.BlockSpec((1, tq, DV), lambda b, h, i: (b, i, h)),
        out_shape=jax.ShapeDtypeStruct((batch, n, ATT_WIDTH), BF16),
        scratch_shapes=[pltpu.VMEM((DV, tq), F32)] * 2 + [pltpu.VMEM((max(tk, m_ctx), tq), BF16)] * 4,
        compiler_params=_params(("parallel", "parallel", "arbitrary")),
        name="diff_attn",
    )(lam_p, q, k, kc, vt, vtc, sg)


def _fft_rows_kernel(f_ref, u_ref, y_ref):
    y = _dot(f_ref[...], u_ref[0]).astype(BF16)
    width = y_ref.shape[-1]
    for j in range(y_ref.shape[1]):
        y_ref[0, j] = y[:, j * width:(j + 1) * width]


def _fft_rows(f1cs, u_view, width):
    batch, n1, cols = u_view.shape
    n2 = cols // width
    per_step = min(n2, 8)
    return pl.pallas_call(
        _fft_rows_kernel,
        grid=(batch, n2 // per_step),
        in_specs=[
            _const_spec(f1cs.shape),
            pl.BlockSpec((1, n1, per_step * width), lambda b, j: (b, 0, j)),
        ],
        out_specs=pl.BlockSpec((1, per_step, 2 * n1, width), lambda b, j: (b, j, 0, 0)),
        out_shape=jax.ShapeDtypeStruct((batch, n2, 2 * n1, width), BF16),
        compiler_params=_params(("parallel", "parallel")),
        name="fft_rows",
    )(f1cs, u_view)


def _fft_cols_kernel(g_ref, y_ref, bdc_ref, bds_ref, o_ref):
    _, n2, kb2, width = y_ref.shape
    kb = kb2 // 2
    rows = kb * n2
    y = y_ref[0].reshape(2 * rows, width)
    z = _dot(g_ref[0], y)
    xr = z[:rows].astype(BF16)
    xi = z[rows:].astype(BF16)
    bdc = bdc_ref[...]
    bds = bds_ref[...]
    half = MXU_DIM
    o = jnp.concatenate(
        [_dot(xr[:, :half], bdc) + _dot(xi[:, :half], bds),
         _dot(xr[:, half:], bdc) + _dot(xi[:, half:], bds)], axis=-1)
    o_ref[0] = o.reshape(n2, kb, width)


def _fft_cols(gbig, y4, bdc, bds):
    batch, n2, rows2, width = y4.shape
    n1 = rows2 // 2
    kb = FFT_KB
    gr = gbig.shape[1]
    return pl.pallas_call(
        _fft_cols_kernel,
        grid=(n1 // kb, batch),
        in_specs=[
            pl.BlockSpec((1, gr, gr), lambda j, b: (j, 0, 0)),
            pl.BlockSpec((1, n2, 2 * kb, width), lambda j, b: (b, 0, j, 0)),
            _const_spec(bdc.shape), _const_spec(bds.shape),
        ],
        out_specs=pl.BlockSpec((1, n2, kb, width), lambda j, b: (b, 0, j, 0)),
        out_shape=jax.ShapeDtypeStruct((batch, n2, n1, width), F32),
        compiler_params=_params(("parallel", "arbitrary")),
        name="fft_cols",
    )(gbig, y4, bdc, bds)


@functools.lru_cache(maxsize=None)
def _fft_tables(n):
    n1 = FFT_N1
    n2 = n // n1
    kb = FFT_KB
    k1 = np.arange(n1)
    ang = 2.0 * np.pi * np.outer(k1, k1) / n1
    f1cs = np.stack([np.cos(ang), -np.sin(ang)], axis=1).reshape(2 * n1, n1)
    scale = 1.0 / math.sqrt(n * F_CH)
    nblk = n1 // kb
    rows = n2 * kb
    gbig = np.zeros((nblk, 2, n2, kb, n2, kb, 2), np.float32)
    n2i = np.arange(n2)
    for j in range(kb):
        kk = (np.arange(nblk)[:, None] * kb + j) + n1 * n2i[None, :]
        a = 2.0 * np.pi * (kk[:, :, None] * n2i[None, None, :] % n) / n
        gr = np.cos(a) * scale
        gi = -np.sin(a) * scale
        gbig[:, 0, :, j, :, j, 0] = gr
        gbig[:, 0, :, j, :, j, 1] = -gi
        gbig[:, 1, :, j, :, j, 0] = gi
        gbig[:, 1, :, j, :, j, 1] = gr
    gbig = gbig.reshape(nblk, 2 * rows, 2 * rows)
    c = np.arange(F_CH)
    angc = 2.0 * np.pi * np.outer(c, c) / F_CH
    eye2 = np.eye(MXU_DIM // F_CH)
    bdc = np.kron(eye2, np.cos(angc))
    bds = np.kron(eye2, np.sin(angc))
    to_bf16 = lambda a: jnp.asarray(a, dtype=F32).astype(BF16)
    return to_bf16(f1cs), to_bf16(gbig), to_bf16(bdc), to_bf16(bds)


def _fourier_mix(u, batch, n):
    f1cs, gbig, bdc, bds = _fft_tables(n)
    n1 = FFT_N1
    n2 = n // n1
    y = _fft_rows(f1cs, u.reshape(batch, n1, n2 * F_WIDTH), F_WIDTH)
    fo = _fft_cols(gbig, y, bdc, bds)
    return fo.reshape(batch * n, F_WIDTH)


def _tail_kernel(xp_ref, x_ref, xn_ref, fp_ref, f_ref, fn_ref, ap_ref, a_ref, an_ref, gp_ref, g_ref, gn_ref,
                 gate0_ref, wo0_ref, g1_ref, shift_ref, scale_ref, wi_ref, cw_ref, gate1_ref, wo1_ref,
                 o_ref, *, bpb):
    i = pl.program_id(0)
    tm = x_ref.shape[0]
    halo = xp_ref.shape[0]
    ext = tm + 2 * halo
    cat = lambda p, m, nx: jnp.concatenate([p[...], m[...], nx[...]], axis=0)
    x = cat(xp_ref, x_ref, xn_ref)
    gs = cat(gp_ref, g_ref, gn_ref).astype(F32)
    yf = (cat(fp_ref, f_ref, fn_ref) * gs[:, :F_WIDTH]).astype(BF16)
    ya = (cat(ap_ref, a_ref, an_ref).astype(F32) * gs[:, F_WIDTH:]).astype(BF16)
    y = _dot(yf, wo0_ref[:F_WIDTH, :]) + _dot(ya, wo0_ref[F_WIDTH:, :])
    x1 = x + gate0_ref[0] * y
    h = _modulated(x1, g1_ref[...], shift_ref[0], scale_ref[0]).astype(BF16)
    w = x.shape[-1]
    u = _dot(h, wi_ref[:, 1 * w:2 * w]) * _dot(h, wi_ref[:, 2 * w:3 * w])
    row = lax.broadcasted_iota(jnp.int32, u.shape, 0)
    lo = jnp.where((i % bpb) == 0, halo, 0)
    hi = jnp.where((i % bpb) == bpb - 1, halo + tm, ext)
    u = jnp.where(jnp.logical_and(row >= lo, row < hi), u, 0.0)
    cw = cw_ref[...]
    conv = (cw[0:1] * pltpu.roll(u, 1, axis=0)[halo:halo + tm]
            + cw[1:2] * u[halo:halo + tm]
            + cw[2:3] * pltpu.roll(u, ext - 1, axis=0)[halo:halo + tm])
    hm = h[halo:halo + tm]
    bg = _dot(hm, wi_ref[:, 0 * w:1 * w])
    gg = _dot(hm, wi_ref[:, 3 * w:4 * w])
    z = (bg * conv * (gg * _sigmoid(gg))).astype(BF16)
    o_ref[...] = x1[halo:halo + tm] + gate1_ref[0] * _dot(z, wo1_ref[...])


def _tail(x2, fo, ao, gs, gate0, wo0, g1, shift1, scale1, wi, cw, gate1, wo1, n):
    rows, d = x2.shape
    tm = ROW_BLOCK
    bpb = n // tm
    hb = tm // HALO
    nhalo = rows // HALO
    main = lambda w: pl.BlockSpec((tm, w), lambda i: (i, 0))
    prev = lambda w: pl.BlockSpec((HALO, w), lambda i: (jnp.maximum(i * hb - 1, 0), 0))
    nxt = lambda w: pl.BlockSpec((HALO, w), lambda i: (jnp.minimum((i + 1) * hb, nhalo - 1), 0))
    with_halo = lambda w: [prev(w), main(w), nxt(w)]
    per_batch = pl.BlockSpec((1, 1, d), lambda i: (i // bpb, 0, 0))
    kern = functools.partial(_tail_kernel, bpb=bpb)
    return pl.pallas_call(
        kern,
        grid=(rows // tm,),
        in_specs=(with_halo(d) + with_halo(F_WIDTH) + with_halo(ATT_WIDTH) + with_halo(d)
                  + [per_batch, _const_spec(wo0.shape), _const_spec((1, d)), per_batch, per_batch,
                     _const_spec(wi.shape), _const_spec(cw.shape), per_batch, _const_spec(wo1.shape)]),
        out_specs=main(d),
        out_shape=jax.ShapeDtypeStruct((rows, d), F32),
        compiler_params=_params(("parallel",)),
        name="tail",
    )(x2, x2, x2, fo, fo, fo, ao, ao, ao, gs, gs, gs, gate0, wo0, g1, shift1, scale1, wi, cw, gate1, wo1)


@functools.lru_cache(maxsize=None)
def _rope_tables(n):
    rows = n // GRID_W
    pos_r = np.repeat(np.arange(rows, dtype=np.float64), GRID_W)
    pos_c = np.tile(np.arange(GRID_W, dtype=np.float64), rows)
    half = DH // 2
    freqs = ROPE_BASE ** (-np.arange(0, half, 2, dtype=np.float64) / half)
    ang_r = pos_r[:, None] * freqs
    ang_c = pos_c[:, None] * freqs
    cos64 = np.concatenate([np.cos(ang_r), np.cos(ang_r), np.cos(ang_c), np.cos(ang_c)], axis=-1)
    sin64 = np.concatenate([-np.sin(ang_r), np.sin(ang_r), -np.sin(ang_c), np.sin(ang_c)], axis=-1)
    cos = np.concatenate([cos64, cos64], axis=-1).astype(np.float32)
    sin = np.concatenate([sin64, sin64], axis=-1).astype(np.float32)
    return jnp.asarray(cos), jnp.asarray(sin)


@functools.lru_cache(maxsize=None)
def _group_sum_matrix():
    return jnp.asarray(np.kron(np.eye(MXU_DIM // DH), np.ones((DH, DH))), dtype=BF16)


def _lambda_init(layer_idx):
    return 0.8 - 0.6 * math.exp(-0.3 * layer_idx)


def kernel(x, c, ctx, c_ctx, norm_g, ada_w, ada_b, even_w_in, even_q_norm, even_k_norm, even_lambda_q1,
           even_lambda_k1, even_lambda_q2, even_lambda_k2, even_subln, even_w_out, odd_w_in, odd_conv_w,
           odd_w_out):
    batch, n, d = x.shape
    m_ctx = ctx.shape[1]
    assert d == D_MODEL and n % (FFT_N1 * FFT_KB) == 0 and n % ROW_BLOCK == 0 and n % GRID_W == 0
    assert batch < 8 and norm_g.shape[0] == 2

    cvec = jnp.zeros((8, d), F32).at[:batch].set(c).at[batch].set(c_ctx)
    mods = _mod_params(cvec, ada_w, ada_b)
    shift0, scale0, gate0 = [mods[0, :, j * d:(j + 1) * d].reshape(8, 1, d) for j in range(3)]
    shift1, scale1, gate1 = [mods[1, :, j * d:(j + 1) * d].reshape(8, 1, d) for j in range(3)]

    w_in = even_w_in[0].astype(BF16)
    wf = w_in[:, 0:F_WIDTH]
    wq = w_in[:, F_WIDTH:F_WIDTH + QK_WIDTH]
    wk = w_in[:, F_WIDTH + QK_WIDTH:F_WIDTH + 2 * QK_WIDTH]
    wv = w_in[:, F_WIDTH + 2 * QK_WIDTH:F_WIDTH + 2 * QK_WIDTH + ATT_WIDTH]
    wg = w_in[:, F_WIDTH + 2 * QK_WIDTH + ATT_WIDTH:]
    reps = QK_WIDTH // DH
    qg = (jnp.tile(even_q_norm[0].astype(F32), reps) * (ATTN_SCALE * LOG2E)).reshape(1, QK_WIDTH)
    kg = jnp.tile(even_k_norm[0].astype(F32), reps).reshape(1, QK_WIDTH)
    bd = _group_sum_matrix()
    cos, sin = _rope_tables(n)
    g0 = norm_g[0].reshape(1, d)
    g1 = norm_g[1].reshape(1, d)

    x2 = x.reshape(batch * n, d)
    u, q, k, vt, gs = _inproj0(x2, shift0, scale0, g0, cos, sin, qg, kg, bd, wf, wq, wk, wv, wg, batch, n)
    kc, vtc = _ctx_kv(ctx.reshape(batch * m_ctx, d), shift0[batch], scale0[batch], g0, kg, bd, wk, wv,
                      batch, m_ctx)

    lam_p = jnp.stack([even_lambda_q1[0], even_lambda_k1[0], even_lambda_q2[0], even_lambda_k2[0]]).astype(F32)
    sg = even_subln[0].astype(F32).reshape(1, DV)
    ao = _diff_attention(lam_p, q.reshape(batch, n, QK_WIDTH), k.reshape(batch, n, QK_WIDTH),
                         kc.reshape(batch, m_ctx, QK_WIDTH), vt, vtc, sg, _lambda_init(0))

    fo = _fourier_mix(u, batch, n)

    out = _tail(x2, fo, ao.reshape(batch * n, ATT_WIDTH), gs, gate0, even_w_out[0].astype(BF16), g1, shift1,
                scale1, odd_w_in[0].astype(BF16), odd_conv_w[0].astype(F32), gate1, odd_w_out[0].astype(BF16), n)
    return out.reshape(batch, n, d)
```

```python
import functools
import math

import numpy as np
import jax
import jax.numpy as jnp
from jax import lax
from jax.experimental import pallas as pl
from jax.experimental.pallas import tpu as pltpu

F32 = jnp.float32
BF16 = jnp.bfloat16

D_MODEL = 1024
GRID_W = 64
F_GROUPS = 4
F_CH = 128
F_WIDTH = F_GROUPS * F_CH
H_DIFF = 4
DH = 64
DV = 128
QK_WIDTH = H_DIFF * 2 * DH
ATT_WIDTH = H_DIFF * DV
CONV_K = 3
ROPE_BASE = 10000.0
EPS = 1e-6
ATTN_SCALE = DH ** -0.5
LOG2E = 1.4426950408889634

LANES = 128
MXU_DIM = 256
VMEM_LIMIT = 56 * 1024 * 1024

ROW_BLOCK = 1024
HALO = 16
FFT_N1 = 128
FFT_KB = 8
Q_BLOCK_ATT = 512
KEY_CHUNK_MAX = 1024
NEG_BIG = -1e30
SAFE_SUM_LO = 2.0 ** -60
SAFE_SUM_HI = 2.0 ** 100


def _dot(a, b):
    return jnp.dot(a, b, preferred_element_type=F32)


def _dot_nt(a, b):
    return lax.dot_general(a, b, (((1,), (1,)), ((), ())), preferred_element_type=F32)


def _sigmoid(x):
    return 1.0 / (1.0 + jnp.exp(-x))


def _params(sem, flags=None):
    return pltpu.CompilerParams(dimension_semantics=sem, vmem_limit_bytes=VMEM_LIMIT, flags=flags)


def _mod_kernel(c_ref, w_ref, b_ref, o_ref):
    cv = c_ref[...]
    a = cv * _sigmoid(cv)
    w = w_ref[0]
    a_hi = a.astype(BF16)
    a_lo = (a - a_hi.astype(F32)).astype(BF16)
    w_hi = w.astype(BF16)
    w_lo = (w - w_hi.astype(F32)).astype(BF16)
    acc = _dot(a_hi, w_hi) + _dot(a_hi, w_lo) + _dot(a_lo, w_hi)
    o_ref[0] = acc + b_ref[0]


def _mod_params(cvec, ada_w, ada_b):
    depth, d, d3 = ada_w.shape
    tn = 1024
    return pl.pallas_call(
        _mod_kernel,
        grid=(depth, d3 // tn),
        in_specs=[
            pl.BlockSpec((8, d), lambda l, j: (0, 0)),
            pl.BlockSpec((1, d, tn), lambda l, j: (l, 0, j)),
            pl.BlockSpec((1, 1, tn), lambda l, j: (l, 0, j)),
        ],
        out_specs=pl.BlockSpec((1, 8, tn), lambda l, j: (l, 0, j)),
        out_shape=jax.ShapeDtypeStruct((depth, 8, d3), F32),
        compiler_params=_params(("parallel", "parallel")),
        name="mod_params",
    )(cvec, ada_w, ada_b.reshape(depth, 1, d3))


def _modulated(x, g, shift, scale):
    ms = jnp.mean(x * x, axis=-1, keepdims=True)
    y = x * lax.rsqrt(ms + EPS) * g
    return y * (1.0 + scale) + shift


def _group_norm(t, gain, bd):
    sq = (t * t).astype(BF16)
    half = MXU_DIM
    ss = jnp.concatenate([_dot(sq[:, :half], bd), _dot(sq[:, half:], bd)], axis=-1)
    return t * lax.rsqrt(ss * (1.0 / DH) + EPS) * gain


def _rope(t, cos, sin):
    width = t.shape[-1]
    quarter = DH // 4
    lane = lax.broadcasted_iota(jnp.int32, t.shape, 1)
    from_above = pltpu.roll(t, width - quarter, axis=1)
    from_below = pltpu.roll(t, quarter, axis=1)
    partner = jnp.where((lane & (2 * quarter - 1)) < quarter, from_above, from_below)
    reps = width // cos.shape[-1]
    cosf = jnp.concatenate([cos] * reps, axis=-1)
    sinf = jnp.concatenate([sin] * reps, axis=-1)
    return t * cosf + partner * sinf


def _inproj_kernel(x_ref, shift_ref, scale_ref, g_ref, cos_ref, sin_ref, qg_ref, kg_ref, bd_ref,
                   wf_ref, wq_ref, wk_ref, wv_ref, wg_ref,
                   u_ref, q_ref, k_ref, vt_ref, gs_ref, us_ref):
    h = _modulated(x_ref[...], g_ref[...], shift_ref[0], scale_ref[0]).astype(BF16)
    cos = cos_ref[...]
    sin = sin_ref[...]
    bd = bd_ref[...]
    _, n2, n1_blk, _ = u_ref.shape
    pitch = us_ref.shape[1] // n1_blk
    uf = _dot(h, wf_ref[...])
    for a in range(n1_blk):
        for s in range(F_WIDTH // LANES):
            us_ref[s, a * pitch:a * pitch + n2, :] = uf[a * n2:(a + 1) * n2, s * LANES:(s + 1) * LANES]
    for b in range(n2):
        for s in range(F_WIDTH // LANES):
            col = us_ref[s, pl.ds(b, n1_blk, stride=pitch), :]
            u_ref[0, b, :, s * LANES:(s + 1) * LANES] = col.astype(BF16)
    q = _rope(_group_norm(_dot(h, wq_ref[...]), qg_ref[...], bd), cos, sin)
    q_ref[...] = q.astype(BF16)
    k = _rope(_group_norm(_dot(h, wk_ref[...]), kg_ref[...], bd), cos, sin)
    k_ref[...] = k.astype(BF16)
    v = _dot(h, wv_ref[...])
    for hh in range(H_DIFF):
        vt_ref[0, hh] = v[:, hh * DV:(hh + 1) * DV].T.astype(BF16)
    g = _dot(h, wg_ref[...])
    gs_ref[...] = (g * _sigmoid(g)).astype(BF16)


def _ctx_kv_kernel(x_ref, shift_ref, scale_ref, g_ref, kg_ref, bd_ref, wk_ref, wv_ref, k_ref, vt_ref):
    h = _modulated(x_ref[...], g_ref[...], shift_ref[...], scale_ref[...]).astype(BF16)
    k = _group_norm(_dot(h, wk_ref[...]), kg_ref[...], bd_ref[...])
    k_ref[...] = k.astype(BF16)
    v = _dot(h, wv_ref[...])
    for hh in range(H_DIFF):
        vt_ref[0, hh] = v[:, hh * DV:(hh + 1) * DV].T.astype(BF16)


def _const_spec(shape):
    zeros = (0,) * len(shape)
    return pl.BlockSpec(shape, lambda *_: zeros)


def _inproj0(x2, shift, scale, g, cos, sin, qg, kg, bd, wf, wq, wk, wv, wg, batch, n):
    rows, d = x2.shape
    tm = ROW_BLOCK
    bpb = n // tm
    n2 = n // FFT_N1
    n1_blk = tm // n2
    assert tm % n2 == 0 and n2 % 8 == 0 and n1_blk % 16 == 0
    row_spec = lambda w: pl.BlockSpec((tm, w), lambda i: (i, 0))
    return pl.pallas_call(
        _inproj_kernel,
        grid=(rows // tm,),
        in_specs=[
            row_spec(d),
            pl.BlockSpec((1, 1, d), lambda i: (i // bpb, 0, 0)),
            pl.BlockSpec((1, 1, d), lambda i: (i // bpb, 0, 0)),
            _const_spec((1, d)),
            pl.BlockSpec((tm, LANES), lambda i: (i % bpb, 0)),
            pl.BlockSpec((tm, LANES), lambda i: (i % bpb, 0)),
            _const_spec((1, QK_WIDTH)),
            _const_spec((1, QK_WIDTH)),
            _const_spec((MXU_DIM, MXU_DIM)),
            _const_spec(wf.shape), _const_spec(wq.shape), _const_spec(wk.shape),
            _const_spec(wv.shape), _const_spec(wg.shape),
        ],
        out_specs=[
            pl.BlockSpec((1, n2, n1_blk, F_WIDTH), lambda i: (i // bpb, 0, i % bpb, 0)),
            row_spec(QK_WIDTH), row_spec(QK_WIDTH),
            pl.BlockSpec((1, H_DIFF, DV, tm), lambda i: (i // bpb, 0, 0, i % bpb)),
            row_spec(d),
        ],
        out_shape=[
            jax.ShapeDtypeStruct((batch, n2, FFT_N1, F_WIDTH), BF16),
            jax.ShapeDtypeStruct((rows, QK_WIDTH), BF16),
            jax.ShapeDtypeStruct((rows, QK_WIDTH), BF16),
            jax.ShapeDtypeStruct((batch, H_DIFF, DV, n), BF16),
            jax.ShapeDtypeStruct((rows, d), BF16),
        ],
        scratch_shapes=[pltpu.VMEM((F_WIDTH // LANES, n1_blk * (n2 + 4), LANES), F32)],
        compiler_params=_params(("parallel",)),
        name="inproj0",
    )(x2, shift, scale, g, cos, sin, qg, kg, bd, wf, wq, wk, wv, wg)


def _ctx_kv(ctx2, shift, scale, g, kg, bd, wk, wv, batch, m_ctx):
    rows, d = ctx2.shape
    return pl.pallas_call(
        _ctx_kv_kernel,
        grid=(batch,),
        in_specs=[
            pl.BlockSpec((m_ctx, d), lambda b: (b, 0)),
            _const_spec((1, d)), _const_spec((1, d)), _const_spec((1, d)),
            _const_spec((1, QK_WIDTH)), _const_spec((MXU_DIM, MXU_DIM)),
            _const_spec(wk.shape), _const_spec(wv.shape),
        ],
        out_specs=[
            pl.BlockSpec((m_ctx, QK_WIDTH), lambda b: (b, 0)),
            pl.BlockSpec((1, H_DIFF, DV, m_ctx), lambda b: (b, 0, 0, 0)),
        ],
        out_shape=[
            jax.ShapeDtypeStruct((rows, QK_WIDTH), BF16),
            jax.ShapeDtypeStruct((batch, H_DIFF, DV, m_ctx), BF16),
        ],
        compiler_params=_params(("parallel",)),
        name="ctx_kv",
    )(ctx2, shift, scale, g, kg, bd, wk, wv)


def _attn_kernel(lam_ref, q_ref, k_ref, kc_ref, vt_ref, vtc_ref, sg_ref, o_ref,
                 acc1_ref, acc2_ref, p1a_ref, p1b_ref, p2a_ref, p2b_ref, *, tk, lam_init):
    tq = q_ref.shape[1]
    n_lat = k_ref.shape[1] // tk
    m_ctx = kc_ref.shape[1]
    q = q_ref[0]
    lane = lax.broadcasted_iota(jnp.int32, q.shape, 1)
    zero = jnp.zeros_like(q)
    qz = (jnp.where(lane < DH, q, zero), jnp.where(lane >= DH, q, zero))
    acc_refs = (acc1_ref, acc2_ref)
    p_refs = ((p1a_ref, p1b_ref), (p2a_ref, p2b_ref))
    chunks = [(k_ref, vt_ref, j * tk, tk) for j in range(n_lat)] + [(kc_ref, vtc_ref, 0, m_ctx)]

    def finish(l1, l2):
        lp = lam_ref[...]
        lam = (jnp.exp(jnp.sum(lp[0:1] * lp[1:2], axis=-1, keepdims=True))
               - jnp.exp(jnp.sum(lp[2:3] * lp[3:4], axis=-1, keepdims=True)) + lam_init)
        ot = acc1_ref[...] / l1 - lam * (acc2_ref[...] / l2)
        o = ot.T
        ms = jnp.mean(o * o, axis=-1, keepdims=True)
        o = o * lax.rsqrt(ms + EPS) * sg_ref[...] * (1.0 - lam_init)
        o_ref[0] = o.astype(BF16)

    def scores(kr, k0, size, mp):
        return _dot_nt(kr[0, pl.ds(k0, size), :], qz[mp])

    def exp_scores(c):
        kr, _, k0, size = chunks[c]
        sums = []
        for mp in range(2):
            p = jnp.exp2(scores(kr, k0, size, mp))
            p_refs[mp][c % 2][0:size, :] = p.astype(BF16)
            sums.append(jnp.sum(p, axis=0, keepdims=True))
        return sums

    def pv(c):
        _, vr, k0, size = chunks[c]
        vc = vr[0, 0, :, k0:k0 + size]
        for mp in range(2):
            upd = _dot(vc, p_refs[mp][c % 2][0:size, :])
            acc_refs[mp][...] = upd if c == 0 else acc_refs[mp][...] + upd

    l = exp_scores(0)
    for c in range(1, len(chunks)):
        sums = exp_scores(c)
        pv(c - 1)
        l = [l[mp] + sums[mp] for mp in range(2)]
    pv(len(chunks) - 1)
    finish(l[0], l[1])

    in_range = jnp.logical_and(jnp.minimum(l[0], l[1]) > SAFE_SUM_LO, jnp.maximum(l[0], l[1]) < SAFE_SUM_HI)
    all_safe = jnp.min(jnp.where(in_range, 1.0, 0.0)) > 0.5

    @pl.when(jnp.logical_not(all_safe))
    def _():
        def online_step(kr, vc, k0, size, carry):
            out = []
            for mp in range(2):
                m_old, l_old = carry[mp]
                s = scores(kr, k0, size, mp)
                m_new = jnp.maximum(m_old, jnp.max(s, axis=0, keepdims=True))
                alpha = jnp.exp2(m_old - m_new)
                p = jnp.exp2(s - m_new)
                acc_refs[mp][...] = alpha * acc_refs[mp][...] + _dot(vc, p.astype(BF16))
                out.append((m_new, alpha * l_old + jnp.sum(p, axis=0, keepdims=True)))
            return tuple(out)

        def body(c, carry):
            k0 = pl.multiple_of(c * tk, tk)
            return online_step(k_ref, vt_ref[0, 0, :, pl.ds(k0, tk)], k0, tk, carry)

        acc1_ref[...] = jnp.zeros_like(acc1_ref)
        acc2_ref[...] = jnp.zeros_like(acc2_ref)
        start = (jnp.full((1, tq), NEG_BIG, F32), jnp.zeros((1, tq), F32))
        carry = lax.fori_loop(0, n_lat, body, (start, start))
        carry = online_step(kc_ref, vtc_ref[0, 0], 0, m_ctx, carry)
        finish(carry[0][1], carry[1][1])


def _pick_key_chunk(n_keys):
    best = LANES
    for cand in range(LANES, KEY_CHUNK_MAX + 1, LANES):
        if n_keys % cand == 0:
            best = cand
    return best


def _diff_attention(lam_p, q, k, kc, vt, vtc, sg, lam_init):
    batch, n, _ = q.shape
    m_ctx = kc.shape[1]
    tq = Q_BLOCK_ATT
    tk = _pick_key_chunk(n)
    hw = 2 * DH
    kern = functools.partial(_attn_kernel, tk=tk, lam_init=lam_init)
    return pl.pallas_call(
        kern,
        grid=(batch, H_DIFF, n // tq),
        in_specs=[
            _const_spec((4, DH)),
            pl.BlockSpec((1, tq, hw), lambda b, h, i: (b, i, h)),
            pl.BlockSpec((1, n, hw), lambda b, h, i: (b, 0, h)),
            pl.BlockSpec((1, m_ctx, hw), lambda b, h, i: (b, 0, h)),
            pl.BlockSpec((1, 1, DV, n), lambda b, h, i: (b, h, 0, 0)),
            pl.BlockSpec((1, 1, DV, m_ctx), lambda b, h, i: (b, h, 0, 0)),
            _const_spec((1, DV)),
        ],
        out_specs=pl.BlockSpec((1, tq, DV), lambda b, h, i: (b, i, h)),
        out_shape=jax.ShapeDtypeStruct((batch, n, ATT_WIDTH), BF16),
        scratch_shapes=[pltpu.VMEM((DV, tq), F32)] * 2 + [pltpu.VMEM((max(tk, m_ctx), tq), BF16)] * 4,
        compiler_params=_params(("parallel", "parallel", "arbitrary")),
        name="diff_attn",
    )(lam_p, q, k, kc, vt, vtc, sg)


def _fft_rows_kernel(f_ref, u_ref, y_ref):
    for j in range(y_ref.shape[1]):
        y_ref[0, j] = _dot(f_ref[...], u_ref[0, j]).astype(BF16)


def _fft_rows(f1cs, u4):
    batch, n2, n1, width = u4.shape
    per_step = min(n2, 8)
    return pl.pallas_call(
        _fft_rows_kernel,
        grid=(batch, n2 // per_step),
        in_specs=[
            _const_spec(f1cs.shape),
            pl.BlockSpec((1, per_step, n1, width), lambda b, j: (b, j, 0, 0)),
        ],
        out_specs=pl.BlockSpec((1, per_step, 2 * n1, width), lambda b, j: (b, j, 0, 0)),
        out_shape=jax.ShapeDtypeStruct((batch, n2, 2 * n1, width), BF16),
        compiler_params=_params(("parallel", "parallel")),
        name="fft_rows",
    )(f1cs, u4)


def _fft_cols_kernel(g_ref, y_ref, bdc_ref, bds_ref, o_ref):
    _, n2, kb2, width = y_ref.shape
    kb = kb2 // 2
    rows = kb * n2
    y = y_ref[0].reshape(2 * rows, width)
    z = _dot(g_ref[0], y)
    xr = z[:rows].astype(BF16)
    xi = z[rows:].astype(BF16)
    bdc = bdc_ref[...]
    bds = bds_ref[...]
    half = MXU_DIM
    o = jnp.concatenate(
        [_dot(xr[:, :half], bdc) + _dot(xi[:, :half], bds),
         _dot(xr[:, half:], bdc) + _dot(xi[:, half:], bds)], axis=-1)
    o_ref[0] = o.reshape(n2, kb, width)


def _fft_cols(gbig, y4, bdc, bds):
    batch, n2, rows2, width = y4.shape
    n1 = rows2 // 2
    kb = FFT_KB
    gr = gbig.shape[1]
    return pl.pallas_call(
        _fft_cols_kernel,
        grid=(n1 // kb, batch),
        in_specs=[
            pl.BlockSpec((1, gr, gr), lambda j, b: (j, 0, 0)),
            pl.BlockSpec((1, n2, 2 * kb, width), lambda j, b: (b, 0, j, 0)),
            _const_spec(bdc.shape), _const_spec(bds.shape),
        ],
        out_specs=pl.BlockSpec((1, n2, kb, width), lambda j, b: (b, 0, j, 0)),
        out_shape=jax.ShapeDtypeStruct((batch, n2, n1, width), F32),
        compiler_params=_params(("parallel", "arbitrary")),
        name="fft_cols",
    )(gbig, y4, bdc, bds)


@functools.lru_cache(maxsize=None)
def _fft_tables(n):
    n1 = FFT_N1
    n2 = n // n1
    kb = FFT_KB
    k1 = np.arange(n1)
    ang = 2.0 * np.pi * np.outer(k1, k1) / n1
    f1cs = np.stack([np.cos(ang), -np.sin(ang)], axis=1).reshape(2 * n1, n1)
    scale = 1.0 / math.sqrt(n * F_CH)
    nblk = n1 // kb
    rows = n2 * kb
    gbig = np.zeros((nblk, 2, n2, kb, n2, kb, 2), np.float32)
    n2i = np.arange(n2)
    for j in range(kb):
        kk = (np.arange(nblk)[:, None] * kb + j) + n1 * n2i[None, :]
        a = 2.0 * np.pi * (kk[:, :, None] * n2i[None, None, :] % n) / n
        gr = np.cos(a) * scale
        gi = -np.sin(a) * scale
        gbig[:, 0, :, j, :, j, 0] = gr
        gbig[:, 0, :, j, :, j, 1] = -gi
        gbig[:, 1, :, j, :, j, 0] = gi
        gbig[:, 1, :, j, :, j, 1] = gr
    gbig = gbig.reshape(nblk, 2 * rows, 2 * rows)
    c = np.arange(F_CH)
    angc = 2.0 * np.pi * np.outer(c, c) / F_CH
    eye2 = np.eye(MXU_DIM // F_CH)
    bdc = np.kron(eye2, np.cos(angc))
    bds = np.kron(eye2, np.sin(angc))
    to_bf16 = lambda a: jnp.asarray(a, dtype=F32).astype(BF16)
    return to_bf16(f1cs), to_bf16(gbig), to_bf16(bdc), to_bf16(bds)


def _fourier_mix(u4, batch, n):
    f1cs, gbig, bdc, bds = _fft_tables(n)
    y = _fft_rows(f1cs, u4)
    fo = _fft_cols(gbig, y, bdc, bds)
    return fo.reshape(batch * n, F_WIDTH)


def _tail_kernel(xp_ref, x_ref, xn_ref, fp_ref, f_ref, fn_ref, ap_ref, a_ref, an_ref, gp_ref, g_ref, gn_ref,
                 gate0_ref, wo0_ref, g1_ref, shift_ref, scale_ref, wi_ref, cw_ref, gate1_ref, wo1_ref,
                 o_ref, *, bpb):
    i = pl.program_id(0)
    tm = x_ref.shape[0]
    halo = xp_ref.shape[0]
    ext = tm + 2 * halo
    cat = lambda p, m, nx: jnp.concatenate([p[...], m[...], nx[...]], axis=0)
    x = cat(xp_ref, x_ref, xn_ref)
    gs = cat(gp_ref, g_ref, gn_ref).astype(F32)
    yf = (cat(fp_ref, f_ref, fn_ref) * gs[:, :F_WIDTH]).astype(BF16)
    ya = (cat(ap_ref, a_ref, an_ref).astype(F32) * gs[:, F_WIDTH:]).astype(BF16)
    y = _dot(yf, wo0_ref[:F_WIDTH, :]) + _dot(ya, wo0_ref[F_WIDTH:, :])
    x1 = x + gate0_ref[0] * y
    h = _modulated(x1, g1_ref[...], shift_ref[0], scale_ref[0]).astype(BF16)
    w = x.shape[-1]
    u = _dot(h, wi_ref[:, 1 * w:2 * w]) * _dot(h, wi_ref[:, 2 * w:3 * w])
    row = lax.broadcasted_iota(jnp.int32, u.shape, 0)
    lo = jnp.where((i % bpb) == 0, halo, 0)
    hi = jnp.where((i % bpb) == bpb - 1, halo + tm, ext)
    u = jnp.where(jnp.logical_and(row >= lo, row < hi), u, 0.0)
    cw = cw_ref[...]
    conv = (cw[0:1] * pltpu.roll(u, 1, axis=0)[halo:halo + tm]
            + cw[1:2] * u[halo:halo + tm]
            + cw[2:3] * pltpu.roll(u, ext - 1, axis=0)[halo:halo + tm])
    hm = h[halo:halo + tm]
    bg = _dot(hm, wi_ref[:, 0 * w:1 * w])
    gg = _dot(hm, wi_ref[:, 3 * w:4 * w])
    z = (bg * conv * (gg * _sigmoid(gg))).astype(BF16)
    o_ref[...] = x1[halo:halo + tm] + gate1_ref[0] * _dot(z, wo1_ref[...])


def _tail(x2, fo, ao, gs, gate0, wo0, g1, shift1, scale1, wi, cw, gate1, wo1, n):
    rows, d = x2.shape
    tm = ROW_BLOCK
    bpb = n // tm
    hb = tm // HALO
    nhalo = rows // HALO
    main = lambda w: pl.BlockSpec((tm, w), lambda i: (i, 0))
    prev = lambda w: pl.BlockSpec((HALO, w), lambda i: (jnp.maximum(i * hb - 1, 0), 0))
    nxt = lambda w: pl.BlockSpec((HALO, w), lambda i: (jnp.minimum((i + 1) * hb, nhalo - 1), 0))
    with_halo = lambda w: [prev(w), main(w), nxt(w)]
    per_batch = pl.BlockSpec((1, 1, d), lambda i: (i // bpb, 0, 0))
    kern = functools.partial(_tail_kernel, bpb=bpb)
    return pl.pallas_call(
        kern,
        grid=(rows // tm,),
        in_specs=(with_halo(d) + with_halo(F_WIDTH) + with_halo(ATT_WIDTH) + with_halo(d)
                  + [per_batch, _const_spec(wo0.shape), _const_spec((1, d)), per_batch, per_batch,
                     _const_spec(wi.shape), _const_spec(cw.shape), per_batch, _const_spec(wo1.shape)]),
        out_specs=main(d),
        out_shape=jax.ShapeDtypeStruct((rows, d), F32),
        compiler_params=_params(("parallel",)),
        name="tail",
    )(x2, x2, x2, fo, fo, fo, ao, ao, ao, gs, gs, gs, gate0, wo0, g1, shift1, scale1, wi, cw, gate1, wo1)


@functools.lru_cache(maxsize=None)
def _rope_tables(n):
    rows = n // GRID_W
    pos_r = np.repeat(np.arange(rows, dtype=np.float64), GRID_W)
    pos_c = np.tile(np.arange(GRID_W, dtype=np.float64), rows)
    half = DH // 2
    freqs = ROPE_BASE ** (-np.arange(0, half, 2, dtype=np.float64) / half)
    ang_r = pos_r[:, None] * freqs
    ang_c = pos_c[:, None] * freqs
    cos64 = np.concatenate([np.cos(ang_r), np.cos(ang_r), np.cos(ang_c), np.cos(ang_c)], axis=-1)
    sin64 = np.concatenate([-np.sin(ang_r), np.sin(ang_r), -np.sin(ang_c), np.sin(ang_c)], axis=-1)
    cos = np.concatenate([cos64, cos64], axis=-1).astype(np.float32)
    sin = np.concatenate([sin64, sin64], axis=-1).astype(np.float32)
    return jnp.asarray(cos), jnp.asarray(sin)


@functools.lru_cache(maxsize=None)
def _group_sum_matrix():
    return jnp.asarray(np.kron(np.eye(MXU_DIM // DH), np.ones((DH, DH))), dtype=BF16)


def _lambda_init(layer_idx):
    return 0.8 - 0.6 * math.exp(-0.3 * layer_idx)


def kernel(x, c, ctx, c_ctx, norm_g, ada_w, ada_b, even_w_in, even_q_norm, even_k_norm, even_lambda_q1,
           even_lambda_k1, even_lambda_q2, even_lambda_k2, even_subln, even_w_out, odd_w_in, odd_conv_w,
           odd_w_out):
    batch, n, d = x.shape
    m_ctx = ctx.shape[1]
    assert d == D_MODEL and n % (FFT_N1 * FFT_KB) == 0 and n % ROW_BLOCK == 0 and n % GRID_W == 0
    assert batch < 8 and norm_g.shape[0] == 2

    cvec = jnp.zeros((8, d), F32).at[:batch].set(c).at[batch].set(c_ctx)
    mods = _mod_params(cvec, ada_w, ada_b)
    shift0, scale0, gate0 = [mods[0, :, j * d:(j + 1) * d].reshape(8, 1, d) for j in range(3)]
    shift1, scale1, gate1 = [mods[1, :, j * d:(j + 1) * d].reshape(8, 1, d) for j in range(3)]

    w_in = even_w_in[0].astype(BF16)
    wf = w_in[:, 0:F_WIDTH]
    wq = w_in[:, F_WIDTH:F_WIDTH + QK_WIDTH]
    wk = w_in[:, F_WIDTH + QK_WIDTH:F_WIDTH + 2 * QK_WIDTH]
    wv = w_in[:, F_WIDTH + 2 * QK_WIDTH:F_WIDTH + 2 * QK_WIDTH + ATT_WIDTH]
    wg = w_in[:, F_WIDTH + 2 * QK_WIDTH + ATT_WIDTH:]
    reps = QK_WIDTH // DH
    qg = (jnp.tile(even_q_norm[0].astype(F32), reps) * (ATTN_SCALE * LOG2E)).reshape(1, QK_WIDTH)
    kg = jnp.tile(even_k_norm[0].astype(F32), reps).reshape(1, QK_WIDTH)
    bd = _group_sum_matrix()
    cos, sin = _rope_tables(n)
    g0 = norm_g[0].reshape(1, d)
    g1 = norm_g[1].reshape(1, d)

    x2 = x.reshape(batch * n, d)
    u, q, k, vt, gs = _inproj0(x2, shift0, scale0, g0, cos, sin, qg, kg, bd, wf, wq, wk, wv, wg, batch, n)
    kc, vtc = _ctx_kv(ctx.reshape(batch * m_ctx, d), shift0[batch], scale0[batch], g0, kg, bd, wk, wv,
                      batch, m_ctx)

    lam_p = jnp.stack([even_lambda_q1[0], even_lambda_k1[0], even_lambda_q2[0], even_lambda_k2[0]]).astype(F32)
    sg = even_subln[0].astype(F32).reshape(1, DV)
    ao = _diff_attention(lam_p, q.reshape(batch, n, QK_WIDTH), k.reshape(batch, n, QK_WIDTH),
                         kc.reshape(batch, m_ctx, QK_WIDTH), vt, vtc, sg, _lambda_init(0))

    fo = _fourier_mix(u, batch, n)

    out = _tail(x2, fo, ao.reshape(batch * n, ATT_WIDTH), gs, gate0, even_w_out[0].astype(BF16), g1, shift1,
                scale1, odd_w_in[0].astype(BF16), odd_conv_w[0].astype(F32), gate1, odd_w_out[0].astype(BF16), n)
    return out.reshape(batch, n, d)
```

```python
import functools
import math

import numpy as np
import jax
import jax.numpy as jnp
from jax import lax
from jax.experimental import pallas as pl
from jax.experimental.pallas import tpu as pltpu

F32 = jnp.float32
BF16 = jnp.bfloat16

D_MODEL = 1024
GRID_W = 64
F_GROUPS = 4
F_CH = 128
F_WIDTH = F_GROUPS * F_CH
H_DIFF = 4
DH = 64
DV = 128
QK_WIDTH = H_DIFF * 2 * DH
ATT_WIDTH = H_DIFF * DV
CONV_K = 3
ROPE_BASE = 10000.0
EPS = 1e-6
ATTN_SCALE = DH ** -0.5
LOG2E = 1.4426950408889634

LANES = 128
MXU_DIM = 256
VMEM_LIMIT = 56 * 1024 * 1024

ROW_BLOCK = 1024
HALO = 16
FFT_N1 = 128
FFT_KB = 8
Q_BLOCK_ATT = 1024
KEY_CHUNK_MAX = 1024
NEG_BIG = -1e30
SAFE_SUM_LO = 2.0 ** -60
SAFE_SUM_HI = 2.0 ** 100


def _dot(a, b):
    return jnp.dot(a, b, preferred_element_type=F32)


def _dot_nt(a, b):
    return lax.dot_general(a, b, (((1,), (1,)), ((), ())), preferred_element_type=F32)


def _sigmoid(x):
    return 1.0 / (1.0 + jnp.exp(-x))


def _params(sem, flags=None):
    return pltpu.CompilerParams(dimension_semantics=sem, vmem_limit_bytes=VMEM_LIMIT, flags=flags)


def _mod_kernel(c_ref, w_ref, b_ref, o_ref):
    cv = c_ref[...]
    a = cv * _sigmoid(cv)
    w = w_ref[0]
    a_hi = a.astype(BF16)
    a_lo = (a - a_hi.astype(F32)).astype(BF16)
    w_hi = w.astype(BF16)
    w_lo = (w - w_hi.astype(F32)).astype(BF16)
    acc = _dot(a_hi, w_hi) + _dot(a_hi, w_lo) + _dot(a_lo, w_hi)
    o_ref[0] = acc + b_ref[0]


def _mod_params(cvec, ada_w, ada_b):
    depth, d, d3 = ada_w.shape
    tn = d3
    return pl.pallas_call(
        _mod_kernel,
        grid=(depth, d3 // tn),
        in_specs=[
            pl.BlockSpec((8, d), lambda l, j: (0, 0)),
            pl.BlockSpec((1, d, tn), lambda l, j: (l, 0, j)),
            pl.BlockSpec((1, 1, tn), lambda l, j: (l, 0, j)),
        ],
        out_specs=pl.BlockSpec((1, 8, tn), lambda l, j: (l, 0, j)),
        out_shape=jax.ShapeDtypeStruct((depth, 8, d3), F32),
        compiler_params=_params(("parallel", "parallel")),
        name="mod_params",
    )(cvec, ada_w, ada_b.reshape(depth, 1, d3))


def _modulated(x, g, shift, scale):
    ms = jnp.mean(x * x, axis=-1, keepdims=True)
    y = x * lax.rsqrt(ms + EPS) * g
    return y * (1.0 + scale) + shift


def _group_norm(t, gain, bd):
    sq = (t * t).astype(BF16)
    half = MXU_DIM
    ss = jnp.concatenate([_dot(sq[:, :half], bd), _dot(sq[:, half:], bd)], axis=-1)
    return t * lax.rsqrt(ss * (1.0 / DH) + EPS) * gain


def _rope(t, cos, sin):
    width = t.shape[-1]
    quarter = DH // 4
    lane = lax.broadcasted_iota(jnp.int32, t.shape, 1)
    from_above = pltpu.roll(t, width - quarter, axis=1)
    from_below = pltpu.roll(t, quarter, axis=1)
    partner = jnp.where((lane & (2 * quarter - 1)) < quarter, from_above, from_below)
    reps = width // cos.shape[-1]
    cosf = jnp.concatenate([cos] * reps, axis=-1)
    sinf = jnp.concatenate([sin] * reps, axis=-1)
    return t * cosf + partner * sinf


def _inproj_kernel(x_ref, shift_ref, scale_ref, g_ref, cos_ref, sin_ref, qg_ref, kg_ref, bd_ref,
                   wf_ref, wq_ref, wk_ref, wv_ref, wg_ref,
                   u_ref, q_ref, k_ref, vt_ref, gs_ref, us_ref):
    h = _modulated(x_ref[...], g_ref[...], shift_ref[0], scale_ref[0]).astype(BF16)
    cos = cos_ref[...]
    sin = sin_ref[...]
    bd = bd_ref[...]
    _, n2, n1_blk, _ = u_ref.shape
    pitch = us_ref.shape[1] // n1_blk
    uf = _dot(h, wf_ref[...])
    for a in range(n1_blk):
        for s in range(F_WIDTH // LANES):
            us_ref[s, a * pitch:a * pitch + n2, :] = uf[a * n2:(a + 1) * n2, s * LANES:(s + 1) * LANES]
    for b in range(n2):
        for s in range(F_WIDTH // LANES):
            col = us_ref[s, pl.ds(b, n1_blk, stride=pitch), :]
            u_ref[0, b, :, s * LANES:(s + 1) * LANES] = col.astype(BF16)
    q = _rope(_group_norm(_dot(h, wq_ref[...]), qg_ref[...], bd), cos, sin)
    q_ref[...] = q.astype(BF16)
    k = _rope(_group_norm(_dot(h, wk_ref[...]), kg_ref[...], bd), cos, sin)
    k_ref[...] = k.astype(BF16)
    v = _dot(h, wv_ref[...])
    for hh in range(H_DIFF):
        vt_ref[0, hh] = v[:, hh * DV:(hh + 1) * DV].T.astype(BF16)
    g = _dot(h, wg_ref[...])
    gs_ref[...] = (g * _sigmoid(g)).astype(BF16)


def _ctx_kv_kernel(x_ref, shift_ref, scale_ref, g_ref, kg_ref, bd_ref, wk_ref, wv_ref, k_ref, vt_ref):
    h = _modulated(x_ref[...], g_ref[...], shift_ref[...], scale_ref[...]).astype(BF16)
    k = _group_norm(_dot(h, wk_ref[...]), kg_ref[...], bd_ref[...])
    k_ref[...] = k.astype(BF16)
    v = _dot(h, wv_ref[...])
    for hh in range(H_DIFF):
        vt_ref[0, hh] = v[:, hh * DV:(hh + 1) * DV].T.astype(BF16)


def _const_spec(shape):
    zeros = (0,) * len(shape)
    return pl.BlockSpec(shape, lambda *_: zeros)


def _inproj0(x2, shift, scale, g, cos, sin, qg, kg, bd, wf, wq, wk, wv, wg, batch, n):
    rows, d = x2.shape
    tm = ROW_BLOCK
    bpb = n // tm
    n2 = n // FFT_N1
    n1_blk = tm // n2
    assert tm % n2 == 0 and n2 % 8 == 0 and n1_blk % 16 == 0
    row_spec = lambda w: pl.BlockSpec((tm, w), lambda i: (i, 0))
    return pl.pallas_call(
        _inproj_kernel,
        grid=(rows // tm,),
        in_specs=[
            row_spec(d),
            pl.BlockSpec((1, 1, d), lambda i: (i // bpb, 0, 0)),
            pl.BlockSpec((1, 1, d), lambda i: (i // bpb, 0, 0)),
            _const_spec((1, d)),
            pl.BlockSpec((tm, LANES), lambda i: (i % bpb, 0)),
            pl.BlockSpec((tm, LANES), lambda i: (i % bpb, 0)),
            _const_spec((1, QK_WIDTH)),
            _const_spec((1, QK_WIDTH)),
            _const_spec((MXU_DIM, MXU_DIM)),
            _const_spec(wf.shape), _const_spec(wq.shape), _const_spec(wk.shape),
            _const_spec(wv.shape), _const_spec(wg.shape),
        ],
        out_specs=[
            pl.BlockSpec((1, n2, n1_blk, F_WIDTH), lambda i: (i // bpb, 0, i % bpb, 0)),
            row_spec(QK_WIDTH), row_spec(QK_WIDTH),
            pl.BlockSpec((1, H_DIFF, DV, tm), lambda i: (i // bpb, 0, 0, i % bpb)),
            row_spec(d),
        ],
        out_shape=[
            jax.ShapeDtypeStruct((batch, n2, FFT_N1, F_WIDTH), BF16),
            jax.ShapeDtypeStruct((rows, QK_WIDTH), BF16),
            jax.ShapeDtypeStruct((rows, QK_WIDTH), BF16),
            jax.ShapeDtypeStruct((batch, H_DIFF, DV, n), BF16),
            jax.ShapeDtypeStruct((rows, d), BF16),
        ],
        scratch_shapes=[pltpu.VMEM((F_WIDTH // LANES, n1_blk * (n2 + 4), LANES), F32)],
        compiler_params=_params(("parallel",)),
        name="inproj0",
    )(x2, shift, scale, g, cos, sin, qg, kg, bd, wf, wq, wk, wv, wg)


def _ctx_kv(ctx2, shift, scale, g, kg, bd, wk, wv, batch, m_ctx):
    rows, d = ctx2.shape
    return pl.pallas_call(
        _ctx_kv_kernel,
        grid=(batch,),
        in_specs=[
            pl.BlockSpec((m_ctx, d), lambda b: (b, 0)),
            _const_spec((1, d)), _const_spec((1, d)), _const_spec((1, d)),
            _const_spec((1, QK_WIDTH)), _const_spec((MXU_DIM, MXU_DIM)),
            _const_spec(wk.shape), _const_spec(wv.shape),
        ],
        out_specs=[
            pl.BlockSpec((m_ctx, QK_WIDTH), lambda b: (b, 0)),
            pl.BlockSpec((1, H_DIFF, DV, m_ctx), lambda b: (b, 0, 0, 0)),
        ],
        out_shape=[
            jax.ShapeDtypeStruct((rows, QK_WIDTH), BF16),
            jax.ShapeDtypeStruct((batch, H_DIFF, DV, m_ctx), BF16),
        ],
        compiler_params=_params(("parallel",)),
        name="ctx_kv",
    )(ctx2, shift, scale, g, kg, bd, wk, wv)


def _attn_kernel(lam_ref, q_ref, k_ref, kc_ref, vt_ref, vtc_ref, sg_ref, o_ref,
                 acc1_ref, acc2_ref, p1a_ref, p1b_ref, p2a_ref, p2b_ref, *, tk, lam_init):
    tq = q_ref.shape[1]
    n_lat = k_ref.shape[1] // tk
    m_ctx = kc_ref.shape[1]
    q = q_ref[0]
    lane = lax.broadcasted_iota(jnp.int32, q.shape, 1)
    zero = jnp.zeros_like(q)
    qz = (jnp.where(lane < DH, q, zero), jnp.where(lane >= DH, q, zero))
    acc_refs = (acc1_ref, acc2_ref)
    p_refs = ((p1a_ref, p1b_ref), (p2a_ref, p2b_ref))
    chunks = [(k_ref, vt_ref, j * tk, tk) for j in range(n_lat)] + [(kc_ref, vtc_ref, 0, m_ctx)]

    def finish(l1, l2):
        lp = lam_ref[...]
        lam = (jnp.exp(jnp.sum(lp[0:1] * lp[1:2], axis=-1, keepdims=True))
               - jnp.exp(jnp.sum(lp[2:3] * lp[3:4], axis=-1, keepdims=True)) + lam_init)
        ot = acc1_ref[...] / l1 - lam * (acc2_ref[...] / l2)
        o = ot.T
        ms = jnp.mean(o * o, axis=-1, keepdims=True)
        o = o * lax.rsqrt(ms + EPS) * sg_ref[...] * (1.0 - lam_init)
        o_ref[0] = o.astype(BF16)

    def scores(kr, k0, size, mp):
        return _dot_nt(kr[0, pl.ds(k0, size), :], qz[mp])

    def exp_scores(c):
        kr, _, k0, size = chunks[c]
        sums = []
        for mp in range(2):
            p = jnp.exp2(scores(kr, k0, size, mp))
            p_refs[mp][c % 2][0:size, :] = p.astype(BF16)
            sums.append(jnp.sum(p, axis=0, keepdims=True))
        return sums

    def pv(c):
        _, vr, k0, size = chunks[c]
        vc = vr[0, 0, :, k0:k0 + size]
        for mp in range(2):
            upd = _dot(vc, p_refs[mp][c % 2][0:size, :])
            acc_refs[mp][...] = upd if c == 0 else acc_refs[mp][...] + upd

    l = exp_scores(0)
    for c in range(1, len(chunks)):
        sums = exp_scores(c)
        pv(c - 1)
        l = [l[mp] + sums[mp] for mp in range(2)]
    pv(len(chunks) - 1)
    finish(l[0], l[1])

    in_range = jnp.logical_and(jnp.minimum(l[0], l[1]) > SAFE_SUM_LO, jnp.maximum(l[0], l[1]) < SAFE_SUM_HI)
    all_safe = jnp.min(jnp.where(in_range, 1.0, 0.0)) > 0.5

    @pl.when(jnp.logical_not(all_safe))
    def _():
        def online_step(kr, vc, k0, size, carry):
            out = []
            for mp in range(2):
                m_old, l_old = carry[mp]
                s = scores(kr, k0, size, mp)
                m_new = jnp.maximum(m_old, jnp.max(s, axis=0, keepdims=True))
                alpha = jnp.exp2(m_old - m_new)
                p = jnp.exp2(s - m_new)
                acc_refs[mp][...] = alpha * acc_refs[mp][...] + _dot(vc, p.astype(BF16))
                out.append((m_new, alpha * l_old + jnp.sum(p, axis=0, keepdims=True)))
            return tuple(out)

        def body(c, carry):
            k0 = pl.multiple_of(c * tk, tk)
            return online_step(k_ref, vt_ref[0, 0, :, pl.ds(k0, tk)], k0, tk, carry)

        acc1_ref[...] = jnp.zeros_like(acc1_ref)
        acc2_ref[...] = jnp.zeros_like(acc2_ref)
        start = (jnp.full((1, tq), NEG_BIG, F32), jnp.zeros((1, tq), F32))
        carry = lax.fori_loop(0, n_lat, body, (start, start))
        carry = online_step(kc_ref, vtc_ref[0, 0], 0, m_ctx, carry)
        finish(carry[0][1], carry[1][1])


def _pick_key_chunk(n_keys):
    best = LANES
    for cand in range(LANES, KEY_CHUNK_MAX + 1, LANES):
        if n_keys % cand == 0:
            best = cand
    return best


def _diff_attention(lam_p, q, k, kc, vt, vtc, sg, lam_init):
    batch, n, _ = q.shape
    m_ctx = kc.shape[1]
    tq = Q_BLOCK_ATT
    tk = _pick_key_chunk(n)
    hw = 2 * DH
    kern = functools.partial(_attn_kernel, tk=tk, lam_init=lam_init)
    return pl.pallas_call(
        kern,
        grid=(batch, H_DIFF, n // tq),
        in_specs=[
            _const_spec((4, DH)),
            pl.BlockSpec((1, tq, hw), lambda b, h, i: (b, i, h)),
            pl.BlockSpec((1, n, hw), lambda b, h, i: (b, 0, h)),
            pl.BlockSpec((1, m_ctx, hw), lambda b, h, i: (b, 0, h)),
            pl.BlockSpec((1, 1, DV, n), lambda b, h, i: (b, h, 0, 0)),
            pl.BlockSpec((1, 1, DV, m_ctx), lambda b, h, i: (b, h, 0, 0)),
            _const_spec((1, DV)),
        ],
        out_specs=pl.BlockSpec((1, tq, DV), lambda b, h, i: (b, i, h)),
        out_shape=jax.ShapeDtypeStruct((batch, n, ATT_WIDTH), BF16),
        scratch_shapes=[pltpu.VMEM((DV, tq), F32)] * 2 + [pltpu.VMEM((max(tk, m_ctx), tq), BF16)] * 4,
        compiler_params=_params(("parallel", "parallel", "arbitrary")),
        name="diff_attn",
    )(lam_p, q, k, kc, vt, vtc, sg)


def _fft_rows_kernel(f_ref, u_ref, y_ref):
    for j in range(y_ref.shape[1]):
        y_ref[0, j] = _dot(f_ref[...], u_ref[0, j]).astype(BF16)


def _fft_rows(f1cs, u4):
    batch, n2, n1, width = u4.shape
    per_step = min(n2, 8)
    return pl.pallas_call(
        _fft_rows_kernel,
        grid=(batch, n2 // per_step),
        in_specs=[
            _const_spec(f1cs.shape),
            pl.BlockSpec((1, per_step, n1, width), lambda b, j: (b, j, 0, 0)),
        ],
        out_specs=pl.BlockSpec((1, per_step, 2 * n1, width), lambda b, j: (b, j, 0, 0)),
        out_shape=jax.ShapeDtypeStruct((batch, n2, 2 * n1, width), BF16),
        compiler_params=_params(("parallel", "parallel")),
        name="fft_rows",
    )(f1cs, u4)


def _fft_cols_kernel(g_ref, y_ref, bdc_ref, bds_ref, o_ref):
    _, n2, kb2, width = y_ref.shape
    kb = kb2 // 2
    rows = kb * n2
    y = y_ref[0].reshape(2 * rows, width)
    z = _dot(g_ref[0], y)
    xr = z[:rows].astype(BF16)
    xi = z[rows:].astype(BF16)
    bdc = bdc_ref[...]
    bds = bds_ref[...]
    half = MXU_DIM
    o = jnp.concatenate(
        [_dot(xr[:, :half], bdc) + _dot(xi[:, :half], bds),
         _dot(xr[:, half:], bdc) + _dot(xi[:, half:], bds)], axis=-1)
    o_ref[0] = o.reshape(n2, kb, width)


def _fft_cols(gbig, y4, bdc, bds):
    batch, n2, rows2, width = y4.shape
    n1 = rows2 // 2
    kb = FFT_KB
    gr = gbig.shape[1]
    return pl.pallas_call(
        _fft_cols_kernel,
        grid=(n1 // kb, batch),
        in_specs=[
            pl.BlockSpec((1, gr, gr), lambda j, b: (j, 0, 0)),
            pl.BlockSpec((1, n2, 2 * kb, width), lambda j, b: (b, 0, j, 0)),
            _const_spec(bdc.shape), _const_spec(bds.shape),
        ],
        out_specs=pl.BlockSpec((1, n2, kb, width), lambda j, b: (b, 0, j, 0)),
        out_shape=jax.ShapeDtypeStruct((batch, n2, n1, width), F32),
        compiler_params=_params(("parallel", "arbitrary")),
        name="fft_cols",
    )(gbig, y4, bdc, bds)


@functools.lru_cache(maxsize=None)
def _fft_tables(n):
    n1 = FFT_N1
    n2 = n // n1
    kb = FFT_KB
    k1 = np.arange(n1)
    ang = 2.0 * np.pi * np.outer(k1, k1) / n1
    f1cs = np.stack([np.cos(ang), -np.sin(ang)], axis=1).reshape(2 * n1, n1)
    scale = 1.0 / math.sqrt(n * F_CH)
    nblk = n1 // kb
    rows = n2 * kb
    gbig = np.zeros((nblk, 2, n2, kb, n2, kb, 2), np.float32)
    n2i = np.arange(n2)
    for j in range(kb):
        kk = (np.arange(nblk)[:, None] * kb + j) + n1 * n2i[None, :]
        a = 2.0 * np.pi * (kk[:, :, None] * n2i[None, None, :] % n) / n
        gr = np.cos(a) * scale
        gi = -np.sin(a) * scale
        gbig[:, 0, :, j, :, j, 0] = gr
        gbig[:, 0, :, j, :, j, 1] = -gi
        gbig[:, 1, :, j, :, j, 0] = gi
        gbig[:, 1, :, j, :, j, 1] = gr
    gbig = gbig.reshape(nblk, 2 * rows, 2 * rows)
    c = np.arange(F_CH)
    angc = 2.0 * np.pi * np.outer(c, c) / F_CH
    eye2 = np.eye(MXU_DIM // F_CH)
    bdc = np.kron(eye2, np.cos(angc))
    bds = np.kron(eye2, np.sin(angc))
    to_bf16 = lambda a: jnp.asarray(a, dtype=F32).astype(BF16)
    return to_bf16(f1cs), to_bf16(gbig), to_bf16(bdc), to_bf16(bds)


def _fourier_mix(u4, batch, n):
    f1cs, gbig, bdc, bds = _fft_tables(n)
    y = _fft_rows(f1cs, u4)
    fo = _fft_cols(gbig, y, bdc, bds)
    return fo.reshape(batch * n, F_WIDTH)


def _tail_kernel(xp_ref, x_ref, xn_ref, fp_ref, f_ref, fn_ref, ap_ref, a_ref, an_ref, gp_ref, g_ref, gn_ref,
                 gate0_ref, wo0_ref, g1_ref, shift_ref, scale_ref, wi_ref, cw_ref, gate1_ref, wo1_ref,
                 o_ref, *, bpb):
    i = pl.program_id(0)
    tm = x_ref.shape[0]
    halo = xp_ref.shape[0]
    ext = tm + 2 * halo
    cat = lambda p, m, nx: jnp.concatenate([p[...], m[...], nx[...]], axis=0)
    x = cat(xp_ref, x_ref, xn_ref)
    gs = cat(gp_ref, g_ref, gn_ref).astype(F32)
    yf = (cat(fp_ref, f_ref, fn_ref) * gs[:, :F_WIDTH]).astype(BF16)
    ya = (cat(ap_ref, a_ref, an_ref).astype(F32) * gs[:, F_WIDTH:]).astype(BF16)
    y = _dot(yf, wo0_ref[:F_WIDTH, :]) + _dot(ya, wo0_ref[F_WIDTH:, :])
    x1 = x + gate0_ref[0] * y
    h = _modulated(x1, g1_ref[...], shift_ref[0], scale_ref[0]).astype(BF16)
    w = x.shape[-1]
    u = _dot(h, wi_ref[:, 1 * w:2 * w]) * _dot(h, wi_ref[:, 2 * w:3 * w])
    row = lax.broadcasted_iota(jnp.int32, u.shape, 0)
    lo = jnp.where((i % bpb) == 0, halo, 0)
    hi = jnp.where((i % bpb) == bpb - 1, halo + tm, ext)
    u = jnp.where(jnp.logical_and(row >= lo, row < hi), u, 0.0)
    cw = cw_ref[...]
    conv = (cw[0:1] * pltpu.roll(u, 1, axis=0)[halo:halo + tm]
            + cw[1:2] * u[halo:halo + tm]
            + cw[2:3] * pltpu.roll(u, ext - 1, axis=0)[halo:halo + tm])
    hm = h[halo:halo + tm]
    bg = _dot(hm, wi_ref[:, 0 * w:1 * w])
    gg = _dot(hm, wi_ref[:, 3 * w:4 * w])
    z = (bg * conv * (gg * _sigmoid(gg))).astype(BF16)
    o_ref[...] = x1[halo:halo + tm] + gate1_ref[0] * _dot(z, wo1_ref[...])


def _tail(x2, fo, ao, gs, gate0, wo0, g1, shift1, scale1, wi, cw, gate1, wo1, n):
    rows, d = x2.shape
    tm = ROW_BLOCK
    bpb = n // tm
    hb = tm // HALO
    nhalo = rows // HALO
    main = lambda w: pl.BlockSpec((tm, w), lambda i: (i, 0))
    prev = lambda w: pl.BlockSpec((HALO, w), lambda i: (jnp.maximum(i * hb - 1, 0), 0))
    nxt = lambda w: pl.BlockSpec((HALO, w), lambda i: (jnp.minimum((i + 1) * hb, nhalo - 1), 0))
    with_halo = lambda w: [prev(w), main(w), nxt(w)]
    per_batch = pl.BlockSpec((1, 1, d), lambda i: (i // bpb, 0, 0))
    kern = functools.partial(_tail_kernel, bpb=bpb)
    return pl.pallas_call(
        kern,
        grid=(rows // tm,),
        in_specs=(with_halo(d) + with_halo(F_WIDTH) + with_halo(ATT_WIDTH) + with_halo(d)
                  + [per_batch, _const_spec(wo0.shape), _const_spec((1, d)), per_batch, per_batch,
                     _const_spec(wi.shape), _const_spec(cw.shape), per_batch, _const_spec(wo1.shape)]),
        out_specs=main(d),
        out_shape=jax.ShapeDtypeStruct((rows, d), F32),
        compiler_params=_params(("parallel",)),
        name="tail",
    )(x2, x2, x2, fo, fo, fo, ao, ao, ao, gs, gs, gs, gate0, wo0, g1, shift1, scale1, wi, cw, gate1, wo1)


@functools.lru_cache(maxsize=None)
def _rope_tables(n):
    rows = n // GRID_W
    pos_r = np.repeat(np.arange(rows, dtype=np.float64), GRID_W)
    pos_c = np.tile(np.arange(GRID_W, dtype=np.float64), rows)
    half = DH // 2
    freqs = ROPE_BASE ** (-np.arange(0, half, 2, dtype=np.float64) / half)
    ang_r = pos_r[:, None] * freqs
    ang_c = pos_c[:, None] * freqs
    cos64 = np.concatenate([np.cos(ang_r), np.cos(ang_r), np.cos(ang_c), np.cos(ang_c)], axis=-1)
    sin64 = np.concatenate([-np.sin(ang_r), np.sin(ang_r), -np.sin(ang_c), np.sin(ang_c)], axis=-1)
    cos = np.concatenate([cos64, cos64], axis=-1).astype(np.float32)
    sin = np.concatenate([sin64, sin64], axis=-1).astype(np.float32)
    return jnp.asarray(cos), jnp.asarray(sin)


@functools.lru_cache(maxsize=None)
def _group_sum_matrix():
    return jnp.asarray(np.kron(np.eye(MXU_DIM // DH), np.ones((DH, DH))), dtype=BF16)


def _lambda_init(layer_idx):
    return 0.8 - 0.6 * math.exp(-0.3 * layer_idx)


def kernel(x, c, ctx, c_ctx, norm_g, ada_w, ada_b, even_w_in, even_q_norm, even_k_norm, even_lambda_q1,
           even_lambda_k1, even_lambda_q2, even_lambda_k2, even_subln, even_w_out, odd_w_in, odd_conv_w,
           odd_w_out):
    batch, n, d = x.shape
    m_ctx = ctx.shape[1]
    assert d == D_MODEL and n % (FFT_N1 * FFT_KB) == 0 and n % ROW_BLOCK == 0 and n % GRID_W == 0
    assert batch < 8 and norm_g.shape[0] == 2

    cvec = jnp.zeros((8, d), F32).at[:batch].set(c).at[batch].set(c_ctx)
    mods = _mod_params(cvec, ada_w, ada_b)
    shift0, scale0, gate0 = [mods[0, :, j * d:(j + 1) * d].reshape(8, 1, d) for j in range(3)]
    shift1, scale1, gate1 = [mods[1, :, j * d:(j + 1) * d].reshape(8, 1, d) for j in range(3)]

    w_in = even_w_in[0].astype(BF16)
    wf = w_in[:, 0:F_WIDTH]
    wq = w_in[:, F_WIDTH:F_WIDTH + QK_WIDTH]
    wk = w_in[:, F_WIDTH + QK_WIDTH:F_WIDTH + 2 * QK_WIDTH]
    wv = w_in[:, F_WIDTH + 2 * QK_WIDTH:F_WIDTH + 2 * QK_WIDTH + ATT_WIDTH]
    wg = w_in[:, F_WIDTH + 2 * QK_WIDTH + ATT_WIDTH:]
    reps = QK_WIDTH // DH
    qg = (jnp.tile(even_q_norm[0].astype(F32), reps) * (ATTN_SCALE * LOG2E)).reshape(1, QK_WIDTH)
    kg = jnp.tile(even_k_norm[0].astype(F32), reps).reshape(1, QK_WIDTH)
    bd = _group_sum_matrix()
    cos, sin = _rope_tables(n)
    g0 = norm_g[0].reshape(1, d)
    g1 = norm_g[1].reshape(1, d)

    x2 = x.reshape(batch * n, d)
    u, q, k, vt, gs = _inproj0(x2, shift0, scale0, g0, cos, sin, qg, kg, bd, wf, wq, wk, wv, wg, batch, n)
    kc, vtc = _ctx_kv(ctx.reshape(batch * m_ctx, d), shift0[batch], scale0[batch], g0, kg, bd, wk, wv,
                      batch, m_ctx)

    lam_p = jnp.stack([even_lambda_q1[0], even_lambda_k1[0], even_lambda_q2[0], even_lambda_k2[0]]).astype(F32)
    sg = even_subln[0].astype(F32).reshape(1, DV)
    ao = _diff_attention(lam_p, q.reshape(batch, n, QK_WIDTH), k.reshape(batch, n, QK_WIDTH),
                         kc.reshape(batch, m_ctx, QK_WIDTH), vt, vtc, sg, _lambda_init(0))

    fo = _fourier_mix(u, batch, n)

    out = _tail(x2, fo, ao.reshape(batch * n, ATT_WIDTH), gs, gate0, even_w_out[0].astype(BF16), g1, shift1,
                scale1, odd_w_in[0].astype(BF16), odd_conv_w[0].astype(F32), gate1, odd_w_out[0].astype(BF16), n)
    return out.reshape(batch, n, d)
```

```python
import functools
import math

import numpy as np
import jax
import jax.numpy as jnp
from jax import lax
from jax.experimental import pallas as pl
from jax.experimental.pallas import tpu as pltpu

F32 = jnp.float32
BF16 = jnp.bfloat16

D_MODEL = 1024
GRID_W = 64
F_GROUPS = 4
F_CH = 128
F_WIDTH = F_GROUPS * F_CH
H_DIFF = 4
DH = 64
DV = 128
QK_WIDTH = H_DIFF * 2 * DH
ATT_WIDTH = H_DIFF * DV
ROPE_BASE = 10000.0
EPS = 1e-6
ATTN_SCALE = DH ** -0.5
LOG2E = 1.4426950408889634

LANES = 128
MXU_DIM = 256
VMEM_LIMIT = 56 * 1024 * 1024

ROW_BLOCK = 1024
HALO = 16
FFT_N1 = 128
FFT_KB = 8
Q_BLOCK_ATT = 1024
KEY_CHUNK_MAX = 1024
NEG_BIG = -1e30
SAFE_SUM_LO = 2.0 ** -60
SAFE_SUM_HI = 2.0 ** 100


def _dot(a, b):
    return jnp.dot(a, b, preferred_element_type=F32)


def _dot_nt(a, b):
    return lax.dot_general(a, b, (((1,), (1,)), ((), ())), preferred_element_type=F32)


def _sigmoid(x):
    return 1.0 / (1.0 + jnp.exp(-x))


def _params(sem):
    return pltpu.CompilerParams(dimension_semantics=sem, vmem_limit_bytes=VMEM_LIMIT)


def _mod_kernel(c_ref, w_ref, b_ref, o_ref):
    cv = c_ref[...]
    a = cv * _sigmoid(cv)
    w = w_ref[0]
    a_hi = a.astype(BF16)
    a_lo = (a - a_hi.astype(F32)).astype(BF16)
    w_hi = w.astype(BF16)
    w_lo = (w - w_hi.astype(F32)).astype(BF16)
    acc = _dot(a_hi, w_hi) + _dot(a_hi, w_lo) + _dot(a_lo, w_hi)
    o_ref[0] = acc + b_ref[0]


def _mod_params(cvec, ada_w, ada_b):
    depth, d, d3 = ada_w.shape
    return pl.pallas_call(
        _mod_kernel,
        grid=(depth,),
        in_specs=[
            pl.BlockSpec((8, d), lambda l: (0, 0)),
            pl.BlockSpec((1, d, d3), lambda l: (l, 0, 0)),
            pl.BlockSpec((1, 1, d3), lambda l: (l, 0, 0)),
        ],
        out_specs=pl.BlockSpec((1, 8, d3), lambda l: (l, 0, 0)),
        out_shape=jax.ShapeDtypeStruct((depth, 8, d3), F32),
        compiler_params=_params(("parallel",)),
        name="mod_params",
    )(cvec, ada_w, ada_b.reshape(depth, 1, d3))


def _modulated(x, g, shift, scale):
    ms = jnp.mean(x * x, axis=-1, keepdims=True)
    y = x * lax.rsqrt(ms + EPS) * g
    return y * (1.0 + scale) + shift


def _group_norm(t, gain, bd):
    sq = (t * t).astype(BF16)
    half = MXU_DIM
    ss = jnp.concatenate([_dot(sq[:, :half], bd), _dot(sq[:, half:], bd)], axis=-1)
    return t * lax.rsqrt(ss * (1.0 / DH) + EPS) * gain


def _rope(t, cos, sin):
    width = t.shape[-1]
    quarter = DH // 4
    lane = lax.broadcasted_iota(jnp.int32, t.shape, 1)
    from_above = pltpu.roll(t, width - quarter, axis=1)
    from_below = pltpu.roll(t, quarter, axis=1)
    partner = jnp.where((lane & (2 * quarter - 1)) < quarter, from_above, from_below)
    reps = width // cos.shape[-1]
    cosf = jnp.concatenate([cos] * reps, axis=-1)
    sinf = jnp.concatenate([sin] * reps, axis=-1)
    return t * cosf + partner * sinf


def _inproj_kernel(x_ref, shift_ref, scale_ref, g_ref, cos_ref, sin_ref, qg_ref, kg_ref, bd_ref,
                   wf_ref, wq_ref, wk_ref, wv_ref, wg_ref,
                   u_ref, q_ref, k_ref, vt_ref, gs_ref, us_ref):
    h = _modulated(x_ref[...], g_ref[...], shift_ref[0], scale_ref[0]).astype(BF16)
    cos = cos_ref[...]
    sin = sin_ref[...]
    bd = bd_ref[...]
    _, n2, n1_blk, _ = u_ref.shape
    pitch = us_ref.shape[1] // n1_blk
    uf = _dot(h, wf_ref[...])
    for a in range(n1_blk):
        for s in range(F_WIDTH // LANES):
            us_ref[s, a * pitch:a * pitch + n2, :] = uf[a * n2:(a + 1) * n2, s * LANES:(s + 1) * LANES]
    for b in range(n2):
        for s in range(F_WIDTH // LANES):
            col = us_ref[s, pl.ds(b, n1_blk, stride=pitch), :]
            u_ref[0, b, :, s * LANES:(s + 1) * LANES] = col.astype(BF16)
    q = _rope(_group_norm(_dot(h, wq_ref[...]), qg_ref[...], bd), cos, sin)
    q_ref[...] = q.astype(BF16)
    k = _rope(_group_norm(_dot(h, wk_ref[...]), kg_ref[...], bd), cos, sin)
    k_ref[...] = k.astype(BF16)
    v = _dot(h, wv_ref[...])
    for hh in range(H_DIFF):
        vt_ref[0, hh] = v[:, hh * DV:(hh + 1) * DV].T.astype(BF16)
    g = _dot(h, wg_ref[...])
    gs_ref[...] = (g * _sigmoid(g)).astype(BF16)


def _ctx_kv_kernel(x_ref, shift_ref, scale_ref, g_ref, kg_ref, bd_ref, wk_ref, wv_ref, k_ref, vt_ref):
    h = _modulated(x_ref[...], g_ref[...], shift_ref[...], scale_ref[...]).astype(BF16)
    k = _group_norm(_dot(h, wk_ref[...]), kg_ref[...], bd_ref[...])
    k_ref[...] = k.astype(BF16)
    v = _dot(h, wv_ref[...])
    for hh in range(H_DIFF):
        vt_ref[0, hh] = v[:, hh * DV:(hh + 1) * DV].T.astype(BF16)


def _const_spec(shape):
    zeros = (0,) * len(shape)
    return pl.BlockSpec(shape, lambda *_: zeros)


def _inproj0(x2, shift, scale, g, cos, sin, qg, kg, bd, wf, wq, wk, wv, wg, batch, n):
    rows, d = x2.shape
    tm = ROW_BLOCK
    bpb = n // tm
    n2 = n // FFT_N1
    n1_blk = tm // n2
    assert tm % n2 == 0 and n2 % 8 == 0 and n1_blk % 16 == 0
    row_spec = lambda w: pl.BlockSpec((tm, w), lambda i: (i, 0))
    return pl.pallas_call(
        _inproj_kernel,
        grid=(rows // tm,),
        in_specs=[
            row_spec(d),
            pl.BlockSpec((1, 1, d), lambda i: (i // bpb, 0, 0)),
            pl.BlockSpec((1, 1, d), lambda i: (i // bpb, 0, 0)),
            _const_spec((1, d)),
            pl.BlockSpec((tm, LANES), lambda i: (i % bpb, 0)),
            pl.BlockSpec((tm, LANES), lambda i: (i % bpb, 0)),
            _const_spec((1, QK_WIDTH)),
            _const_spec((1, QK_WIDTH)),
            _const_spec((MXU_DIM, MXU_DIM)),
            _const_spec(wf.shape), _const_spec(wq.shape), _const_spec(wk.shape),
            _const_spec(wv.shape), _const_spec(wg.shape),
        ],
        out_specs=[
            pl.BlockSpec((1, n2, n1_blk, F_WIDTH), lambda i: (i // bpb, 0, i % bpb, 0)),
            row_spec(QK_WIDTH), row_spec(QK_WIDTH),
            pl.BlockSpec((1, H_DIFF, DV, tm), lambda i: (i // bpb, 0, 0, i % bpb)),
            row_spec(d),
        ],
        out_shape=[
            jax.ShapeDtypeStruct((batch, n2, FFT_N1, F_WIDTH), BF16),
            jax.ShapeDtypeStruct((rows, QK_WIDTH), BF16),
            jax.ShapeDtypeStruct((rows, QK_WIDTH), BF16),
            jax.ShapeDtypeStruct((batch, H_DIFF, DV, n), BF16),
            jax.ShapeDtypeStruct((rows, d), BF16),
        ],
        scratch_shapes=[pltpu.VMEM((F_WIDTH // LANES, n1_blk * (n2 + 4), LANES), F32)],
        compiler_params=_params(("parallel",)),
        name="inproj0",
    )(x2, shift, scale, g, cos, sin, qg, kg, bd, wf, wq, wk, wv, wg)


def _ctx_kv(ctx2, shift, scale, g, kg, bd, wk, wv, batch, m_ctx):
    rows, d = ctx2.shape
    return pl.pallas_call(
        _ctx_kv_kernel,
        grid=(batch,),
        in_specs=[
            pl.BlockSpec((m_ctx, d), lambda b: (b, 0)),
            _const_spec((1, d)), _const_spec((1, d)), _const_spec((1, d)),
            _const_spec((1, QK_WIDTH)), _const_spec((MXU_DIM, MXU_DIM)),
            _const_spec(wk.shape), _const_spec(wv.shape),
        ],
        out_specs=[
            pl.BlockSpec((m_ctx, QK_WIDTH), lambda b: (b, 0)),
            pl.BlockSpec((1, H_DIFF, DV, m_ctx), lambda b: (b, 0, 0, 0)),
        ],
        out_shape=[
            jax.ShapeDtypeStruct((rows, QK_WIDTH), BF16),
            jax.ShapeDtypeStruct((batch, H_DIFF, DV, m_ctx), BF16),
        ],
        compiler_params=_params(("parallel",)),
        name="ctx_kv",
    )(ctx2, shift, scale, g, kg, bd, wk, wv)


def _attn_kernel(lam_ref, q_ref, k_ref, kc_ref, vt_ref, vtc_ref, sg_ref, o_ref,
                 acc1_ref, acc2_ref, p1a_ref, p1b_ref, p2a_ref, p2b_ref, *, tk, lam_init):
    tq = q_ref.shape[1]
    n_lat = k_ref.shape[1] // tk
    m_ctx = kc_ref.shape[1]
    q = q_ref[0]
    lane = lax.broadcasted_iota(jnp.int32, q.shape, 1)
    zero = jnp.zeros_like(q)
    qz = (jnp.where(lane < DH, q, zero), jnp.where(lane >= DH, q, zero))
    acc_refs = (acc1_ref, acc2_ref)
    p_refs = ((p1a_ref, p1b_ref), (p2a_ref, p2b_ref))
    chunks = [(k_ref, vt_ref, j * tk, tk) for j in range(n_lat)] + [(kc_ref, vtc_ref, 0, m_ctx)]

    def finish(l1, l2):
        lp = lam_ref[...]
        lam = (jnp.exp(jnp.sum(lp[0:1] * lp[1:2], axis=-1, keepdims=True))
               - jnp.exp(jnp.sum(lp[2:3] * lp[3:4], axis=-1, keepdims=True)) + lam_init)
        ot = acc1_ref[...] / l1 - lam * (acc2_ref[...] / l2)
        o = ot.T
        ms = jnp.mean(o * o, axis=-1, keepdims=True)
        o = o * lax.rsqrt(ms + EPS) * sg_ref[...] * (1.0 - lam_init)
        o_ref[0] = o.astype(BF16)

    def scores(kr, k0, size, mp):
        return _dot_nt(kr[0, pl.ds(k0, size), :], qz[mp])

    def exp_scores(c):
        kr, _, k0, size = chunks[c]
        sums = []
        for mp in range(2):
            p = jnp.exp2(scores(kr, k0, size, mp))
            p_refs[mp][c % 2][0:size, :] = p.astype(BF16)
            sums.append(jnp.sum(p, axis=0, keepdims=True))
        return sums

    def pv(c):
        _, vr, k0, size = chunks[c]
        vc = vr[0, 0, :, k0:k0 + size]
        for mp in range(2):
            upd = _dot(vc, p_refs[mp][c % 2][0:size, :])
            acc_refs[mp][...] = upd if c == 0 else acc_refs[mp][...] + upd

    l = exp_scores(0)
    for c in range(1, len(chunks)):
        sums = exp_scores(c)
        pv(c - 1)
        l = [l[mp] + sums[mp] for mp in range(2)]
    pv(len(chunks) - 1)
    finish(l[0], l[1])

    in_range = jnp.logical_and(jnp.minimum(l[0], l[1]) > SAFE_SUM_LO, jnp.maximum(l[0], l[1]) < SAFE_SUM_HI)
    all_safe = jnp.min(jnp.where(in_range, 1.0, 0.0)) > 0.5

    @pl.when(jnp.logical_not(all_safe))
    def _():
        def online_step(kr, vc, k0, size, carry):
            out = []
            for mp in range(2):
                m_old, l_old = carry[mp]
                s = scores(kr, k0, size, mp)
                m_new = jnp.maximum(m_old, jnp.max(s, axis=0, keepdims=True))
                alpha = jnp.exp2(m_old - m_new)
                p = jnp.exp2(s - m_new)
                acc_refs[mp][...] = alpha * acc_refs[mp][...] + _dot(vc, p.astype(BF16))
                out.append((m_new, alpha * l_old + jnp.sum(p, axis=0, keepdims=True)))
            return tuple(out)

        def body(c, carry):
            k0 = pl.multiple_of(c * tk, tk)
            return online_step(k_ref, vt_ref[0, 0, :, pl.ds(k0, tk)], k0, tk, carry)

        acc1_ref[...] = jnp.zeros_like(acc1_ref)
        acc2_ref[...] = jnp.zeros_like(acc2_ref)
        start = (jnp.full((1, tq), NEG_BIG, F32), jnp.zeros((1, tq), F32))
        carry = lax.fori_loop(0, n_lat, body, (start, start))
        carry = online_step(kc_ref, vtc_ref[0, 0], 0, m_ctx, carry)
        finish(carry[0][1], carry[1][1])


def _pick_key_chunk(n_keys):
    best = LANES
    for cand in range(LANES, KEY_CHUNK_MAX + 1, LANES):
        if n_keys % cand == 0:
            best = cand
    return best


def _diff_attention(lam_p, q, k, kc, vt, vtc, sg, lam_init):
    batch, n, _ = q.shape
    m_ctx = kc.shape[1]
    tq = Q_BLOCK_ATT
    tk = _pick_key_chunk(n)
    hw = 2 * DH
    kern = functools.partial(_attn_kernel, tk=tk, lam_init=lam_init)
    return pl.pallas_call(
        kern,
        grid=(batch, H_DIFF, n // tq),
        in_specs=[
            _const_spec((4, DH)),
            pl.BlockSpec((1, tq, hw), lambda b, h, i: (b, i, h)),
            pl.BlockSpec((1, n, hw), lambda b, h, i: (b, 0, h)),
            pl.BlockSpec((1, m_ctx, hw), lambda b, h, i: (b, 0, h)),
            pl.BlockSpec((1, 1, DV, n), lambda b, h, i: (b, h, 0, 0)),
            pl.BlockSpec((1, 1, DV, m_ctx), lambda b, h, i: (b, h, 0, 0)),
            _const_spec((1, DV)),
        ],
        out_specs=pl.BlockSpec((1, tq, DV), lambda b, h, i: (b, i, h)),
        out_shape=jax.ShapeDtypeStruct((batch, n, ATT_WIDTH), BF16),
        scratch_shapes=[pltpu.VMEM((DV, tq), F32)] * 2 + [pltpu.VMEM((max(tk, m_ctx), tq), BF16)] * 4,
        compiler_params=_params(("parallel", "parallel", "arbitrary")),
        name="diff_attn",
    )(lam_p, q, k, kc, vt, vtc, sg)


def _fft_rows_kernel(f_ref, u_ref, y_ref):
    for j in range(y_ref.shape[1]):
        y_ref[0, j] = _dot(f_ref[...], u_ref[0, j]).astype(BF16)


def _fft_rows(f1cs, u4):
    batch, n2, n1, width = u4.shape
    per_step = min(n2, 16)
    return pl.pallas_call(
        _fft_rows_kernel,
        grid=(batch, n2 // per_step),
        in_specs=[
            _const_spec(f1cs.shape),
            pl.BlockSpec((1, per_step, n1, width), lambda b, j: (b, j, 0, 0)),
        ],
        out_specs=pl.BlockSpec((1, per_step, 2 * n1, width), lambda b, j: (b, j, 0, 0)),
        out_shape=jax.ShapeDtypeStruct((batch, n2, 2 * n1, width), BF16),
        compiler_params=_params(("parallel", "parallel")),
        name="fft_rows",
    )(f1cs, u4)


def _fft_cols_kernel(g_ref, y_ref, bdc_ref, bds_ref, o_ref):
    _, n2, kb2, width = y_ref.shape
    kb = kb2 // 2
    rows = kb * n2
    y = y_ref[0].reshape(2 * rows, width)
    z = _dot(g_ref[0], y)
    xr = z[:rows].astype(BF16)
    xi = z[rows:].astype(BF16)
    bdc = bdc_ref[...]
    bds = bds_ref[...]
    half = MXU_DIM
    o = jnp.concatenate(
        [_dot(xr[:, :half], bdc) + _dot(xi[:, :half], bds),
         _dot(xr[:, half:], bdc) + _dot(xi[:, half:], bds)], axis=-1)
    o_ref[0] = o.reshape(n2, kb, width)


def _fft_cols(gbig, y4, bdc, bds):
    batch, n2, rows2, width = y4.shape
    n1 = rows2 // 2
    kb = FFT_KB
    gr = gbig.shape[1]
    return pl.pallas_call(
        _fft_cols_kernel,
        grid=(n1 // kb, batch),
        in_specs=[
            pl.BlockSpec((1, gr, gr), lambda j, b: (j, 0, 0)),
            pl.BlockSpec((1, n2, 2 * kb, width), lambda j, b: (b, 0, j, 0)),
            _const_spec(bdc.shape), _const_spec(bds.shape),
        ],
        out_specs=pl.BlockSpec((1, n2, kb, width), lambda j, b: (b, 0, j, 0)),
        out_shape=jax.ShapeDtypeStruct((batch, n2, n1, width), F32),
        compiler_params=_params(("parallel", "arbitrary")),
        name="fft_cols",
    )(gbig, y4, bdc, bds)


@functools.lru_cache(maxsize=None)
def _fft_tables(n):
    n1 = FFT_N1
    n2 = n // n1
    kb = FFT_KB
    k1 = np.arange(n1)
    ang = 2.0 * np.pi * np.outer(k1, k1) / n1
    f1cs = np.stack([np.cos(ang), -np.sin(ang)], axis=1).reshape(2 * n1, n1)
    scale = 1.0 / math.sqrt(n * F_CH)
    nblk = n1 // kb
    rows = n2 * kb
    gbig = np.zeros((nblk, 2, n2, kb, n2, kb, 2), np.float32)
    n2i = np.arange(n2)
    for j in range(kb):
        kk = (np.arange(nblk)[:, None] * kb + j) + n1 * n2i[None, :]
        a = 2.0 * np.pi * (kk[:, :, None] * n2i[None, None, :] % n) / n
        gr = np.cos(a) * scale
        gi = -np.sin(a) * scale
        gbig[:, 0, :, j, :, j, 0] = gr
        gbig[:, 0, :, j, :, j, 1] = -gi
        gbig[:, 1, :, j, :, j, 0] = gi
        gbig[:, 1, :, j, :, j, 1] = gr
    gbig = gbig.reshape(nblk, 2 * rows, 2 * rows)
    c = np.arange(F_CH)
    angc = 2.0 * np.pi * np.outer(c, c) / F_CH
    eye2 = np.eye(MXU_DIM // F_CH)
    bdc = np.kron(eye2, np.cos(angc))
    bds = np.kron(eye2, np.sin(angc))
    to_bf16 = lambda a: jnp.asarray(a, dtype=F32).astype(BF16)
    return to_bf16(f1cs), to_bf16(gbig), to_bf16(bdc), to_bf16(bds)


def _fourier_mix(u4, batch, n):
    f1cs, gbig, bdc, bds = _fft_tables(n)
    y = _fft_rows(f1cs, u4)
    fo = _fft_cols(gbig, y, bdc, bds)
    return fo.reshape(batch * n, F_WIDTH)


def _tail_kernel(xp_ref, x_ref, xn_ref, fp_ref, f_ref, fn_ref, ap_ref, a_ref, an_ref, gp_ref, g_ref, gn_ref,
                 gate0_ref, wo0_ref, g1_ref, shift_ref, scale_ref, wi_ref, cw_ref, gate1_ref, wo1_ref,
                 o_ref, *, bpb):
    i = pl.program_id(0)
    tm = x_ref.shape[0]
    halo = xp_ref.shape[0]
    ext = tm + 2 * halo
    cat = lambda p, m, nx: jnp.concatenate([p[...], m[...], nx[...]], axis=0)
    x = cat(xp_ref, x_ref, xn_ref)
    gs = cat(gp_ref, g_ref, gn_ref).astype(F32)
    yf = (cat(fp_ref, f_ref, fn_ref) * gs[:, :F_WIDTH]).astype(BF16)
    ya = (cat(ap_ref, a_ref, an_ref).astype(F32) * gs[:, F_WIDTH:]).astype(BF16)
    y = _dot(yf, wo0_ref[:F_WIDTH, :]) + _dot(ya, wo0_ref[F_WIDTH:, :])
    x1 = x + gate0_ref[0] * y
    h = _modulated(x1, g1_ref[...], shift_ref[0], scale_ref[0]).astype(BF16)
    w = x.shape[-1]
    u = _dot(h, wi_ref[:, 1 * w:2 * w]) * _dot(h, wi_ref[:, 2 * w:3 * w])
    row = lax.broadcasted_iota(jnp.int32, u.shape, 0)
    lo = jnp.where((i % bpb) == 0, halo, 0)
    hi = jnp.where((i % bpb) == bpb - 1, halo + tm, ext)
    u = jnp.where(jnp.logical_and(row >= lo, row < hi), u, 0.0)
    cw = cw_ref[...]
    conv = (cw[0:1] * pltpu.roll(u, 1, axis=0)[halo:halo + tm]
            + cw[1:2] * u[halo:halo + tm]
            + cw[2:3] * pltpu.roll(u, ext - 1, axis=0)[halo:halo + tm])
    hm = h[halo:halo + tm]
    bg = _dot(hm, wi_ref[:, 0 * w:1 * w])
    gg = _dot(hm, wi_ref[:, 3 * w:4 * w])
    z = (bg * conv * (gg * _sigmoid(gg))).astype(BF16)
    o_ref[...] = x1[halo:halo + tm] + gate1_ref[0] * _dot(z, wo1_ref[...])


def _tail(x2, fo, ao, gs, gate0, wo0, g1, shift1, scale1, wi, cw, gate1, wo1, n):
    rows, d = x2.shape
    tm = ROW_BLOCK
    bpb = n // tm
    hb = tm // HALO
    nhalo = rows // HALO
    main = lambda w: pl.BlockSpec((tm, w), lambda i: (i, 0))
    prev = lambda w: pl.BlockSpec((HALO, w), lambda i: (jnp.maximum(i * hb - 1, 0), 0))
    nxt = lambda w: pl.BlockSpec((HALO, w), lambda i: (jnp.minimum((i + 1) * hb, nhalo - 1), 0))
    with_halo = lambda w: [prev(w), main(w), nxt(w)]
    per_batch = pl.BlockSpec((1, 1, d), lambda i: (i // bpb, 0, 0))
    kern = functools.partial(_tail_kernel, bpb=bpb)
    return pl.pallas_call(
        kern,
        grid=(rows // tm,),
        in_specs=(with_halo(d) + with_halo(F_WIDTH) + with_halo(ATT_WIDTH) + with_halo(d)
                  + [per_batch, _const_spec(wo0.shape), _const_spec((1, d)), per_batch, per_batch,
                     _const_spec(wi.shape), _const_spec(cw.shape), per_batch, _const_spec(wo1.shape)]),
        out_specs=main(d),
        out_shape=jax.ShapeDtypeStruct((rows, d), F32),
        compiler_params=_params(("parallel",)),
        name="tail",
    )(x2, x2, x2, fo, fo, fo, ao, ao, ao, gs, gs, gs, gate0, wo0, g1, shift1, scale1, wi, cw, gate1, wo1)


@functools.lru_cache(maxsize=None)
def _rope_tables(n):
    rows = n // GRID_W
    pos_r = np.repeat(np.arange(rows, dtype=np.float64), GRID_W)
    pos_c = np.tile(np.arange(GRID_W, dtype=np.float64), rows)
    half = DH // 2
    freqs = ROPE_BASE ** (-np.arange(0, half, 2, dtype=np.float64) / half)
    ang_r = pos_r[:, None] * freqs
    ang_c = pos_c[:, None] * freqs
    cos64 = np.concatenate([np.cos(ang_r), np.cos(ang_r), np.cos(ang_c), np.cos(ang_c)], axis=-1)
    sin64 = np.concatenate([-np.sin(ang_r), np.sin(ang_r), -np.sin(ang_c), np.sin(ang_c)], axis=-1)
    cos = np.concatenate([cos64, cos64], axis=-1).astype(np.float32)
    sin = np.concatenate([sin64, sin64], axis=-1).astype(np.float32)
    return jnp.asarray(cos), jnp.asarray(sin)


@functools.lru_cache(maxsize=None)
def _group_sum_matrix():
    return jnp.asarray(np.kron(np.eye(MXU_DIM // DH), np.ones((DH, DH))), dtype=BF16)


def _lambda_init(layer_idx):
    return 0.8 - 0.6 * math.exp(-0.3 * layer_idx)


def kernel(x, c, ctx, c_ctx, norm_g, ada_w, ada_b, even_w_in, even_q_norm, even_k_norm, even_lambda_q1,
           even_lambda_k1, even_lambda_q2, even_lambda_k2, even_subln, even_w_out, odd_w_in, odd_conv_w,
           odd_w_out):
    batch, n, d = x.shape
    m_ctx = ctx.shape[1]
    assert d == D_MODEL and n % (FFT_N1 * FFT_KB) == 0 and n % ROW_BLOCK == 0 and n % GRID_W == 0
    assert batch < 8 and norm_g.shape[0] == 2

    cvec = jnp.zeros((8, d), F32).at[:batch].set(c).at[batch].set(c_ctx)
    mods = _mod_params(cvec, ada_w, ada_b)
    shift0, scale0, gate0 = [mods[0, :, j * d:(j + 1) * d].reshape(8, 1, d) for j in range(3)]
    shift1, scale1, gate1 = [mods[1, :, j * d:(j + 1) * d].reshape(8, 1, d) for j in range(3)]

    w_in = even_w_in[0].astype(BF16)
    wf = w_in[:, 0:F_WIDTH]
    wq = w_in[:, F_WIDTH:F_WIDTH + QK_WIDTH]
    wk = w_in[:, F_WIDTH + QK_WIDTH:F_WIDTH + 2 * QK_WIDTH]
    wv = w_in[:, F_WIDTH + 2 * QK_WIDTH:F_WIDTH + 2 * QK_WIDTH + ATT_WIDTH]
    wg = w_in[:, F_WIDTH + 2 * QK_WIDTH + ATT_WIDTH:]
    reps = QK_WIDTH // DH
    qg = (jnp.tile(even_q_norm[0].astype(F32), reps) * (ATTN_SCALE * LOG2E)).reshape(1, QK_WIDTH)
    kg = jnp.tile(even_k_norm[0].astype(F32), reps).reshape(1, QK_WIDTH)
    bd = _group_sum_matrix()
    cos, sin = _rope_tables(n)
    g0 = norm_g[0].reshape(1, d)
    g1 = norm_g[1].reshape(1, d)

    x2 = x.reshape(batch * n, d)
    u, q, k, vt, gs = _inproj0(x2, shift0, scale0, g0, cos, sin, qg, kg, bd, wf, wq, wk, wv, wg, batch, n)
    kc, vtc = _ctx_kv(ctx.reshape(batch * m_ctx, d), shift0[batch], scale0[batch], g0, kg, bd, wk, wv,
                      batch, m_ctx)

    lam_p = jnp.stack([even_lambda_q1[0], even_lambda_k1[0], even_lambda_q2[0], even_lambda_k2[0]]).astype(F32)
    sg = even_subln[0].astype(F32).reshape(1, DV)
    ao = _diff_attention(lam_p, q.reshape(batch, n, QK_WIDTH), k.reshape(batch, n, QK_WIDTH),
                         kc.reshape(batch, m_ctx, QK_WIDTH), vt, vtc, sg, _lambda_init(0))

    fo = _fourier_mix(u, batch, n)

    out = _tail(x2, fo, ao.reshape(batch * n, ATT_WIDTH), gs, gate0, even_w_out[0].astype(BF16), g1, shift1,
                scale1, odd_w_in[0].astype(BF16), odd_conv_w[0].astype(F32), gate1, odd_w_out[0].astype(BF16), n)
    return out.reshape(batch, n, d)
```

```python
import functools
import math

import numpy as np
import jax
import jax.numpy as jnp
from jax import lax
from jax.experimental import pallas as pl
from jax.experimental.pallas import tpu as pltpu

F32 = jnp.float32
BF16 = jnp.bfloat16

D_MODEL = 1024
GRID_W = 64
F_GROUPS = 4
F_CH = 128
F_WIDTH = F_GROUPS * F_CH
H_DIFF = 4
DH = 64
DV = 128
QK_WIDTH = H_DIFF * 2 * DH
ATT_WIDTH = H_DIFF * DV
ROPE_BASE = 10000.0
EPS = 1e-6
ATTN_SCALE = DH ** -0.5
LOG2E = 1.4426950408889634

LANES = 128
MXU_DIM = 256
VMEM_LIMIT = 56 * 1024 * 1024

ROW_BLOCK = 1024
HALO = 16
FFT_N1 = 128
FFT_KB = 8
Q_BLOCK_ATT = 1024
KEY_CHUNK_MAX = 1024
PV_GROUP = 4
P_BUFFERS = 6
NEG_BIG = -1e30
SAFE_SUM_LO = 2.0 ** -60
SAFE_SUM_HI = 2.0 ** 100


def _dot(a, b):
    return jnp.dot(a, b, preferred_element_type=F32)


def _dot_nt(a, b):
    return lax.dot_general(a, b, (((1,), (1,)), ((), ())), preferred_element_type=F32)


def _sigmoid(x):
    return 1.0 / (1.0 + jnp.exp(-x))


def _params(sem):
    return pltpu.CompilerParams(dimension_semantics=sem, vmem_limit_bytes=VMEM_LIMIT)


def _mod_kernel(c_ref, w_ref, b_ref, o_ref):
    cv = c_ref[...]
    a = cv * _sigmoid(cv)
    w = w_ref[0]
    a_hi = a.astype(BF16)
    a_lo = (a - a_hi.astype(F32)).astype(BF16)
    w_hi = w.astype(BF16)
    w_lo = (w - w_hi.astype(F32)).astype(BF16)
    acc = _dot(a_hi, w_hi) + _dot(a_hi, w_lo) + _dot(a_lo, w_hi)
    o_ref[0] = acc + b_ref[0]


def _mod_params(cvec, ada_w, ada_b):
    depth, d, d3 = ada_w.shape
    return pl.pallas_call(
        _mod_kernel,
        grid=(depth,),
        in_specs=[
            pl.BlockSpec((8, d), lambda l: (0, 0)),
            pl.BlockSpec((1, d, d3), lambda l: (l, 0, 0)),
            pl.BlockSpec((1, 1, d3), lambda l: (l, 0, 0)),
        ],
        out_specs=pl.BlockSpec((1, 8, d3), lambda l: (l, 0, 0)),
        out_shape=jax.ShapeDtypeStruct((depth, 8, d3), F32),
        compiler_params=_params(("parallel",)),
        name="mod_params",
    )(cvec, ada_w, ada_b.reshape(depth, 1, d3))


def _modulated(x, g, shift, scale):
    ms = jnp.mean(x * x, axis=-1, keepdims=True)
    y = x * lax.rsqrt(ms + EPS) * g
    return y * (1.0 + scale) + shift


def _group_norm(t, gain, bd):
    sq = (t * t).astype(BF16)
    half = MXU_DIM
    ss = jnp.concatenate([_dot(sq[:, :half], bd), _dot(sq[:, half:], bd)], axis=-1)
    return t * lax.rsqrt(ss * (1.0 / DH) + EPS) * gain


def _rope(t, cos, sin):
    width = t.shape[-1]
    quarter = DH // 4
    lane = lax.broadcasted_iota(jnp.int32, t.shape, 1)
    from_above = pltpu.roll(t, width - quarter, axis=1)
    from_below = pltpu.roll(t, quarter, axis=1)
    partner = jnp.where((lane & (2 * quarter - 1)) < quarter, from_above, from_below)
    reps = width // cos.shape[-1]
    cosf = jnp.concatenate([cos] * reps, axis=-1)
    sinf = jnp.concatenate([sin] * reps, axis=-1)
    return t * cosf + partner * sinf


def _inproj_kernel(x_ref, shift_ref, scale_ref, g_ref, cos_ref, sin_ref, qg_ref, kg_ref, bd_ref,
                   wf_ref, wq_ref, wk_ref, wv_ref, wg_ref,
                   u_ref, q_ref, k_ref, vt_ref, gs_ref, us_ref):
    h = _modulated(x_ref[...], g_ref[...], shift_ref[0], scale_ref[0]).astype(BF16)
    cos = cos_ref[...]
    sin = sin_ref[...]
    bd = bd_ref[...]
    _, n2, n1_blk, _ = u_ref.shape
    pitch = us_ref.shape[1] // n1_blk
    uf = _dot(h, wf_ref[...])
    for a in range(n1_blk):
        for s in range(F_WIDTH // LANES):
            us_ref[s, a * pitch:a * pitch + n2, :] = uf[a * n2:(a + 1) * n2, s * LANES:(s + 1) * LANES]
    for b in range(n2):
        for s in range(F_WIDTH // LANES):
            col = us_ref[s, pl.ds(b, n1_blk, stride=pitch), :]
            u_ref[0, b, :, s * LANES:(s + 1) * LANES] = col.astype(BF16)
    q = _rope(_group_norm(_dot(h, wq_ref[...]), qg_ref[...], bd), cos, sin)
    q_ref[...] = q.astype(BF16)
    k = _rope(_group_norm(_dot(h, wk_ref[...]), kg_ref[...], bd), cos, sin)
    k_ref[...] = k.astype(BF16)
    v = _dot(h, wv_ref[...])
    for hh in range(H_DIFF):
        vt_ref[0, hh] = v[:, hh * DV:(hh + 1) * DV].T.astype(BF16)
    g = _dot(h, wg_ref[...])
    gs_ref[...] = (g * _sigmoid(g)).astype(BF16)


def _ctx_kv_kernel(x_ref, shift_ref, scale_ref, g_ref, kg_ref, bd_ref, wk_ref, wv_ref, k_ref, vt_ref):
    h = _modulated(x_ref[...], g_ref[...], shift_ref[...], scale_ref[...]).astype(BF16)
    k = _group_norm(_dot(h, wk_ref[...]), kg_ref[...], bd_ref[...])
    k_ref[...] = k.astype(BF16)
    v = _dot(h, wv_ref[...])
    for hh in range(H_DIFF):
        vt_ref[0, hh] = v[:, hh * DV:(hh + 1) * DV].T.astype(BF16)


def _const_spec(shape):
    zeros = (0,) * len(shape)
    return pl.BlockSpec(shape, lambda *_: zeros)


def _inproj0(x2, shift, scale, g, cos, sin, qg, kg, bd, wf, wq, wk, wv, wg, batch, n):
    rows, d = x2.shape
    tm = ROW_BLOCK
    bpb = n // tm
    n2 = n // FFT_N1
    n1_blk = tm // n2
    assert tm % n2 == 0 and n2 % 8 == 0 and n1_blk % 16 == 0
    row_spec = lambda w: pl.BlockSpec((tm, w), lambda i: (i, 0))
    return pl.pallas_call(
        _inproj_kernel,
        grid=(rows // tm,),
        in_specs=[
            row_spec(d),
            pl.BlockSpec((1, 1, d), lambda i: (i // bpb, 0, 0)),
            pl.BlockSpec((1, 1, d), lambda i: (i // bpb, 0, 0)),
            _const_spec((1, d)),
            pl.BlockSpec((tm, LANES), lambda i: (i % bpb, 0)),
            pl.BlockSpec((tm, LANES), lambda i: (i % bpb, 0)),
            _const_spec((1, QK_WIDTH)),
            _const_spec((1, QK_WIDTH)),
            _const_spec((MXU_DIM, MXU_DIM)),
            _const_spec(wf.shape), _const_spec(wq.shape), _const_spec(wk.shape),
            _const_spec(wv.shape), _const_spec(wg.shape),
        ],
        out_specs=[
            pl.BlockSpec((1, n2, n1_blk, F_WIDTH), lambda i: (i // bpb, 0, i % bpb, 0)),
            row_spec(QK_WIDTH), row_spec(QK_WIDTH),
            pl.BlockSpec((1, H_DIFF, DV, tm), lambda i: (i // bpb, 0, 0, i % bpb)),
            row_spec(d),
        ],
        out_shape=[
            jax.ShapeDtypeStruct((batch, n2, FFT_N1, F_WIDTH), BF16),
            jax.ShapeDtypeStruct((rows, QK_WIDTH), BF16),
            jax.ShapeDtypeStruct((rows, QK_WIDTH), BF16),
            jax.ShapeDtypeStruct((batch, H_DIFF, DV, n), BF16),
            jax.ShapeDtypeStruct((rows, d), BF16),
        ],
        scratch_shapes=[pltpu.VMEM((F_WIDTH // LANES, n1_blk * (n2 + 4), LANES), F32)],
        compiler_params=_params(("parallel",)),
        name="inproj0",
    )(x2, shift, scale, g, cos, sin, qg, kg, bd, wf, wq, wk, wv, wg)


def _ctx_kv(ctx2, shift, scale, g, kg, bd, wk, wv, batch, m_ctx):
    rows, d = ctx2.shape
    return pl.pallas_call(
        _ctx_kv_kernel,
        grid=(batch,),
        in_specs=[
            pl.BlockSpec((m_ctx, d), lambda b: (b, 0)),
            _const_spec((1, d)), _const_spec((1, d)), _const_spec((1, d)),
            _const_spec((1, QK_WIDTH)), _const_spec((MXU_DIM, MXU_DIM)),
            _const_spec(wk.shape), _const_spec(wv.shape),
        ],
        out_specs=[
            pl.BlockSpec((m_ctx, QK_WIDTH), lambda b: (b, 0)),
            pl.BlockSpec((1, H_DIFF, DV, m_ctx), lambda b: (b, 0, 0, 0)),
        ],
        out_shape=[
            jax.ShapeDtypeStruct((rows, QK_WIDTH), BF16),
            jax.ShapeDtypeStruct((batch, H_DIFF, DV, m_ctx), BF16),
        ],
        compiler_params=_params(("parallel",)),
        name="ctx_kv",
    )(ctx2, shift, scale, g, kg, bd, wk, wv)


def _attn_kernel(lam_ref, q_ref, k_ref, kc_ref, vt_ref, vtc_ref, sg_ref, o_ref, acc1_ref, acc2_ref, *p_bufs,
                 tk, lam_init):
    tq = q_ref.shape[1]
    n_lat = k_ref.shape[1] // tk
    m_ctx = kc_ref.shape[1]
    q = q_ref[0]
    lane = lax.broadcasted_iota(jnp.int32, q.shape, 1)
    zero = jnp.zeros_like(q)
    qz = (jnp.where(lane < DH, q, zero), jnp.where(lane >= DH, q, zero))
    acc_refs = (acc1_ref, acc2_ref)
    p_refs = (p_bufs[:P_BUFFERS], p_bufs[P_BUFFERS:])
    sizes = [tk] * (n_lat - 1) + [tk + m_ctx]
    groups = [tuple(range(g, min(g + PV_GROUP, n_lat))) for g in range(0, n_lat, PV_GROUP)]

    def key_rows(c):
        rows = k_ref[0, c * tk:(c + 1) * tk, :]
        return jnp.concatenate([rows, kc_ref[0]], axis=0) if c == n_lat - 1 else rows

    def value_cols(c):
        cols = vt_ref[0, 0, :, c * tk:(c + 1) * tk]
        return jnp.concatenate([cols, vtc_ref[0, 0]], axis=1) if c == n_lat - 1 else cols

    def finish(l1, l2):
        lp = lam_ref[...]
        lam = (jnp.exp(jnp.sum(lp[0:1] * lp[1:2], axis=-1, keepdims=True))
               - jnp.exp(jnp.sum(lp[2:3] * lp[3:4], axis=-1, keepdims=True)) + lam_init)
        ot = acc1_ref[...] / l1 - lam * (acc2_ref[...] / l2)
        o = ot.T
        ms = jnp.mean(o * o, axis=-1, keepdims=True)
        o = o * lax.rsqrt(ms + EPS) * sg_ref[...] * (1.0 - lam_init)
        o_ref[0] = o.astype(BF16)

    def scores(kr, k0, size, mp):
        return _dot_nt(kr[0, pl.ds(k0, size), :], qz[mp])

    def exp_scores(c):
        kc = key_rows(c)
        sums = []
        for mp in range(2):
            p = jnp.exp2(_dot_nt(kc, qz[mp]))
            p_refs[mp][c % P_BUFFERS][0:sizes[c], :] = p.astype(BF16)
            sums.append(jnp.sum(p, axis=0, keepdims=True))
        return sums

    def pv(gi):
        members = groups[gi]
        vc = jnp.concatenate([value_cols(c) for c in members], axis=1)
        for mp in range(2):
            pp = jnp.concatenate([p_refs[mp][c % P_BUFFERS][0:sizes[c], :] for c in members], axis=0)
            upd = _dot(vc, pp)
            acc_refs[mp][...] = upd if gi == 0 else acc_refs[mp][...] + upd

    l = None
    issued = 0
    for c in range(n_lat):
        sums = exp_scores(c)
        l = sums if l is None else [l[mp] + sums[mp] for mp in range(2)]
        while issued < len(groups) and groups[issued][-1] < c:
            pv(issued)
            issued += 1
    for gi in range(issued, len(groups)):
        pv(gi)
    finish(l[0], l[1])

    in_range = jnp.logical_and(jnp.minimum(l[0], l[1]) > SAFE_SUM_LO, jnp.maximum(l[0], l[1]) < SAFE_SUM_HI)
    all_safe = jnp.min(jnp.where(in_range, 1.0, 0.0)) > 0.5

    @pl.when(jnp.logical_not(all_safe))
    def _():
        def online_step(kr, vc, k0, size, carry):
            out = []
            for mp in range(2):
                m_old, l_old = carry[mp]
                s = scores(kr, k0, size, mp)
                m_new = jnp.maximum(m_old, jnp.max(s, axis=0, keepdims=True))
                alpha = jnp.exp2(m_old - m_new)
                p = jnp.exp2(s - m_new)
                acc_refs[mp][...] = alpha * acc_refs[mp][...] + _dot(vc, p.astype(BF16))
                out.append((m_new, alpha * l_old + jnp.sum(p, axis=0, keepdims=True)))
            return tuple(out)

        def body(c, carry):
            k0 = pl.multiple_of(c * tk, tk)
            return online_step(k_ref, vt_ref[0, 0, :, pl.ds(k0, tk)], k0, tk, carry)

        acc1_ref[...] = jnp.zeros_like(acc1_ref)
        acc2_ref[...] = jnp.zeros_like(acc2_ref)
        start = (jnp.full((1, tq), NEG_BIG, F32), jnp.zeros((1, tq), F32))
        carry = lax.fori_loop(0, n_lat, body, (start, start))
        carry = online_step(kc_ref, vtc_ref[0, 0], 0, m_ctx, carry)
        finish(carry[0][1], carry[1][1])


def _pick_key_chunk(n_keys):
    best = LANES
    for cand in range(LANES, KEY_CHUNK_MAX + 1, LANES):
        if n_keys % cand == 0:
            best = cand
    return best


def _diff_attention(lam_p, q, k, kc, vt, vtc, sg, lam_init):
    batch, n, _ = q.shape
    m_ctx = kc.shape[1]
    tq = Q_BLOCK_ATT
    tk = _pick_key_chunk(n)
    hw = 2 * DH
    kern = functools.partial(_attn_kernel, tk=tk, lam_init=lam_init)
    return pl.pallas_call(
        kern,
        grid=(batch, H_DIFF, n // tq),
        in_specs=[
            _const_spec((4, DH)),
            pl.BlockSpec((1, tq, hw), lambda b, h, i: (b, i, h)),
            pl.BlockSpec((1, n, hw), lambda b, h, i: (b, 0, h)),
            pl.BlockSpec((1, m_ctx, hw), lambda b, h, i: (b, 0, h)),
            pl.BlockSpec((1, 1, DV, n), lambda b, h, i: (b, h, 0, 0)),
            pl.BlockSpec((1, 1, DV, m_ctx), lambda b, h, i: (b, h, 0, 0)),
            _const_spec((1, DV)),
        ],
        out_specs=pl.BlockSpec((1, tq, DV), lambda b, h, i: (b, i, h)),
        out_shape=jax.ShapeDtypeStruct((batch, n, ATT_WIDTH), BF16),
        scratch_shapes=([pltpu.VMEM((DV, tq), F32)] * 2
                        + [pltpu.VMEM((tk + m_ctx, tq), BF16)] * (2 * P_BUFFERS)),
        compiler_params=_params(("parallel", "parallel", "arbitrary")),
        name="diff_attn",
    )(lam_p, q, k, kc, vt, vtc, sg)


def _fft_rows_kernel(f_ref, u_ref, y_ref):
    for j in range(y_ref.shape[1]):
        y_ref[0, j] = _dot(f_ref[...], u_ref[0, j]).astype(BF16)


def _fft_rows(f1cs, u4):
    batch, n2, n1, width = u4.shape
    per_step = min(n2, 16)
    return pl.pallas_call(
        _fft_rows_kernel,
        grid=(batch, n2 // per_step),
        in_specs=[
            _const_spec(f1cs.shape),
            pl.BlockSpec((1, per_step, n1, width), lambda b, j: (b, j, 0, 0)),
        ],
        out_specs=pl.BlockSpec((1, per_step, 2 * n1, width), lambda b, j: (b, j, 0, 0)),
        out_shape=jax.ShapeDtypeStruct((batch, n2, 2 * n1, width), BF16),
        compiler_params=_params(("parallel", "parallel")),
        name="fft_rows",
    )(f1cs, u4)


def _fft_cols_kernel(g_ref, y_ref, bdc_ref, bds_ref, o_ref):
    _, n2, kb2, width = y_ref.shape
    kb = kb2 // 2
    rows = kb * n2
    y = y_ref[0].reshape(2 * rows, width)
    z = _dot(g_ref[0], y)
    xr = z[:rows].astype(BF16)
    xi = z[rows:].astype(BF16)
    bdc = bdc_ref[...]
    bds = bds_ref[...]
    half = MXU_DIM
    o = jnp.concatenate(
        [_dot(xr[:, :half], bdc) + _dot(xi[:, :half], bds),
         _dot(xr[:, half:], bdc) + _dot(xi[:, half:], bds)], axis=-1)
    o_ref[0] = o.reshape(n2, kb, width)


def _fft_cols(gbig, y4, bdc, bds):
    batch, n2, rows2, width = y4.shape
    n1 = rows2 // 2
    kb = FFT_KB
    gr = gbig.shape[1]
    return pl.pallas_call(
        _fft_cols_kernel,
        grid=(n1 // kb, batch),
        in_specs=[
            pl.BlockSpec((1, gr, gr), lambda j, b: (j, 0, 0)),
            pl.BlockSpec((1, n2, 2 * kb, width), lambda j, b: (b, 0, j, 0)),
            _const_spec(bdc.shape), _const_spec(bds.shape),
        ],
        out_specs=pl.BlockSpec((1, n2, kb, width), lambda j, b: (b, 0, j, 0)),
        out_shape=jax.ShapeDtypeStruct((batch, n2, n1, width), F32),
        compiler_params=_params(("parallel", "arbitrary")),
        name="fft_cols",
    )(gbig, y4, bdc, bds)


@functools.lru_cache(maxsize=None)
def _fft_tables(n):
    n1 = FFT_N1
    n2 = n // n1
    kb = FFT_KB
    k1 = np.arange(n1)
    ang = 2.0 * np.pi * np.outer(k1, k1) / n1
    f1cs = np.stack([np.cos(ang), -np.sin(ang)], axis=1).reshape(2 * n1, n1)
    scale = 1.0 / math.sqrt(n * F_CH)
    nblk = n1 // kb
    rows = n2 * kb
    gbig = np.zeros((nblk, 2, n2, kb, n2, kb, 2), np.float32)
    n2i = np.arange(n2)
    for j in range(kb):
        kk = (np.arange(nblk)[:, None] * kb + j) + n1 * n2i[None, :]
        a = 2.0 * np.pi * (kk[:, :, None] * n2i[None, None, :] % n) / n
        gr = np.cos(a) * scale
        gi = -np.sin(a) * scale
        gbig[:, 0, :, j, :, j, 0] = gr
        gbig[:, 0, :, j, :, j, 1] = -gi
        gbig[:, 1, :, j, :, j, 0] = gi
        gbig[:, 1, :, j, :, j, 1] = gr
    gbig = gbig.reshape(nblk, 2 * rows, 2 * rows)
    c = np.arange(F_CH)
    angc = 2.0 * np.pi * np.outer(c, c) / F_CH
    eye2 = np.eye(MXU_DIM // F_CH)
    bdc = np.kron(eye2, np.cos(angc))
    bds = np.kron(eye2, np.sin(angc))
    to_bf16 = lambda a: jnp.asarray(a, dtype=F32).astype(BF16)
    return to_bf16(f1cs), to_bf16(gbig), to_bf16(bdc), to_bf16(bds)


def _fourier_mix(u4, batch, n):
    f1cs, gbig, bdc, bds = _fft_tables(n)
    y = _fft_rows(f1cs, u4)
    fo = _fft_cols(gbig, y, bdc, bds)
    return fo.reshape(batch * n, F_WIDTH)


def _tail_kernel(xp_ref, x_ref, xn_ref, fp_ref, f_ref, fn_ref, ap_ref, a_ref, an_ref, gp_ref, g_ref, gn_ref,
                 gate0_ref, wo0_ref, g1_ref, shift_ref, scale_ref, wi_ref, cw_ref, gate1_ref, wo1_ref,
                 o_ref, *, bpb):
    i = pl.program_id(0)
    tm = x_ref.shape[0]
    halo = xp_ref.shape[0]
    ext = tm + 2 * halo
    cat = lambda p, m, nx: jnp.concatenate([p[...], m[...], nx[...]], axis=0)
    x = cat(xp_ref, x_ref, xn_ref)
    gs = cat(gp_ref, g_ref, gn_ref).astype(F32)
    yf = (cat(fp_ref, f_ref, fn_ref) * gs[:, :F_WIDTH]).astype(BF16)
    ya = (cat(ap_ref, a_ref, an_ref).astype(F32) * gs[:, F_WIDTH:]).astype(BF16)
    y = _dot(yf, wo0_ref[:F_WIDTH, :]) + _dot(ya, wo0_ref[F_WIDTH:, :])
    x1 = x + gate0_ref[0] * y
    h = _modulated(x1, g1_ref[...], shift_ref[0], scale_ref[0]).astype(BF16)
    w = x.shape[-1]
    u = _dot(h, wi_ref[:, 1 * w:2 * w]) * _dot(h, wi_ref[:, 2 * w:3 * w])
    row = lax.broadcasted_iota(jnp.int32, u.shape, 0)
    lo = jnp.where((i % bpb) == 0, halo, 0)
    hi = jnp.where((i % bpb) == bpb - 1, halo + tm, ext)
    u = jnp.where(jnp.logical_and(row >= lo, row < hi), u, 0.0)
    cw = cw_ref[...]
    conv = (cw[0:1] * pltpu.roll(u, 1, axis=0)[halo:halo + tm]
            + cw[1:2] * u[halo:halo + tm]
            + cw[2:3] * pltpu.roll(u, ext - 1, axis=0)[halo:halo + tm])
    hm = h[halo:halo + tm]
    bg = _dot(hm, wi_ref[:, 0 * w:1 * w])
    gg = _dot(hm, wi_ref[:, 3 * w:4 * w])
    z = (bg * conv * (gg * _sigmoid(gg))).astype(BF16)
    o_ref[...] = x1[halo:halo + tm] + gate1_ref[0] * _dot(z, wo1_ref[...])


def _tail(x2, fo, ao, gs, gate0, wo0, g1, shift1, scale1, wi, cw, gate1, wo1, n):
    rows, d = x2.shape
    tm = ROW_BLOCK
    bpb = n // tm
    hb = tm // HALO
    nhalo = rows // HALO
    main = lambda w: pl.BlockSpec((tm, w), lambda i: (i, 0))
    prev = lambda w: pl.BlockSpec((HALO, w), lambda i: (jnp.maximum(i * hb - 1, 0), 0))
    nxt = lambda w: pl.BlockSpec((HALO, w), lambda i: (jnp.minimum((i + 1) * hb, nhalo - 1), 0))
    with_halo = lambda w: [prev(w), main(w), nxt(w)]
    per_batch = pl.BlockSpec((1, 1, d), lambda i: (i // bpb, 0, 0))
    kern = functools.partial(_tail_kernel, bpb=bpb)
    return pl.pallas_call(
        kern,
        grid=(rows // tm,),
        in_specs=(with_halo(d) + with_halo(F_WIDTH) + with_halo(ATT_WIDTH) + with_halo(d)
                  + [per_batch, _const_spec(wo0.shape), _const_spec((1, d)), per_batch, per_batch,
                     _const_spec(wi.shape), _const_spec(cw.shape), per_batch, _const_spec(wo1.shape)]),
        out_specs=main(d),
        out_shape=jax.ShapeDtypeStruct((rows, d), F32),
        compiler_params=_params(("parallel",)),
        name="tail",
    )(x2, x2, x2, fo, fo, fo, ao, ao, ao, gs, gs, gs, gate0, wo0, g1, shift1, scale1, wi, cw, gate1, wo1)


@functools.lru_cache(maxsize=None)
def _rope_tables(n):
    rows = n // GRID_W
    pos_r = np.repeat(np.arange(rows, dtype=np.float64), GRID_W)
    pos_c = np.tile(np.arange(GRID_W, dtype=np.float64), rows)
    half = DH // 2
    freqs = ROPE_BASE ** (-np.arange(0, half, 2, dtype=np.float64) / half)
    ang_r = pos_r[:, None] * freqs
    ang_c = pos_c[:, None] * freqs
    cos64 = np.concatenate([np.cos(ang_r), np.cos(ang_r), np.cos(ang_c), np.cos(ang_c)], axis=-1)
    sin64 = np.concatenate([-np.sin(ang_r), np.sin(ang_r), -np.sin(ang_c), np.sin(ang_c)], axis=-1)
    cos = np.concatenate([cos64, cos64], axis=-1).astype(np.float32)
    sin = np.concatenate([sin64, sin64], axis=-1).astype(np.float32)
    return jnp.asarray(cos), jnp.asarray(sin)


@functools.lru_cache(maxsize=None)
def _group_sum_matrix():
    return jnp.asarray(np.kron(np.eye(MXU_DIM // DH), np.ones((DH, DH))), dtype=BF16)


def _lambda_init(layer_idx):
    return 0.8 - 0.6 * math.exp(-0.3 * layer_idx)


def kernel(x, c, ctx, c_ctx, norm_g, ada_w, ada_b, even_w_in, even_q_norm, even_k_norm, even_lambda_q1,
           even_lambda_k1, even_lambda_q2, even_lambda_k2, even_subln, even_w_out, odd_w_in, odd_conv_w,
           odd_w_out):
    batch, n, d = x.shape
    m_ctx = ctx.shape[1]
    assert d == D_MODEL and n % (FFT_N1 * FFT_KB) == 0 and n % ROW_BLOCK == 0 and n % GRID_W == 0
    assert batch < 8 and norm_g.shape[0] == 2

    cvec = jnp.zeros((8, d), F32).at[:batch].set(c).at[batch].set(c_ctx)
    mods = _mod_params(cvec, ada_w, ada_b)
    shift0, scale0, gate0 = [mods[0, :, j * d:(j + 1) * d].reshape(8, 1, d) for j in range(3)]
    shift1, scale1, gate1 = [mods[1, :, j * d:(j + 1) * d].reshape(8, 1, d) for j in range(3)]

    w_in = even_w_in[0].astype(BF16)
    wf = w_in[:, 0:F_WIDTH]
    wq = w_in[:, F_WIDTH:F_WIDTH + QK_WIDTH]
    wk = w_in[:, F_WIDTH + QK_WIDTH:F_WIDTH + 2 * QK_WIDTH]
    wv = w_in[:, F_WIDTH + 2 * QK_WIDTH:F_WIDTH + 2 * QK_WIDTH + ATT_WIDTH]
    wg = w_in[:, F_WIDTH + 2 * QK_WIDTH + ATT_WIDTH:]
    reps = QK_WIDTH // DH
    qg = (jnp.tile(even_q_norm[0].astype(F32), reps) * (ATTN_SCALE * LOG2E)).reshape(1, QK_WIDTH)
    kg = jnp.tile(even_k_norm[0].astype(F32), reps).reshape(1, QK_WIDTH)
    bd = _group_sum_matrix()
    cos, sin = _rope_tables(n)
    g0 = norm_g[0].reshape(1, d)
    g1 = norm_g[1].reshape(1, d)

    x2 = x.reshape(batch * n, d)
    u, q, k, vt, gs = _inproj0(x2, shift0, scale0, g0, cos, sin, qg, kg, bd, wf, wq, wk, wv, wg, batch, n)
    kc, vtc = _ctx_kv(ctx.reshape(batch * m_ctx, d), shift0[batch], scale0[batch], g0, kg, bd, wk, wv,
                      batch, m_ctx)

    lam_p = jnp.stack([even_lambda_q1[0], even_lambda_k1[0], even_lambda_q2[0], even_lambda_k2[0]]).astype(F32)
    sg = even_subln[0].astype(F32).reshape(1, DV)
    ao = _diff_attention(lam_p, q.reshape(batch, n, QK_WIDTH), k.reshape(batch, n, QK_WIDTH),
                         kc.reshape(batch, m_ctx, QK_WIDTH), vt, vtc, sg, _lambda_init(0))

    fo = _fourier_mix(u, batch, n)

    out = _tail(x2, fo, ao.reshape(batch * n, ATT_WIDTH), gs, gate0, even_w_out[0].astype(BF16), g1, shift1,
                scale1, odd_w_in[0].astype(BF16), odd_conv_w[0].astype(F32), gate1, odd_w_out[0].astype(BF16), n)
    return out.reshape(batch, n, d)
```

```python
import functools
import math

import numpy as np
import jax
import jax.numpy as jnp
from jax import lax
from jax.experimental import pallas as pl
from jax.experimental.pallas import tpu as pltpu

F32 = jnp.float32
BF16 = jnp.bfloat16

D_MODEL = 1024
GRID_W = 64
F_GROUPS = 4
F_CH = 128
F_WIDTH = F_GROUPS * F_CH
H_DIFF = 4
DH = 64
DV = 128
QK_WIDTH = H_DIFF * 2 * DH
ATT_WIDTH = H_DIFF * DV
ROPE_BASE = 10000.0
EPS = 1e-6
ATTN_SCALE = DH ** -0.5
LOG2E = 1.4426950408889634

LANES = 128
MXU_DIM = 256
VMEM_LIMIT = 56 * 1024 * 1024

ROW_BLOCK = 1024
HALO = 16
FFT_N1 = 128
FFT_KB = 8
Q_BLOCK_ATT = 1024
KEY_CHUNK_MAX = 1024
PV_GROUP = 4
P_BUFFERS = 6
NEG_BIG = -1e30
SAFE_SUM_LO = 2.0 ** -60
SAFE_SUM_HI = 2.0 ** 100


def _dot(a, b):
    return jnp.dot(a, b, preferred_element_type=F32)


def _dot_nt(a, b):
    return lax.dot_general(a, b, (((1,), (1,)), ((), ())), preferred_element_type=F32)


def _sigmoid(x):
    return 1.0 / (1.0 + jnp.exp(-x))


def _params(sem):
    return pltpu.CompilerParams(dimension_semantics=sem, vmem_limit_bytes=VMEM_LIMIT)


def _mod_kernel(c_ref, w_ref, b_ref, o_ref):
    cv = c_ref[...]
    a = cv * _sigmoid(cv)
    w = w_ref[0]
    a_hi = a.astype(BF16)
    a_lo = (a - a_hi.astype(F32)).astype(BF16)
    w_hi = w.astype(BF16)
    w_lo = (w - w_hi.astype(F32)).astype(BF16)
    acc = _dot(a_hi, w_hi) + _dot(a_hi, w_lo) + _dot(a_lo, w_hi)
    o_ref[0] = acc + b_ref[0]


def _mod_params(cvec, ada_w, ada_b):
    depth, d, d3 = ada_w.shape
    return pl.pallas_call(
        _mod_kernel,
        grid=(depth,),
        in_specs=[
            pl.BlockSpec((8, d), lambda l: (0, 0)),
            pl.BlockSpec((1, d, d3), lambda l: (l, 0, 0)),
            pl.BlockSpec((1, 1, d3), lambda l: (l, 0, 0)),
        ],
        out_specs=pl.BlockSpec((1, 8, d3), lambda l: (l, 0, 0)),
        out_shape=jax.ShapeDtypeStruct((depth, 8, d3), F32),
        compiler_params=_params(("parallel",)),
        name="mod_params",
    )(cvec, ada_w, ada_b.reshape(depth, 1, d3))


def _modulated(x, g, shift, scale):
    ms = jnp.mean(x * x, axis=-1, keepdims=True)
    y = x * lax.rsqrt(ms + EPS) * g
    return y * (1.0 + scale) + shift


def _group_norm(t, gain, bd):
    sq = (t * t).astype(BF16)
    half = MXU_DIM
    ss = jnp.concatenate([_dot(sq[:, :half], bd), _dot(sq[:, half:], bd)], axis=-1)
    return t * lax.rsqrt(ss * (1.0 / DH) + EPS) * gain


def _rope(t, cos, sin):
    width = t.shape[-1]
    quarter = DH // 4
    lane = lax.broadcasted_iota(jnp.int32, t.shape, 1)
    from_above = pltpu.roll(t, width - quarter, axis=1)
    from_below = pltpu.roll(t, quarter, axis=1)
    partner = jnp.where((lane & (2 * quarter - 1)) < quarter, from_above, from_below)
    reps = width // cos.shape[-1]
    cosf = jnp.concatenate([cos] * reps, axis=-1)
    sinf = jnp.concatenate([sin] * reps, axis=-1)
    return t * cosf + partner * sinf


def _inproj_kernel(x_ref, shift_ref, scale_ref, g_ref, cos_ref, sin_ref, qg_ref, kg_ref, bd_ref,
                   wf_ref, wq_ref, wk_ref, wv_ref, wg_ref,
                   u_ref, q_ref, k_ref, vt_ref, gs_ref, us_ref):
    h = _modulated(x_ref[...], g_ref[...], shift_ref[0], scale_ref[0]).astype(BF16)
    cos = cos_ref[...]
    sin = sin_ref[...]
    bd = bd_ref[...]
    _, n2, n1_blk, _ = u_ref.shape
    pitch = us_ref.shape[1] // n1_blk
    uf = _dot(h, wf_ref[...])
    for a in range(n1_blk):
        for s in range(F_WIDTH // LANES):
            us_ref[s, a * pitch:a * pitch + n2, :] = uf[a * n2:(a + 1) * n2, s * LANES:(s + 1) * LANES]
    for b in range(n2):
        for s in range(F_WIDTH // LANES):
            col = us_ref[s, pl.ds(b, n1_blk, stride=pitch), :]
            u_ref[0, b, :, s * LANES:(s + 1) * LANES] = col.astype(BF16)
    q = _rope(_group_norm(_dot(h, wq_ref[...]), qg_ref[...], bd), cos, sin)
    q_ref[...] = q.astype(BF16)
    k = _rope(_group_norm(_dot(h, wk_ref[...]), kg_ref[...], bd), cos, sin)
    k_ref[...] = k.astype(BF16)
    v = _dot(h, wv_ref[...])
    for hh in range(H_DIFF):
        vt_ref[0, hh] = v[:, hh * DV:(hh + 1) * DV].T.astype(BF16)
    g = _dot(h, wg_ref[...])
    gs_ref[...] = (g * _sigmoid(g)).astype(BF16)


def _ctx_kv_kernel(x_ref, shift_ref, scale_ref, g_ref, kg_ref, bd_ref, wk_ref, wv_ref, k_ref, vt_ref):
    h = _modulated(x_ref[...], g_ref[...], shift_ref[...], scale_ref[...]).astype(BF16)
    k = _group_norm(_dot(h, wk_ref[...]), kg_ref[...], bd_ref[...])
    k_ref[...] = k.astype(BF16)
    v = _dot(h, wv_ref[...])
    for hh in range(H_DIFF):
        vt_ref[0, hh] = v[:, hh * DV:(hh + 1) * DV].T.astype(BF16)


def _const_spec(shape):
    zeros = (0,) * len(shape)
    return pl.BlockSpec(shape, lambda *_: zeros)


def _inproj0(x2, shift, scale, g, cos, sin, qg, kg, bd, wf, wq, wk, wv, wg, batch, n):
    rows, d = x2.shape
    tm = ROW_BLOCK
    bpb = n // tm
    n2 = n // FFT_N1
    n1_blk = tm // n2
    assert tm % n2 == 0 and n2 % 8 == 0 and n1_blk % 16 == 0
    row_spec = lambda w: pl.BlockSpec((tm, w), lambda i: (i, 0))
    return pl.pallas_call(
        _inproj_kernel,
        grid=(rows // tm,),
        in_specs=[
            row_spec(d),
            pl.BlockSpec((1, 1, d), lambda i: (i // bpb, 0, 0)),
            pl.BlockSpec((1, 1, d), lambda i: (i // bpb, 0, 0)),
            _const_spec((1, d)),
            pl.BlockSpec((tm, LANES), lambda i: (i % bpb, 0)),
            pl.BlockSpec((tm, LANES), lambda i: (i % bpb, 0)),
            _const_spec((1, QK_WIDTH)),
            _const_spec((1, QK_WIDTH)),
            _const_spec((MXU_DIM, MXU_DIM)),
            _const_spec(wf.shape), _const_spec(wq.shape), _const_spec(wk.shape),
            _const_spec(wv.shape), _const_spec(wg.shape),
        ],
        out_specs=[
            pl.BlockSpec((1, n2, n1_blk, F_WIDTH), lambda i: (i // bpb, 0, i % bpb, 0)),
            row_spec(QK_WIDTH), row_spec(QK_WIDTH),
            pl.BlockSpec((1, H_DIFF, DV, tm), lambda i: (i // bpb, 0, 0, i % bpb)),
            row_spec(d),
        ],
        out_shape=[
            jax.ShapeDtypeStruct((batch, n2, FFT_N1, F_WIDTH), BF16),
            jax.ShapeDtypeStruct((rows, QK_WIDTH), BF16),
            jax.ShapeDtypeStruct((rows, QK_WIDTH), BF16),
            jax.ShapeDtypeStruct((batch, H_DIFF, DV, n), BF16),
            jax.ShapeDtypeStruct((rows, d), BF16),
        ],
        scratch_shapes=[pltpu.VMEM((F_WIDTH // LANES, n1_blk * (n2 + 4), LANES), F32)],
        compiler_params=_params(("parallel",)),
        name="inproj0",
    )(x2, shift, scale, g, cos, sin, qg, kg, bd, wf, wq, wk, wv, wg)


def _ctx_kv(ctx2, shift, scale, g, kg, bd, wk, wv, batch, m_ctx):
    rows, d = ctx2.shape
    return pl.pallas_call(
        _ctx_kv_kernel,
        grid=(batch,),
        in_specs=[
            pl.BlockSpec((m_ctx, d), lambda b: (b, 0)),
            _const_spec((1, d)), _const_spec((1, d)), _const_spec((1, d)),
            _const_spec((1, QK_WIDTH)), _const_spec((MXU_DIM, MXU_DIM)),
            _const_spec(wk.shape), _const_spec(wv.shape),
        ],
        out_specs=[
            pl.BlockSpec((m_ctx, QK_WIDTH), lambda b: (b, 0)),
            pl.BlockSpec((1, H_DIFF, DV, m_ctx), lambda b: (b, 0, 0, 0)),
        ],
        out_shape=[
            jax.ShapeDtypeStruct((rows, QK_WIDTH), BF16),
            jax.ShapeDtypeStruct((batch, H_DIFF, DV, m_ctx), BF16),
        ],
        compiler_params=_params(("parallel",)),
        name="ctx_kv",
    )(ctx2, shift, scale, g, kg, bd, wk, wv)


def _attn_kernel(lam_ref, q_ref, k_ref, kc_ref, vt_ref, vtc_ref, sg_ref, o_ref, acc1_ref, acc2_ref, *p_bufs,
                 tk, lam_init):
    tq = q_ref.shape[1]
    n_lat = k_ref.shape[1] // tk
    m_ctx = kc_ref.shape[1]
    q = q_ref[0]
    lane = lax.broadcasted_iota(jnp.int32, q.shape, 1)
    zero = jnp.zeros_like(q)
    qz = (jnp.where(lane < DH, q, zero), jnp.where(lane >= DH, q, zero))
    acc_refs = (acc1_ref, acc2_ref)
    p_refs = (p_bufs[:P_BUFFERS], p_bufs[P_BUFFERS:])
    sizes = [tk] * (n_lat - 1) + [tk + m_ctx]
    groups = [tuple(range(g, min(g + PV_GROUP, n_lat))) for g in range(0, n_lat, PV_GROUP)]

    def key_rows(c):
        rows = k_ref[0, c * tk:(c + 1) * tk, :]
        return jnp.concatenate([rows, kc_ref[0]], axis=0) if c == n_lat - 1 else rows

    def value_cols(c):
        cols = vt_ref[0, 0, :, c * tk:(c + 1) * tk]
        return jnp.concatenate([cols, vtc_ref[0, 0]], axis=1) if c == n_lat - 1 else cols

    def finish(l1, l2):
        lp = lam_ref[...]
        lam = (jnp.exp(jnp.sum(lp[0:1] * lp[1:2], axis=-1, keepdims=True))
               - jnp.exp(jnp.sum(lp[2:3] * lp[3:4], axis=-1, keepdims=True)) + lam_init)
        ot = acc1_ref[...] / l1 - lam * (acc2_ref[...] / l2)
        o = ot.T
        ms = jnp.mean(o * o, axis=-1, keepdims=True)
        o = o * lax.rsqrt(ms + EPS) * sg_ref[...] * (1.0 - lam_init)
        o_ref[0] = o.astype(BF16)

    def scores(kr, k0, size, mp):
        return _dot_nt(kr[0, pl.ds(k0, size), :], qz[mp])

    def exp_scores(c):
        kc = key_rows(c)
        sums = []
        for mp in range(2):
            p = jnp.exp2(_dot_nt(kc, qz[mp]))
            p_refs[mp][c % P_BUFFERS][0:sizes[c], :] = p.astype(BF16)
            sums.append(jnp.sum(p, axis=0, keepdims=True))
        return sums

    def pv(gi):
        members = groups[gi]
        vc = jnp.concatenate([value_cols(c) for c in members], axis=1)
        for mp in range(2):
            pp = jnp.concatenate([p_refs[mp][c % P_BUFFERS][0:sizes[c], :] for c in members], axis=0)
            upd = _dot(vc, pp)
            acc_refs[mp][...] = upd if gi == 0 else acc_refs[mp][...] + upd

    l = None
    issued = 0
    for c in range(n_lat):
        sums = exp_scores(c)
        l = sums if l is None else [l[mp] + sums[mp] for mp in range(2)]
        while issued < len(groups) and groups[issued][-1] < c:
            pv(issued)
            issued += 1
    for gi in range(issued, len(groups)):
        pv(gi)
    finish(l[0], l[1])

    in_range = jnp.logical_and(jnp.minimum(l[0], l[1]) > SAFE_SUM_LO, jnp.maximum(l[0], l[1]) < SAFE_SUM_HI)
    all_safe = jnp.min(jnp.where(in_range, 1.0, 0.0)) > 0.5

    @pl.when(jnp.logical_not(all_safe))
    def _():
        def online_step(kr, vc, k0, size, carry):
            out = []
            for mp in range(2):
                m_old, l_old = carry[mp]
                s = scores(kr, k0, size, mp)
                m_new = jnp.maximum(m_old, jnp.max(s, axis=0, keepdims=True))
                alpha = jnp.exp2(m_old - m_new)
                p = jnp.exp2(s - m_new)
                acc_refs[mp][...] = alpha * acc_refs[mp][...] + _dot(vc, p.astype(BF16))
                out.append((m_new, alpha * l_old + jnp.sum(p, axis=0, keepdims=True)))
            return tuple(out)

        def body(c, carry):
            k0 = pl.multiple_of(c * tk, tk)
            return online_step(k_ref, vt_ref[0, 0, :, pl.ds(k0, tk)], k0, tk, carry)

        acc1_ref[...] = jnp.zeros_like(acc1_ref)
        acc2_ref[...] = jnp.zeros_like(acc2_ref)
        start = (jnp.full((1, tq), NEG_BIG, F32), jnp.zeros((1, tq), F32))
        carry = lax.fori_loop(0, n_lat, body, (start, start))
        carry = online_step(kc_ref, vtc_ref[0, 0], 0, m_ctx, carry)
        finish(carry[0][1], carry[1][1])


def _pick_key_chunk(n_keys):
    best = LANES
    for cand in range(LANES, KEY_CHUNK_MAX + 1, LANES):
        if n_keys % cand == 0:
            best = cand
    return best


def _diff_attention(lam_p, q, k, kc, vt, vtc, sg, lam_init):
    batch, n, _ = q.shape
    m_ctx = kc.shape[1]
    tq = Q_BLOCK_ATT
    tk = _pick_key_chunk(n)
    hw = 2 * DH
    kern = functools.partial(_attn_kernel, tk=tk, lam_init=lam_init)
    return pl.pallas_call(
        kern,
        grid=(batch, H_DIFF, n // tq),
        in_specs=[
            _const_spec((4, DH)),
            pl.BlockSpec((1, tq, hw), lambda b, h, i: (b, i, h)),
            pl.BlockSpec((1, n, hw), lambda b, h, i: (b, 0, h)),
            pl.BlockSpec((1, m_ctx, hw), lambda b, h, i: (b, 0, h)),
            pl.BlockSpec((1, 1, DV, n), lambda b, h, i: (b, h, 0, 0)),
            pl.BlockSpec((1, 1, DV, m_ctx), lambda b, h, i: (b, h, 0, 0)),
            _const_spec((1, DV)),
        ],
        out_specs=pl.BlockSpec((1, tq, DV), lambda b, h, i: (b, i, h)),
        out_shape=jax.ShapeDtypeStruct((batch, n, ATT_WIDTH), BF16),
        scratch_shapes=([pltpu.VMEM((DV, tq), F32)] * 2
                        + [pltpu.VMEM((tk + m_ctx, tq), BF16)] * (2 * P_BUFFERS)),
        compiler_params=_params(("parallel", "parallel", "arbitrary")),
        name="diff_attn",
    )(lam_p, q, k, kc, vt, vtc, sg)


def _fft_rows_kernel(f_ref, u_ref, y_ref):
    for j in range(y_ref.shape[1]):
        y_ref[0, j] = _dot(f_ref[...], u_ref[0, j]).astype(BF16)


def _fft_rows(f1cs, u4):
    batch, n2, n1, width = u4.shape
    per_step = min(n2, 16)
    return pl.pallas_call(
        _fft_rows_kernel,
        grid=(batch, n2 // per_step),
        in_specs=[
            _const_spec(f1cs.shape),
            pl.BlockSpec((1, per_step, n1, width), lambda b, j: (b, j, 0, 0)),
        ],
        out_specs=pl.BlockSpec((1, per_step, 2 * n1, width), lambda b, j: (b, j, 0, 0)),
        out_shape=jax.ShapeDtypeStruct((batch, n2, 2 * n1, width), BF16),
        compiler_params=_params(("parallel", "parallel")),
        name="fft_rows",
    )(f1cs, u4)


def _fft_cols_kernel(g_ref, y_ref, bdc_ref, bds_ref, o_ref):
    batch, n2, kb2, width = y_ref.shape
    kb = kb2 // 2
    rows = kb * n2
    bdc = bdc_ref[...]
    bds = bds_ref[...]
    half = MXU_DIM
    for b in range(batch):
        y = y_ref[b].reshape(2 * rows, width)
        z = _dot(g_ref[0], y)
        xr = z[:rows].astype(BF16)
        xi = z[rows:].astype(BF16)
        o = jnp.concatenate(
            [_dot(xr[:, :half], bdc) + _dot(xi[:, :half], bds),
             _dot(xr[:, half:], bdc) + _dot(xi[:, half:], bds)], axis=-1)
        o_ref[b] = o.reshape(n2, kb, width)


def _fft_cols(gbig, y4, bdc, bds):
    batch, n2, rows2, width = y4.shape
    n1 = rows2 // 2
    kb = FFT_KB
    gr = gbig.shape[1]
    return pl.pallas_call(
        _fft_cols_kernel,
        grid=(n1 // kb,),
        in_specs=[
            pl.BlockSpec((1, gr, gr), lambda j: (j, 0, 0)),
            pl.BlockSpec((batch, n2, 2 * kb, width), lambda j: (0, 0, j, 0)),
            _const_spec(bdc.shape), _const_spec(bds.shape),
        ],
        out_specs=pl.BlockSpec((batch, n2, kb, width), lambda j: (0, 0, j, 0)),
        out_shape=jax.ShapeDtypeStruct((batch, n2, n1, width), F32),
        compiler_params=_params(("parallel",)),
        name="fft_cols",
    )(gbig, y4, bdc, bds)


@functools.lru_cache(maxsize=None)
def _fft_tables(n):
    n1 = FFT_N1
    n2 = n // n1
    kb = FFT_KB
    k1 = np.arange(n1)
    ang = 2.0 * np.pi * np.outer(k1, k1) / n1
    f1cs = np.stack([np.cos(ang), -np.sin(ang)], axis=1).reshape(2 * n1, n1)
    scale = 1.0 / math.sqrt(n * F_CH)
    nblk = n1 // kb
    rows = n2 * kb
    gbig = np.zeros((nblk, 2, n2, kb, n2, kb, 2), np.float32)
    n2i = np.arange(n2)
    for j in range(kb):
        kk = (np.arange(nblk)[:, None] * kb + j) + n1 * n2i[None, :]
        a = 2.0 * np.pi * (kk[:, :, None] * n2i[None, None, :] % n) / n
        gr = np.cos(a) * scale
        gi = -np.sin(a) * scale
        gbig[:, 0, :, j, :, j, 0] = gr
        gbig[:, 0, :, j, :, j, 1] = -gi
        gbig[:, 1, :, j, :, j, 0] = gi
        gbig[:, 1, :, j, :, j, 1] = gr
    gbig = gbig.reshape(nblk, 2 * rows, 2 * rows)
    c = np.arange(F_CH)
    angc = 2.0 * np.pi * np.outer(c, c) / F_CH
    eye2 = np.eye(MXU_DIM // F_CH)
    bdc = np.kron(eye2, np.cos(angc))
    bds = np.kron(eye2, np.sin(angc))
    to_bf16 = lambda a: jnp.asarray(a, dtype=F32).astype(BF16)
    return to_bf16(f1cs), to_bf16(gbig), to_bf16(bdc), to_bf16(bds)


def _fourier_mix(u4, batch, n):
    f1cs, gbig, bdc, bds = _fft_tables(n)
    y = _fft_rows(f1cs, u4)
    fo = _fft_cols(gbig, y, bdc, bds)
    return fo.reshape(batch * n, F_WIDTH)


def _tail_kernel(xp_ref, x_ref, xn_ref, fp_ref, f_ref, fn_ref, ap_ref, a_ref, an_ref, gp_ref, g_ref, gn_ref,
                 gate0_ref, wo0_ref, g1_ref, shift_ref, scale_ref, wi_ref, cw_ref, gate1_ref, wo1_ref,
                 o_ref, *, bpb):
    i = pl.program_id(0)
    tm = x_ref.shape[0]
    halo = xp_ref.shape[0]
    ext = tm + 2 * halo
    cat = lambda p, m, nx: jnp.concatenate([p[...], m[...], nx[...]], axis=0)
    x = cat(xp_ref, x_ref, xn_ref)
    gs = cat(gp_ref, g_ref, gn_ref).astype(F32)
    yf = (cat(fp_ref, f_ref, fn_ref) * gs[:, :F_WIDTH]).astype(BF16)
    ya = (cat(ap_ref, a_ref, an_ref).astype(F32) * gs[:, F_WIDTH:]).astype(BF16)
    y = _dot(yf, wo0_ref[:F_WIDTH, :]) + _dot(ya, wo0_ref[F_WIDTH:, :])
    x1 = x + gate0_ref[0] * y
    h = _modulated(x1, g1_ref[...], shift_ref[0], scale_ref[0]).astype(BF16)
    w = x.shape[-1]
    u = _dot(h, wi_ref[:, 1 * w:2 * w]) * _dot(h, wi_ref[:, 2 * w:3 * w])
    row = lax.broadcasted_iota(jnp.int32, u.shape, 0)
    lo = jnp.where((i % bpb) == 0, halo, 0)
    hi = jnp.where((i % bpb) == bpb - 1, halo + tm, ext)
    u = jnp.where(jnp.logical_and(row >= lo, row < hi), u, 0.0)
    cw = cw_ref[...]
    conv = (cw[0:1] * pltpu.roll(u, 1, axis=0)[halo:halo + tm]
            + cw[1:2] * u[halo:halo + tm]
            + cw[2:3] * pltpu.roll(u, ext - 1, axis=0)[halo:halo + tm])
    hm = h[halo:halo + tm]
    bg = _dot(hm, wi_ref[:, 0 * w:1 * w])
    gg = _dot(hm, wi_ref[:, 3 * w:4 * w])
    z = (bg * conv * (gg * _sigmoid(gg))).astype(BF16)
    o_ref[...] = x1[halo:halo + tm] + gate1_ref[0] * _dot(z, wo1_ref[...])


def _tail(x2, fo, ao, gs, gate0, wo0, g1, shift1, scale1, wi, cw, gate1, wo1, n):
    rows, d = x2.shape
    tm = ROW_BLOCK
    bpb = n // tm
    hb = tm // HALO
    nhalo = rows // HALO
    main = lambda w: pl.BlockSpec((tm, w), lambda i: (i, 0))
    prev = lambda w: pl.BlockSpec((HALO, w), lambda i: (jnp.maximum(i * hb - 1, 0), 0))
    nxt = lambda w: pl.BlockSpec((HALO, w), lambda i: (jnp.minimum((i + 1) * hb, nhalo - 1), 0))
    with_halo = lambda w: [prev(w), main(w), nxt(w)]
    per_batch = pl.BlockSpec((1, 1, d), lambda i: (i // bpb, 0, 0))
    kern = functools.partial(_tail_kernel, bpb=bpb)
    return pl.pallas_call(
        kern,
        grid=(rows // tm,),
        in_specs=(with_halo(d) + with_halo(F_WIDTH) + with_halo(ATT_WIDTH) + with_halo(d)
                  + [per_batch, _const_spec(wo0.shape), _const_spec((1, d)), per_batch, per_batch,
                     _const_spec(wi.shape), _const_spec(cw.shape), per_batch, _const_spec(wo1.shape)]),
        out_specs=main(d),
        out_shape=jax.ShapeDtypeStruct((rows, d), F32),
        compiler_params=_params(("parallel",)),
        name="tail",
    )(x2, x2, x2, fo, fo, fo, ao, ao, ao, gs, gs, gs, gate0, wo0, g1, shift1, scale1, wi, cw, gate1, wo1)


@functools.lru_cache(maxsize=None)
def _rope_tables(n):
    rows = n // GRID_W
    pos_r = np.repeat(np.arange(rows, dtype=np.float64), GRID_W)
    pos_c = np.tile(np.arange(GRID_W, dtype=np.float64), rows)
    half = DH // 2
    freqs = ROPE_BASE ** (-np.arange(0, half, 2, dtype=np.float64) / half)
    ang_r = pos_r[:, None] * freqs
    ang_c = pos_c[:, None] * freqs
    cos64 = np.concatenate([np.cos(ang_r), np.cos(ang_r), np.cos(ang_c), np.cos(ang_c)], axis=-1)
    sin64 = np.concatenate([-np.sin(ang_r), np.sin(ang_r), -np.sin(ang_c), np.sin(ang_c)], axis=-1)
    cos = np.concatenate([cos64, cos64], axis=-1).astype(np.float32)
    sin = np.concatenate([sin64, sin64], axis=-1).astype(np.float32)
    return jnp.asarray(cos), jnp.asarray(sin)


@functools.lru_cache(maxsize=None)
def _group_sum_matrix():
    return jnp.asarray(np.kron(np.eye(MXU_DIM // DH), np.ones((DH, DH))), dtype=BF16)


def _lambda_init(layer_idx):
    return 0.8 - 0.6 * math.exp(-0.3 * layer_idx)


def kernel(x, c, ctx, c_ctx, norm_g, ada_w, ada_b, even_w_in, even_q_norm, even_k_norm, even_lambda_q1,
           even_lambda_k1, even_lambda_q2, even_lambda_k2, even_subln, even_w_out, odd_w_in, odd_conv_w,
           odd_w_out):
    batch, n, d = x.shape
    m_ctx = ctx.shape[1]
    assert d == D_MODEL and n % (FFT_N1 * FFT_KB) == 0 and n % ROW_BLOCK == 0 and n % GRID_W == 0
    assert batch < 8 and norm_g.shape[0] == 2

    cvec = jnp.zeros((8, d), F32).at[:batch].set(c).at[batch].set(c_ctx)
    mods = _mod_params(cvec, ada_w, ada_b)
    shift0, scale0, gate0 = [mods[0, :, j * d:(j + 1) * d].reshape(8, 1, d) for j in range(3)]
    shift1, scale1, gate1 = [mods[1, :, j * d:(j + 1) * d].reshape(8, 1, d) for j in range(3)]

    w_in = even_w_in[0].astype(BF16)
    wf = w_in[:, 0:F_WIDTH]
    wq = w_in[:, F_WIDTH:F_WIDTH + QK_WIDTH]
    wk = w_in[:, F_WIDTH + QK_WIDTH:F_WIDTH + 2 * QK_WIDTH]
    wv = w_in[:, F_WIDTH + 2 * QK_WIDTH:F_WIDTH + 2 * QK_WIDTH + ATT_WIDTH]
    wg = w_in[:, F_WIDTH + 2 * QK_WIDTH + ATT_WIDTH:]
    reps = QK_WIDTH // DH
    qg = (jnp.tile(even_q_norm[0].astype(F32), reps) * (ATTN_SCALE * LOG2E)).reshape(1, QK_WIDTH)
    kg = jnp.tile(even_k_norm[0].astype(F32), reps).reshape(1, QK_WIDTH)
    bd = _group_sum_matrix()
    cos, sin = _rope_tables(n)
    g0 = norm_g[0].reshape(1, d)
    g1 = norm_g[1].reshape(1, d)

    x2 = x.reshape(batch * n, d)
    u, q, k, vt, gs = _inproj0(x2, shift0, scale0, g0, cos, sin, qg, kg, bd, wf, wq, wk, wv, wg, batch, n)
    kc, vtc = _ctx_kv(ctx.reshape(batch * m_ctx, d), shift0[batch], scale0[batch], g0, kg, bd, wk, wv,
                      batch, m_ctx)

    lam_p = jnp.stack([even_lambda_q1[0], even_lambda_k1[0], even_lambda_q2[0], even_lambda_k2[0]]).astype(F32)
    sg = even_subln[0].astype(F32).reshape(1, DV)
    ao = _diff_attention(lam_p, q.reshape(batch, n, QK_WIDTH), k.reshape(batch, n, QK_WIDTH),
                         kc.reshape(batch, m_ctx, QK_WIDTH), vt, vtc, sg, _lambda_init(0))

    fo = _fourier_mix(u, batch, n)

    out = _tail(x2, fo, ao.reshape(batch * n, ATT_WIDTH), gs, gate0, even_w_out[0].astype(BF16), g1, shift1,
                scale1, odd_w_in[0].astype(BF16), odd_conv_w[0].astype(F32), gate1, odd_w_out[0].astype(BF16), n)
    return out.reshape(batch, n, d)
```

```python
import functools
import math

import numpy as np
import jax
import jax.numpy as jnp
from jax import lax
from jax.experimental import pallas as pl
from jax.experimental.pallas import tpu as pltpu

F32 = jnp.float32
BF16 = jnp.bfloat16

D_MODEL = 1024
GRID_W = 64
F_GROUPS = 4
F_CH = 128
F_WIDTH = F_GROUPS * F_CH
H_DIFF = 4
DH = 64
DV = 128
QK_WIDTH = H_DIFF * 2 * DH
ATT_WIDTH = H_DIFF * DV
ROPE_BASE = 10000.0
EPS = 1e-6
ATTN_SCALE = DH ** -0.5
LOG2E = 1.4426950408889634

LANES = 128
MXU_DIM = 256
VMEM_LIMIT = 56 * 1024 * 1024

ROW_BLOCK = 1024
HALO = 16
CHANNEL_SLICE = MXU_DIM
FFT_N1 = 128
FFT_KB = 8
Q_BLOCK_ATT = 1024
KEY_CHUNK_MAX = 1024
PV_GROUP = 4
P_BUFFERS = 6
NEG_BIG = -1e30
SAFE_SUM_LO = 2.0 ** -60
SAFE_SUM_HI = 2.0 ** 100


def _dot(a, b):
    return jnp.dot(a, b, preferred_element_type=F32)


def _dot_nt(a, b):
    return lax.dot_general(a, b, (((1,), (1,)), ((), ())), preferred_element_type=F32)


def _sigmoid(x):
    return 1.0 / (1.0 + jnp.exp(-x))


def _params(sem):
    return pltpu.CompilerParams(dimension_semantics=sem, vmem_limit_bytes=VMEM_LIMIT)


def _mod_kernel(c_ref, w_ref, b_ref, o_ref):
    cv = c_ref[...]
    a = cv * _sigmoid(cv)
    w = w_ref[0]
    a_hi = a.astype(BF16)
    a_lo = (a - a_hi.astype(F32)).astype(BF16)
    w_hi = w.astype(BF16)
    w_lo = (w - w_hi.astype(F32)).astype(BF16)
    acc = _dot(a_hi, w_hi) + _dot(a_hi, w_lo) + _dot(a_lo, w_hi)
    o_ref[0] = acc + b_ref[0]


def _mod_params(cvec, ada_w, ada_b):
    depth, d, d3 = ada_w.shape
    return pl.pallas_call(
        _mod_kernel,
        grid=(depth,),
        in_specs=[
            pl.BlockSpec((8, d), lambda l: (0, 0)),
            pl.BlockSpec((1, d, d3), lambda l: (l, 0, 0)),
            pl.BlockSpec((1, 1, d3), lambda l: (l, 0, 0)),
        ],
        out_specs=pl.BlockSpec((1, 8, d3), lambda l: (l, 0, 0)),
        out_shape=jax.ShapeDtypeStruct((depth, 8, d3), F32),
        compiler_params=_params(("parallel",)),
        name="mod_params",
    )(cvec, ada_w, ada_b.reshape(depth, 1, d3))


def _modulated(x, g, shift, scale):
    ms = jnp.mean(x * x, axis=-1, keepdims=True)
    y = x * lax.rsqrt(ms + EPS) * g
    return y * (1.0 + scale) + shift


def _group_norm(t, gain, bd):
    sq = (t * t).astype(BF16)
    half = MXU_DIM
    ss = jnp.concatenate([_dot(sq[:, :half], bd), _dot(sq[:, half:], bd)], axis=-1)
    return t * lax.rsqrt(ss * (1.0 / DH) + EPS) * gain


def _rope(t, cos, sin):
    width = t.shape[-1]
    quarter = DH // 4
    lane = lax.broadcasted_iota(jnp.int32, t.shape, 1)
    from_above = pltpu.roll(t, width - quarter, axis=1)
    from_below = pltpu.roll(t, quarter, axis=1)
    partner = jnp.where((lane & (2 * quarter - 1)) < quarter, from_above, from_below)
    reps = width // cos.shape[-1]
    cosf = jnp.concatenate([cos] * reps, axis=-1)
    sinf = jnp.concatenate([sin] * reps, axis=-1)
    return t * cosf + partner * sinf


def _inproj_kernel(x_ref, shift_ref, scale_ref, g_ref, cos_ref, sin_ref, qg_ref, kg_ref, bd_ref,
                   wf_ref, wq_ref, wk_ref, wv_ref, wg_ref,
                   u_ref, q_ref, k_ref, vt_ref, gs_ref, us_ref):
    h = _modulated(x_ref[...], g_ref[...], shift_ref[0], scale_ref[0]).astype(BF16)
    cos = cos_ref[...]
    sin = sin_ref[...]
    bd = bd_ref[...]
    _, n2, n1_blk, _ = u_ref.shape
    pitch = us_ref.shape[1] // n1_blk
    uf = _dot(h, wf_ref[...])
    for a in range(n1_blk):
        for s in range(F_WIDTH // LANES):
            us_ref[s, a * pitch:a * pitch + n2, :] = uf[a * n2:(a + 1) * n2, s * LANES:(s + 1) * LANES]
    for b in range(n2):
        for s in range(F_WIDTH // LANES):
            col = us_ref[s, pl.ds(b, n1_blk, stride=pitch), :]
            u_ref[0, b, :, s * LANES:(s + 1) * LANES] = col.astype(BF16)
    q = _rope(_group_norm(_dot(h, wq_ref[...]), qg_ref[...], bd), cos, sin)
    q_ref[...] = q.astype(BF16)
    k = _rope(_group_norm(_dot(h, wk_ref[...]), kg_ref[...], bd), cos, sin)
    k_ref[...] = k.astype(BF16)
    v = _dot(h, wv_ref[...])
    for hh in range(H_DIFF):
        vt_ref[0, hh] = v[:, hh * DV:(hh + 1) * DV].T.astype(BF16)
    g = _dot(h, wg_ref[...])
    gs_ref[...] = (g * _sigmoid(g)).astype(BF16)


def _ctx_kv_kernel(x_ref, shift_ref, scale_ref, g_ref, kg_ref, bd_ref, wk_ref, wv_ref, k_ref, vt_ref):
    h = _modulated(x_ref[...], g_ref[...], shift_ref[...], scale_ref[...]).astype(BF16)
    k = _group_norm(_dot(h, wk_ref[...]), kg_ref[...], bd_ref[...])
    k_ref[...] = k.astype(BF16)
    v = _dot(h, wv_ref[...])
    for hh in range(H_DIFF):
        vt_ref[0, hh] = v[:, hh * DV:(hh + 1) * DV].T.astype(BF16)


def _const_spec(shape):
    zeros = (0,) * len(shape)
    return pl.BlockSpec(shape, lambda *_: zeros)


def _inproj0(x2, shift, scale, g, cos, sin, qg, kg, bd, wf, wq, wk, wv, wg, batch, n):
    rows, d = x2.shape
    tm = ROW_BLOCK
    bpb = n // tm
    n2 = n // FFT_N1
    n1_blk = tm // n2
    assert tm % n2 == 0 and n2 % 8 == 0 and n1_blk % 16 == 0
    row_spec = lambda w: pl.BlockSpec((tm, w), lambda i: (i, 0))
    return pl.pallas_call(
        _inproj_kernel,
        grid=(rows // tm,),
        in_specs=[
            row_spec(d),
            pl.BlockSpec((1, 1, d), lambda i: (i // bpb, 0, 0)),
            pl.BlockSpec((1, 1, d), lambda i: (i // bpb, 0, 0)),
            _const_spec((1, d)),
            pl.BlockSpec((tm, LANES), lambda i: (i % bpb, 0)),
            pl.BlockSpec((tm, LANES), lambda i: (i % bpb, 0)),
            _const_spec((1, QK_WIDTH)),
            _const_spec((1, QK_WIDTH)),
            _const_spec((MXU_DIM, MXU_DIM)),
            _const_spec(wf.shape), _const_spec(wq.shape), _const_spec(wk.shape),
            _const_spec(wv.shape), _const_spec(wg.shape),
        ],
        out_specs=[
            pl.BlockSpec((1, n2, n1_blk, F_WIDTH), lambda i: (i // bpb, 0, i % bpb, 0)),
            row_spec(QK_WIDTH), row_spec(QK_WIDTH),
            pl.BlockSpec((1, H_DIFF, DV, tm), lambda i: (i // bpb, 0, 0, i % bpb)),
            row_spec(d),
        ],
        out_shape=[
            jax.ShapeDtypeStruct((batch, n2, FFT_N1, F_WIDTH), BF16),
            jax.ShapeDtypeStruct((rows, QK_WIDTH), BF16),
            jax.ShapeDtypeStruct((rows, QK_WIDTH), BF16),
            jax.ShapeDtypeStruct((batch, H_DIFF, DV, n), BF16),
            jax.ShapeDtypeStruct((rows, d), BF16),
        ],
        scratch_shapes=[pltpu.VMEM((F_WIDTH // LANES, n1_blk * (n2 + 4), LANES), F32)],
        compiler_params=_params(("parallel",)),
        name="inproj0",
    )(x2, shift, scale, g, cos, sin, qg, kg, bd, wf, wq, wk, wv, wg)


def _ctx_kv(ctx2, shift, scale, g, kg, bd, wk, wv, batch, m_ctx):
    rows, d = ctx2.shape
    return pl.pallas_call(
        _ctx_kv_kernel,
        grid=(batch,),
        in_specs=[
            pl.BlockSpec((m_ctx, d), lambda b: (b, 0)),
            _const_spec((1, d)), _const_spec((1, d)), _const_spec((1, d)),
            _const_spec((1, QK_WIDTH)), _const_spec((MXU_DIM, MXU_DIM)),
            _const_spec(wk.shape), _const_spec(wv.shape),
        ],
        out_specs=[
            pl.BlockSpec((m_ctx, QK_WIDTH), lambda b: (b, 0)),
            pl.BlockSpec((1, H_DIFF, DV, m_ctx), lambda b: (b, 0, 0, 0)),
        ],
        out_shape=[
            jax.ShapeDtypeStruct((rows, QK_WIDTH), BF16),
            jax.ShapeDtypeStruct((batch, H_DIFF, DV, m_ctx), BF16),
        ],
        compiler_params=_params(("parallel",)),
        name="ctx_kv",
    )(ctx2, shift, scale, g, kg, bd, wk, wv)


def _attn_kernel(lam_ref, q_ref, k_ref, kc_ref, vt_ref, vtc_ref, sg_ref, o_ref, acc1_ref, acc2_ref, *p_bufs,
                 tk, lam_init):
    tq = q_ref.shape[1]
    n_lat = k_ref.shape[1] // tk
    m_ctx = kc_ref.shape[1]
    q = q_ref[0]
    lane = lax.broadcasted_iota(jnp.int32, q.shape, 1)
    zero = jnp.zeros_like(q)
    qz = (jnp.where(lane < DH, q, zero), jnp.where(lane >= DH, q, zero))
    acc_refs = (acc1_ref, acc2_ref)
    p_refs = (p_bufs[:P_BUFFERS], p_bufs[P_BUFFERS:])
    sizes = [tk] * (n_lat - 1) + [tk + m_ctx]
    groups = [tuple(range(g, min(g + PV_GROUP, n_lat))) for g in range(0, n_lat, PV_GROUP)]

    def key_rows(c):
        rows = k_ref[0, c * tk:(c + 1) * tk, :]
        return jnp.concatenate([rows, kc_ref[0]], axis=0) if c == n_lat - 1 else rows

    def value_cols(c):
        cols = vt_ref[0, 0, :, c * tk:(c + 1) * tk]
        return jnp.concatenate([cols, vtc_ref[0, 0]], axis=1) if c == n_lat - 1 else cols

    def finish(l1, l2):
        lp = lam_ref[...]
        lam = (jnp.exp(jnp.sum(lp[0:1] * lp[1:2], axis=-1, keepdims=True))
               - jnp.exp(jnp.sum(lp[2:3] * lp[3:4], axis=-1, keepdims=True)) + lam_init)
        ot = acc1_ref[...] / l1 - lam * (acc2_ref[...] / l2)
        o = ot.T
        ms = jnp.mean(o * o, axis=-1, keepdims=True)
        o = o * lax.rsqrt(ms + EPS) * sg_ref[...] * (1.0 - lam_init)
        o_ref[0] = o.astype(BF16)

    def scores(kr, k0, size, mp):
        return _dot_nt(kr[0, pl.ds(k0, size), :], qz[mp])

    def exp_scores(c):
        kc = key_rows(c)
        sums = []
        for mp in range(2):
            p = jnp.exp2(_dot_nt(kc, qz[mp]))
            p_refs[mp][c % P_BUFFERS][0:sizes[c], :] = p.astype(BF16)
            sums.append(jnp.sum(p, axis=0, keepdims=True))
        return sums

    def pv(gi):
        members = groups[gi]
        vc = jnp.concatenate([value_cols(c) for c in members], axis=1)
        for mp in range(2):
            pp = jnp.concatenate([p_refs[mp][c % P_BUFFERS][0:sizes[c], :] for c in members], axis=0)
            upd = _dot(vc, pp)
            acc_refs[mp][...] = upd if gi == 0 else acc_refs[mp][...] + upd

    l = None
    issued = 0
    for c in range(n_lat):
        sums = exp_scores(c)
        l = sums if l is None else [l[mp] + sums[mp] for mp in range(2)]
        while issued < len(groups) and groups[issued][-1] < c:
            pv(issued)
            issued += 1
    for gi in range(issued, len(groups)):
        pv(gi)
    finish(l[0], l[1])

    in_range = jnp.logical_and(jnp.minimum(l[0], l[1]) > SAFE_SUM_LO, jnp.maximum(l[0], l[1]) < SAFE_SUM_HI)
    all_safe = jnp.min(jnp.where(in_range, 1.0, 0.0)) > 0.5

    @pl.when(jnp.logical_not(all_safe))
    def _():
        def online_step(kr, vc, k0, size, carry):
            out = []
            for mp in range(2):
                m_old, l_old = carry[mp]
                s = scores(kr, k0, size, mp)
                m_new = jnp.maximum(m_old, jnp.max(s, axis=0, keepdims=True))
                alpha = jnp.exp2(m_old - m_new)
                p = jnp.exp2(s - m_new)
                acc_refs[mp][...] = alpha * acc_refs[mp][...] + _dot(vc, p.astype(BF16))
                out.append((m_new, alpha * l_old + jnp.sum(p, axis=0, keepdims=True)))
            return tuple(out)

        def body(c, carry):
            k0 = pl.multiple_of(c * tk, tk)
            return online_step(k_ref, vt_ref[0, 0, :, pl.ds(k0, tk)], k0, tk, carry)

        acc1_ref[...] = jnp.zeros_like(acc1_ref)
        acc2_ref[...] = jnp.zeros_like(acc2_ref)
        start = (jnp.full((1, tq), NEG_BIG, F32), jnp.zeros((1, tq), F32))
        carry = lax.fori_loop(0, n_lat, body, (start, start))
        carry = online_step(kc_ref, vtc_ref[0, 0], 0, m_ctx, carry)
        finish(carry[0][1], carry[1][1])


def _pick_key_chunk(n_keys):
    best = LANES
    for cand in range(LANES, KEY_CHUNK_MAX + 1, LANES):
        if n_keys % cand == 0:
            best = cand
    return best


def _diff_attention(lam_p, q, k, kc, vt, vtc, sg, lam_init):
    batch, n, _ = q.shape
    m_ctx = kc.shape[1]
    tq = Q_BLOCK_ATT
    tk = _pick_key_chunk(n)
    hw = 2 * DH
    kern = functools.partial(_attn_kernel, tk=tk, lam_init=lam_init)
    return pl.pallas_call(
        kern,
        grid=(batch, H_DIFF, n // tq),
        in_specs=[
            _const_spec((4, DH)),
            pl.BlockSpec((1, tq, hw), lambda b, h, i: (b, i, h)),
            pl.BlockSpec((1, n, hw), lambda b, h, i: (b, 0, h)),
            pl.BlockSpec((1, m_ctx, hw), lambda b, h, i: (b, 0, h)),
            pl.BlockSpec((1, 1, DV, n), lambda b, h, i: (b, h, 0, 0)),
            pl.BlockSpec((1, 1, DV, m_ctx), lambda b, h, i: (b, h, 0, 0)),
            _const_spec((1, DV)),
        ],
        out_specs=pl.BlockSpec((1, tq, DV), lambda b, h, i: (b, i, h)),
        out_shape=jax.ShapeDtypeStruct((batch, n, ATT_WIDTH), BF16),
        scratch_shapes=([pltpu.VMEM((DV, tq), F32)] * 2
                        + [pltpu.VMEM((tk + m_ctx, tq), BF16)] * (2 * P_BUFFERS)),
        compiler_params=_params(("parallel", "parallel", "arbitrary")),
        name="diff_attn",
    )(lam_p, q, k, kc, vt, vtc, sg)


def _fft_rows_kernel(f_ref, u_ref, y_ref):
    for j in range(y_ref.shape[1]):
        y_ref[0, j] = _dot(f_ref[...], u_ref[0, j]).astype(BF16)


def _fft_rows(f1cs, u4):
    batch, n2, n1, width = u4.shape
    per_step = min(n2, 16)
    return pl.pallas_call(
        _fft_rows_kernel,
        grid=(batch, n2 // per_step),
        in_specs=[
            _const_spec(f1cs.shape),
            pl.BlockSpec((1, per_step, n1, width), lambda b, j: (b, j, 0, 0)),
        ],
        out_specs=pl.BlockSpec((1, per_step, 2 * n1, width), lambda b, j: (b, j, 0, 0)),
        out_shape=jax.ShapeDtypeStruct((batch, n2, 2 * n1, width), BF16),
        compiler_params=_params(("parallel", "parallel")),
        name="fft_rows",
    )(f1cs, u4)


def _fft_cols_kernel(g_ref, y_ref, bdc_ref, bds_ref, o_ref):
    batch, n2, kb2, width = y_ref.shape
    kb = kb2 // 2
    rows = kb * n2
    bdc = bdc_ref[...]
    bds = bds_ref[...]
    half = MXU_DIM
    for b in range(batch):
        y = y_ref[b].reshape(2 * rows, width)
        z = _dot(g_ref[0], y)
        xr = z[:rows].astype(BF16)
        xi = z[rows:].astype(BF16)
        o = jnp.concatenate(
            [_dot(xr[:, :half], bdc) + _dot(xi[:, :half], bds),
             _dot(xr[:, half:], bdc) + _dot(xi[:, half:], bds)], axis=-1)
        o_ref[b] = o.reshape(n2, kb, width)


def _fft_cols(gbig, y4, bdc, bds):
    batch, n2, rows2, width = y4.shape
    n1 = rows2 // 2
    kb = FFT_KB
    gr = gbig.shape[1]
    return pl.pallas_call(
        _fft_cols_kernel,
        grid=(n1 // kb,),
        in_specs=[
            pl.BlockSpec((1, gr, gr), lambda j: (j, 0, 0)),
            pl.BlockSpec((batch, n2, 2 * kb, width), lambda j: (0, 0, j, 0)),
            _const_spec(bdc.shape), _const_spec(bds.shape),
        ],
        out_specs=pl.BlockSpec((batch, n2, kb, width), lambda j: (0, 0, j, 0)),
        out_shape=jax.ShapeDtypeStruct((batch, n2, n1, width), F32),
        compiler_params=_params(("parallel",)),
        name="fft_cols",
    )(gbig, y4, bdc, bds)


@functools.lru_cache(maxsize=None)
def _fft_tables(n):
    n1 = FFT_N1
    n2 = n // n1
    kb = FFT_KB
    k1 = np.arange(n1)
    ang = 2.0 * np.pi * np.outer(k1, k1) / n1
    f1cs = np.stack([np.cos(ang), -np.sin(ang)], axis=1).reshape(2 * n1, n1)
    scale = 1.0 / math.sqrt(n * F_CH)
    nblk = n1 // kb
    rows = n2 * kb
    gbig = np.zeros((nblk, 2, n2, kb, n2, kb, 2), np.float32)
    n2i = np.arange(n2)
    for j in range(kb):
        kk = (np.arange(nblk)[:, None] * kb + j) + n1 * n2i[None, :]
        a = 2.0 * np.pi * (kk[:, :, None] * n2i[None, None, :] % n) / n
        gr = np.cos(a) * scale
        gi = -np.sin(a) * scale
        gbig[:, 0, :, j, :, j, 0] = gr
        gbig[:, 0, :, j, :, j, 1] = -gi
        gbig[:, 1, :, j, :, j, 0] = gi
        gbig[:, 1, :, j, :, j, 1] = gr
    gbig = gbig.reshape(nblk, 2 * rows, 2 * rows)
    c = np.arange(F_CH)
    angc = 2.0 * np.pi * np.outer(c, c) / F_CH
    eye2 = np.eye(MXU_DIM // F_CH)
    bdc = np.kron(eye2, np.cos(angc))
    bds = np.kron(eye2, np.sin(angc))
    to_bf16 = lambda a: jnp.asarray(a, dtype=F32).astype(BF16)
    return to_bf16(f1cs), to_bf16(gbig), to_bf16(bdc), to_bf16(bds)


def _fourier_mix(u4, batch, n):
    f1cs, gbig, bdc, bds = _fft_tables(n)
    y = _fft_rows(f1cs, u4)
    fo = _fft_cols(gbig, y, bdc, bds)
    return fo.reshape(batch * n, F_WIDTH)


def _tail_kernel(xp_ref, x_ref, xn_ref, fp_ref, f_ref, fn_ref, ap_ref, a_ref, an_ref, gp_ref, g_ref, gn_ref,
                 gate0_ref, wo0_ref, g1_ref, shift_ref, scale_ref, wi_ref, cw_ref, gate1_ref, wo1_ref,
                 o_ref, *, bpb):
    i = pl.program_id(0)
    tm = x_ref.shape[0]
    halo = xp_ref.shape[0]
    ext = tm + 2 * halo
    cat = lambda p, m, nx: jnp.concatenate([p[...], m[...], nx[...]], axis=0)
    x = cat(xp_ref, x_ref, xn_ref)
    gs = cat(gp_ref, g_ref, gn_ref).astype(F32)
    yf = (cat(fp_ref, f_ref, fn_ref) * gs[:, :F_WIDTH]).astype(BF16)
    ya = (cat(ap_ref, a_ref, an_ref).astype(F32) * gs[:, F_WIDTH:]).astype(BF16)
    w = x.shape[-1]
    cs = CHANNEL_SLICE
    gate0 = gate0_ref[0]
    x1_parts = []
    ssq = None
    for part in range(w // cs):
        sl = slice(part * cs, (part + 1) * cs)
        y = _dot(yf, wo0_ref[:F_WIDTH, sl]) + _dot(ya, wo0_ref[F_WIDTH:, sl])
        x1_part = x[:, sl] + gate0[:, sl] * y
        x1_parts.append(x1_part)
        part_sum = jnp.sum(x1_part * x1_part, axis=-1, keepdims=True)
        ssq = part_sum if ssq is None else ssq + part_sum
    x1 = jnp.concatenate(x1_parts, axis=-1)
    inv_rms = lax.rsqrt(ssq * (1.0 / w) + EPS)
    h = ((x1 * inv_rms * g1_ref[...]) * (1.0 + scale_ref[0]) + shift_ref[0]).astype(BF16)
    row = lax.broadcasted_iota(jnp.int32, (ext, cs), 0)
    lo = jnp.where((i % bpb) == 0, halo, 0)
    hi = jnp.where((i % bpb) == bpb - 1, halo + tm, ext)
    in_sequence = jnp.logical_and(row >= lo, row < hi)
    cw = cw_ref[...]
    hm = h[halo:halo + tm]
    y1 = None
    for part in range(w // cs):
        c0 = part * cs
        u = _dot(h, wi_ref[:, w + c0:w + c0 + cs]) * _dot(h, wi_ref[:, 2 * w + c0:2 * w + c0 + cs])
        u = jnp.where(in_sequence, u, 0.0)
        conv = (cw[0:1, c0:c0 + cs] * pltpu.roll(u, 1, axis=0)[halo:halo + tm]
                + cw[1:2, c0:c0 + cs] * u[halo:halo + tm]
                + cw[2:3, c0:c0 + cs] * pltpu.roll(u, ext - 1, axis=0)[halo:halo + tm])
        bg = _dot(hm, wi_ref[:, c0:c0 + cs])
        gg = _dot(hm, wi_ref[:, 3 * w + c0:3 * w + c0 + cs])
        z = (bg * conv * (gg * _sigmoid(gg))).astype(BF16)
        y_part = _dot(z, wo1_ref[c0:c0 + cs, :])
        y1 = y_part if y1 is None else y1 + y_part
    o_ref[...] = x1[halo:halo + tm] + gate1_ref[0] * y1


def _tail(x2, fo, ao, gs, gate0, wo0, g1, shift1, scale1, wi, cw, gate1, wo1, n):
    rows, d = x2.shape
    tm = ROW_BLOCK
    bpb = n // tm
    hb = tm // HALO
    nhalo = rows // HALO
    main = lambda w: pl.BlockSpec((tm, w), lambda i: (i, 0))
    prev = lambda w: pl.BlockSpec((HALO, w), lambda i: (jnp.maximum(i * hb - 1, 0), 0))
    nxt = lambda w: pl.BlockSpec((HALO, w), lambda i: (jnp.minimum((i + 1) * hb, nhalo - 1), 0))
    with_halo = lambda w: [prev(w), main(w), nxt(w)]
    per_batch = pl.BlockSpec((1, 1, d), lambda i: (i // bpb, 0, 0))
    kern = functools.partial(_tail_kernel, bpb=bpb)
    return pl.pallas_call(
        kern,
        grid=(rows // tm,),
        in_specs=(with_halo(d) + with_halo(F_WIDTH) + with_halo(ATT_WIDTH) + with_halo(d)
                  + [per_batch, _const_spec(wo0.shape), _const_spec((1, d)), per_batch, per_batch,
                     _const_spec(wi.shape), _const_spec(cw.shape), per_batch, _const_spec(wo1.shape)]),
        out_specs=main(d),
        out_shape=jax.ShapeDtypeStruct((rows, d), F32),
        compiler_params=_params(("parallel",)),
        name="tail",
    )(x2, x2, x2, fo, fo, fo, ao, ao, ao, gs, gs, gs, gate0, wo0, g1, shift1, scale1, wi, cw, gate1, wo1)


@functools.lru_cache(maxsize=None)
def _rope_tables(n):
    rows = n // GRID_W
    pos_r = np.repeat(np.arange(rows, dtype=np.float64), GRID_W)
    pos_c = np.tile(np.arange(GRID_W, dtype=np.float64), rows)
    half = DH // 2
    freqs = ROPE_BASE ** (-np.arange(0, half, 2, dtype=np.float64) / half)
    ang_r = pos_r[:, None] * freqs
    ang_c = pos_c[:, None] * freqs
    cos64 = np.concatenate([np.cos(ang_r), np.cos(ang_r), np.cos(ang_c), np.cos(ang_c)], axis=-1)
    sin64 = np.concatenate([-np.sin(ang_r), np.sin(ang_r), -np.sin(ang_c), np.sin(ang_c)], axis=-1)
    cos = np.concatenate([cos64, cos64], axis=-1).astype(np.float32)
    sin = np.concatenate([sin64, sin64], axis=-1).astype(np.float32)
    return jnp.asarray(cos), jnp.asarray(sin)


@functools.lru_cache(maxsize=None)
def _group_sum_matrix():
    return jnp.asarray(np.kron(np.eye(MXU_DIM // DH), np.ones((DH, DH))), dtype=BF16)


def _lambda_init(layer_idx):
    return 0.8 - 0.6 * math.exp(-0.3 * layer_idx)


def kernel(x, c, ctx, c_ctx, norm_g, ada_w, ada_b, even_w_in, even_q_norm, even_k_norm, even_lambda_q1,
           even_lambda_k1, even_lambda_q2, even_lambda_k2, even_subln, even_w_out, odd_w_in, odd_conv_w,
           odd_w_out):
    batch, n, d = x.shape
    m_ctx = ctx.shape[1]
    assert d == D_MODEL and n % (FFT_N1 * FFT_KB) == 0 and n % ROW_BLOCK == 0 and n % GRID_W == 0
    assert batch < 8 and norm_g.shape[0] == 2

    cvec = jnp.zeros((8, d), F32).at[:batch].set(c).at[batch].set(c_ctx)
    mods = _mod_params(cvec, ada_w, ada_b)
    shift0, scale0, gate0 = [mods[0, :, j * d:(j + 1) * d].reshape(8, 1, d) for j in range(3)]
    shift1, scale1, gate1 = [mods[1, :, j * d:(j + 1) * d].reshape(8, 1, d) for j in range(3)]

    w_in = even_w_in[0].astype(BF16)
    wf = w_in[:, 0:F_WIDTH]
    wq = w_in[:, F_WIDTH:F_WIDTH + QK_WIDTH]
    wk = w_in[:, F_WIDTH + QK_WIDTH:F_WIDTH + 2 * QK_WIDTH]
    wv = w_in[:, F_WIDTH + 2 * QK_WIDTH:F_WIDTH + 2 * QK_WIDTH + ATT_WIDTH]
    wg = w_in[:, F_WIDTH + 2 * QK_WIDTH + ATT_WIDTH:]
    reps = QK_WIDTH // DH
    qg = (jnp.tile(even_q_norm[0].astype(F32), reps) * (ATTN_SCALE * LOG2E)).reshape(1, QK_WIDTH)
    kg = jnp.tile(even_k_norm[0].astype(F32), reps).reshape(1, QK_WIDTH)
    bd = _group_sum_matrix()
    cos, sin = _rope_tables(n)
    g0 = norm_g[0].reshape(1, d)
    g1 = norm_g[1].reshape(1, d)

    x2 = x.reshape(batch * n, d)
    u, q, k, vt, gs = _inproj0(x2, shift0, scale0, g0, cos, sin, qg, kg, bd, wf, wq, wk, wv, wg, batch, n)
    kc, vtc = _ctx_kv(ctx.reshape(batch * m_ctx, d), shift0[batch], scale0[batch], g0, kg, bd, wk, wv,
                      batch, m_ctx)

    lam_p = jnp.stack([even_lambda_q1[0], even_lambda_k1[0], even_lambda_q2[0], even_lambda_k2[0]]).astype(F32)
    sg = even_subln[0].astype(F32).reshape(1, DV)
    ao = _diff_attention(lam_p, q.reshape(batch, n, QK_WIDTH), k.reshape(batch, n, QK_WIDTH),
                         kc.reshape(batch, m_ctx, QK_WIDTH), vt, vtc, sg, _lambda_init(0))

    fo = _fourier_mix(u, batch, n)

    out = _tail(x2, fo, ao.reshape(batch * n, ATT_WIDTH), gs, gate0, even_w_out[0].astype(BF16), g1, shift1,
                scale1, odd_w_in[0].astype(BF16), odd_conv_w[0].astype(F32), gate1, odd_w_out[0].astype(BF16), n)
    return out.reshape(batch, n, d)
```

```python
import functools
import math

import numpy as np
import jax
import jax.numpy as jnp
from jax import lax
from jax.experimental import pallas as pl
from jax.experimental.pallas import tpu as pltpu

F32 = jnp.float32
BF16 = jnp.bfloat16

D_MODEL = 1024
GRID_W = 64
F_GROUPS = 4
F_CH = 128
F_WIDTH = F_GROUPS * F_CH
H_DIFF = 4
DH = 64
DV = 128
QK_WIDTH = H_DIFF * 2 * DH
ATT_WIDTH = H_DIFF * DV
ROPE_BASE = 10000.0
EPS = 1e-6
ATTN_SCALE = DH ** -0.5
LOG2E = 1.4426950408889634

LANES = 128
MXU_DIM = 256
VMEM_LIMIT = 56 * 1024 * 1024

ROW_BLOCK = 1024
HALO = 16
CHANNEL_SLICE = MXU_DIM
FFT_N1 = 128
FFT_KB = 8
Q_BLOCK_ATT = 1024
KEY_CHUNK_MAX = 1024
PV_GROUP = 4
P_BUFFERS = 6
NEG_BIG = -1e30
SAFE_SUM_LO = 2.0 ** -60
SAFE_SUM_HI = 2.0 ** 100


def _dot(a, b):
    return jnp.dot(a, b, preferred_element_type=F32)


def _dot_nt(a, b):
    return lax.dot_general(a, b, (((1,), (1,)), ((), ())), preferred_element_type=F32)


def _sigmoid(x):
    return 1.0 / (1.0 + jnp.exp(-x))


def _params(sem):
    return pltpu.CompilerParams(dimension_semantics=sem, vmem_limit_bytes=VMEM_LIMIT)


def _mod_kernel(c_ref, w_ref, b_ref, o_ref):
    cv = c_ref[...]
    a = cv * _sigmoid(cv)
    w = w_ref[0]
    a_hi = a.astype(BF16)
    a_lo = (a - a_hi.astype(F32)).astype(BF16)
    w_hi = w.astype(BF16)
    w_lo = (w - w_hi.astype(F32)).astype(BF16)
    acc = _dot(a_hi, w_hi) + _dot(a_hi, w_lo) + _dot(a_lo, w_hi)
    o_ref[0] = acc + b_ref[0]


def _mod_params(cvec, ada_w, ada_b):
    depth, d, d3 = ada_w.shape
    return pl.pallas_call(
        _mod_kernel,
        grid=(depth,),
        in_specs=[
            pl.BlockSpec((8, d), lambda l: (0, 0)),
            pl.BlockSpec((1, d, d3), lambda l: (l, 0, 0)),
            pl.BlockSpec((1, 1, d3), lambda l: (l, 0, 0)),
        ],
        out_specs=pl.BlockSpec((1, 8, d3), lambda l: (l, 0, 0)),
        out_shape=jax.ShapeDtypeStruct((depth, 8, d3), F32),
        compiler_params=_params(("parallel",)),
        name="mod_params",
    )(cvec, ada_w, ada_b.reshape(depth, 1, d3))


def _modulated(x, g, shift, scale):
    ms = jnp.mean(x * x, axis=-1, keepdims=True)
    y = x * lax.rsqrt(ms + EPS) * g
    return y * (1.0 + scale) + shift


def _group_norm(t, gain, bd):
    sq = (t * t).astype(BF16)
    half = MXU_DIM
    ss = jnp.concatenate([_dot(sq[:, :half], bd), _dot(sq[:, half:], bd)], axis=-1)
    return t * lax.rsqrt(ss * (1.0 / DH) + EPS) * gain


def _rope(t, cos, sin):
    width = t.shape[-1]
    quarter = DH // 4
    lane = lax.broadcasted_iota(jnp.int32, t.shape, 1)
    from_above = pltpu.roll(t, width - quarter, axis=1)
    from_below = pltpu.roll(t, quarter, axis=1)
    partner = jnp.where((lane & (2 * quarter - 1)) < quarter, from_above, from_below)
    reps = width // cos.shape[-1]
    cosf = jnp.concatenate([cos] * reps, axis=-1)
    sinf = jnp.concatenate([sin] * reps, axis=-1)
    return t * cosf + partner * sinf


def _inproj_kernel(x_ref, shift_ref, scale_ref, g_ref, cos_ref, sin_ref, qg_ref, kg_ref, bd_ref,
                   wf_ref, wq_ref, wk_ref, wv_ref, wg_ref,
                   u_ref, q_ref, k_ref, vt_ref, gs_ref, us_ref):
    h = _modulated(x_ref[...], g_ref[...], shift_ref[0], scale_ref[0]).astype(BF16)
    cos = cos_ref[...]
    sin = sin_ref[...]
    bd = bd_ref[...]
    _, n2, n1_blk, _ = u_ref.shape
    pitch = us_ref.shape[1] // n1_blk
    uf = _dot(h, wf_ref[...])
    for a in range(n1_blk):
        for s in range(F_WIDTH // LANES):
            us_ref[s, a * pitch:a * pitch + n2, :] = uf[a * n2:(a + 1) * n2, s * LANES:(s + 1) * LANES]
    for b in range(n2):
        for s in range(F_WIDTH // LANES):
            col = us_ref[s, pl.ds(b, n1_blk, stride=pitch), :]
            u_ref[0, b, :, s * LANES:(s + 1) * LANES] = col.astype(BF16)
    q = _rope(_group_norm(_dot(h, wq_ref[...]), qg_ref[...], bd), cos, sin)
    q_ref[...] = q.astype(BF16)
    k = _rope(_group_norm(_dot(h, wk_ref[...]), kg_ref[...], bd), cos, sin)
    k_ref[...] = k.astype(BF16)
    v = _dot(h, wv_ref[...])
    for hh in range(H_DIFF):
        vt_ref[0, hh] = v[:, hh * DV:(hh + 1) * DV].T.astype(BF16)
    g = _dot(h, wg_ref[...])
    gs_ref[...] = (g * _sigmoid(g)).astype(BF16)


def _ctx_kv_kernel(x_ref, shift_ref, scale_ref, g_ref, kg_ref, bd_ref, wk_ref, wv_ref, k_ref, vt_ref):
    h = _modulated(x_ref[...], g_ref[...], shift_ref[...], scale_ref[...]).astype(BF16)
    k = _group_norm(_dot(h, wk_ref[...]), kg_ref[...], bd_ref[...])
    k_ref[...] = k.astype(BF16)
    v = _dot(h, wv_ref[...])
    for hh in range(H_DIFF):
        vt_ref[0, hh] = v[:, hh * DV:(hh + 1) * DV].T.astype(BF16)


def _const_spec(shape):
    zeros = (0,) * len(shape)
    return pl.BlockSpec(shape, lambda *_: zeros)


def _inproj0(x2, shift, scale, g, cos, sin, qg, kg, bd, wf, wq, wk, wv, wg, batch, n):
    rows, d = x2.shape
    tm = ROW_BLOCK
    bpb = n // tm
    n2 = n // FFT_N1
    n1_blk = tm // n2
    assert tm % n2 == 0 and n2 % 8 == 0 and n1_blk % 16 == 0
    row_spec = lambda w: pl.BlockSpec((tm, w), lambda i: (i, 0))
    return pl.pallas_call(
        _inproj_kernel,
        grid=(rows // tm,),
        in_specs=[
            row_spec(d),
            pl.BlockSpec((1, 1, d), lambda i: (i // bpb, 0, 0)),
            pl.BlockSpec((1, 1, d), lambda i: (i // bpb, 0, 0)),
            _const_spec((1, d)),
            pl.BlockSpec((tm, LANES), lambda i: (i % bpb, 0)),
            pl.BlockSpec((tm, LANES), lambda i: (i % bpb, 0)),
            _const_spec((1, QK_WIDTH)),
            _const_spec((1, QK_WIDTH)),
            _const_spec((MXU_DIM, MXU_DIM)),
            _const_spec(wf.shape), _const_spec(wq.shape), _const_spec(wk.shape),
            _const_spec(wv.shape), _const_spec(wg.shape),
        ],
        out_specs=[
            pl.BlockSpec((1, n2, n1_blk, F_WIDTH), lambda i: (i // bpb, 0, i % bpb, 0)),
            row_spec(QK_WIDTH), row_spec(QK_WIDTH),
            pl.BlockSpec((1, H_DIFF, DV, tm), lambda i: (i // bpb, 0, 0, i % bpb)),
            row_spec(d),
        ],
        out_shape=[
            jax.ShapeDtypeStruct((batch, n2, FFT_N1, F_WIDTH), BF16),
            jax.ShapeDtypeStruct((rows, QK_WIDTH), BF16),
            jax.ShapeDtypeStruct((rows, QK_WIDTH), BF16),
            jax.ShapeDtypeStruct((batch, H_DIFF, DV, n), BF16),
            jax.ShapeDtypeStruct((rows, d), BF16),
        ],
        scratch_shapes=[pltpu.VMEM((F_WIDTH // LANES, n1_blk * (n2 + 4), LANES), F32)],
        compiler_params=_params(("parallel",)),
        name="inproj0",
    )(x2, shift, scale, g, cos, sin, qg, kg, bd, wf, wq, wk, wv, wg)


def _ctx_kv(ctx2, shift, scale, g, kg, bd, wk, wv, batch, m_ctx):
    rows, d = ctx2.shape
    return pl.pallas_call(
        _ctx_kv_kernel,
        grid=(batch,),
        in_specs=[
            pl.BlockSpec((m_ctx, d), lambda b: (b, 0)),
            _const_spec((1, d)), _const_spec((1, d)), _const_spec((1, d)),
            _const_spec((1, QK_WIDTH)), _const_spec((MXU_DIM, MXU_DIM)),
            _const_spec(wk.shape), _const_spec(wv.shape),
        ],
        out_specs=[
            pl.BlockSpec((m_ctx, QK_WIDTH), lambda b: (b, 0)),
            pl.BlockSpec((1, H_DIFF, DV, m_ctx), lambda b: (b, 0, 0, 0)),
        ],
        out_shape=[
            jax.ShapeDtypeStruct((rows, QK_WIDTH), BF16),
            jax.ShapeDtypeStruct((batch, H_DIFF, DV, m_ctx), BF16),
        ],
        compiler_params=_params(("parallel",)),
        name="ctx_kv",
    )(ctx2, shift, scale, g, kg, bd, wk, wv)


def _masked_queries(q_ref):
    q = q_ref[0]
    lane = lax.broadcasted_iota(jnp.int32, q.shape, 1)
    zero = jnp.zeros_like(q)
    return (jnp.where(lane < DH, q, zero), jnp.where(lane >= DH, q, zero))


def _attn_finish(lam_ref, sg_ref, o_ref, acc1_ref, acc2_ref, l1, l2, lam_init):
    lp = lam_ref[...]
    lam = (jnp.exp(jnp.sum(lp[0:1] * lp[1:2], axis=-1, keepdims=True))
           - jnp.exp(jnp.sum(lp[2:3] * lp[3:4], axis=-1, keepdims=True)) + lam_init)
    ot = acc1_ref[...] / l1 - lam * (acc2_ref[...] / l2)
    o = ot.T
    ms = jnp.mean(o * o, axis=-1, keepdims=True)
    o = o * lax.rsqrt(ms + EPS) * sg_ref[...] * (1.0 - lam_init)
    o_ref[0] = o.astype(BF16)


def _attn_kernel(lam_ref, q_ref, k_ref, kc_ref, vt_ref, vtc_ref, sg_ref, o_ref, bad_ref, acc1_ref, acc2_ref,
                 *p_bufs, tk, lam_init):
    n_lat = k_ref.shape[1] // tk
    m_ctx = kc_ref.shape[1]
    qz = _masked_queries(q_ref)
    acc_refs = (acc1_ref, acc2_ref)
    p_refs = (p_bufs[:P_BUFFERS], p_bufs[P_BUFFERS:])
    sizes = [tk] * (n_lat - 1) + [tk + m_ctx]
    groups = [tuple(range(g, min(g + PV_GROUP, n_lat))) for g in range(0, n_lat, PV_GROUP)]

    def key_rows(c):
        rows = k_ref[0, c * tk:(c + 1) * tk, :]
        return jnp.concatenate([rows, kc_ref[0]], axis=0) if c == n_lat - 1 else rows

    def value_cols(c):
        cols = vt_ref[0, 0, :, c * tk:(c + 1) * tk]
        return jnp.concatenate([cols, vtc_ref[0, 0]], axis=1) if c == n_lat - 1 else cols

    def exp_scores(c):
        kc = key_rows(c)
        sums = []
        for mp in range(2):
            p = jnp.exp2(_dot_nt(kc, qz[mp]))
            p_refs[mp][c % P_BUFFERS][0:sizes[c], :] = p.astype(BF16)
            sums.append(jnp.sum(p, axis=0, keepdims=True))
        return sums

    def pv(gi):
        members = groups[gi]
        vc = jnp.concatenate([value_cols(c) for c in members], axis=1)
        for mp in range(2):
            pp = jnp.concatenate([p_refs[mp][c % P_BUFFERS][0:sizes[c], :] for c in members], axis=0)
            upd = _dot(vc, pp)
            acc_refs[mp][...] = upd if gi == 0 else acc_refs[mp][...] + upd

    l = None
    issued = 0
    for c in range(n_lat):
        sums = exp_scores(c)
        l = sums if l is None else [l[mp] + sums[mp] for mp in range(2)]
        while issued < len(groups) and groups[issued][-1] < c:
            pv(issued)
            issued += 1
    for gi in range(issued, len(groups)):
        pv(gi)
    _attn_finish(lam_ref, sg_ref, o_ref, acc1_ref, acc2_ref, l[0], l[1], lam_init)
    in_range = jnp.logical_and(jnp.minimum(l[0], l[1]) > SAFE_SUM_LO, jnp.maximum(l[0], l[1]) < SAFE_SUM_HI)
    bad_ref[0, 0, 0] = jnp.where(in_range, 0.0, 1.0)


def _attn_redo_kernel(lam_ref, q_ref, k_ref, kc_ref, vt_ref, vtc_ref, sg_ref, o_ref, acc1_ref, acc2_ref, *,
                      tk, lam_init):
    tq = q_ref.shape[1]
    n_lat = k_ref.shape[1] // tk
    m_ctx = kc_ref.shape[1]
    qz = _masked_queries(q_ref)
    acc_refs = (acc1_ref, acc2_ref)

    def online_step(kr, vc, k0, size, carry):
        out = []
        for mp in range(2):
            m_old, l_old = carry[mp]
            s = _dot_nt(kr[0, pl.ds(k0, size), :], qz[mp])
            m_new = jnp.maximum(m_old, jnp.max(s, axis=0, keepdims=True))
            alpha = jnp.exp2(m_old - m_new)
            p = jnp.exp2(s - m_new)
            acc_refs[mp][...] = alpha * acc_refs[mp][...] + _dot(vc, p.astype(BF16))
            out.append((m_new, alpha * l_old + jnp.sum(p, axis=0, keepdims=True)))
        return tuple(out)

    def body(c, carry):
        k0 = pl.multiple_of(c * tk, tk)
        return online_step(k_ref, vt_ref[0, 0, :, pl.ds(k0, tk)], k0, tk, carry)

    acc1_ref[...] = jnp.zeros_like(acc1_ref)
    acc2_ref[...] = jnp.zeros_like(acc2_ref)
    start = (jnp.full((1, tq), NEG_BIG, F32), jnp.zeros((1, tq), F32))
    carry = lax.fori_loop(0, n_lat, body, (start, start))
    carry = online_step(kc_ref, vtc_ref[0, 0], 0, m_ctx, carry)
    _attn_finish(lam_ref, sg_ref, o_ref, acc1_ref, acc2_ref, carry[0][1], carry[1][1], lam_init)


def _pick_key_chunk(n_keys):
    best = LANES
    for cand in range(LANES, KEY_CHUNK_MAX + 1, LANES):
        if n_keys % cand == 0:
            best = cand
    return best


def _diff_attention(lam_p, q, k, kc, vt, vtc, sg, lam_init):
    batch, n, _ = q.shape
    m_ctx = kc.shape[1]
    tq = Q_BLOCK_ATT
    tk = _pick_key_chunk(n)
    hw = 2 * DH
    grid = (batch, H_DIFF, n // tq)
    in_specs = [
        _const_spec((4, DH)),
        pl.BlockSpec((1, tq, hw), lambda b, h, i: (b, i, h)),
        pl.BlockSpec((1, n, hw), lambda b, h, i: (b, 0, h)),
        pl.BlockSpec((1, m_ctx, hw), lambda b, h, i: (b, 0, h)),
        pl.BlockSpec((1, 1, DV, n), lambda b, h, i: (b, h, 0, 0)),
        pl.BlockSpec((1, 1, DV, m_ctx), lambda b, h, i: (b, h, 0, 0)),
        _const_spec((1, DV)),
    ]
    o_spec = pl.BlockSpec((1, tq, DV), lambda b, h, i: (b, i, h))
    o_shape = jax.ShapeDtypeStruct((batch, n, ATT_WIDTH), BF16)
    acc = [pltpu.VMEM((DV, tq), F32)] * 2
    args = (lam_p, q, k, kc, vt, vtc, sg)
    ao, bad = pl.pallas_call(
        functools.partial(_attn_kernel, tk=tk, lam_init=lam_init),
        grid=grid,
        in_specs=in_specs,
        out_specs=[o_spec, pl.BlockSpec((1, 1, 1, 1, tq), lambda b, h, i: (b, h, i, 0, 0))],
        out_shape=[o_shape, jax.ShapeDtypeStruct((batch, H_DIFF, n // tq, 1, tq), F32)],
        scratch_shapes=acc + [pltpu.VMEM((tk + m_ctx, tq), BF16)] * (2 * P_BUFFERS),
        compiler_params=_params(("parallel", "parallel", "arbitrary")),
        name="diff_attn",
    )(*args)

    def redo():
        return pl.pallas_call(
            functools.partial(_attn_redo_kernel, tk=tk, lam_init=lam_init),
            grid=grid,
            in_specs=in_specs,
            out_specs=o_spec,
            out_shape=o_shape,
            scratch_shapes=acc,
            compiler_params=_params(("parallel", "parallel", "arbitrary")),
            name="diff_attn_redo",
        )(*args)

    return lax.cond(jnp.max(bad) > 0.5, redo, lambda: ao)


def _fft_rows_kernel(f_ref, u_ref, y_ref):
    for j in range(y_ref.shape[1]):
        y_ref[0, j] = _dot(f_ref[...], u_ref[0, j]).astype(BF16)


def _fft_rows(f1cs, u4):
    batch, n2, n1, width = u4.shape
    per_step = min(n2, 16)
    return pl.pallas_call(
        _fft_rows_kernel,
        grid=(batch, n2 // per_step),
        in_specs=[
            _const_spec(f1cs.shape),
            pl.BlockSpec((1, per_step, n1, width), lambda b, j: (b, j, 0, 0)),
        ],
        out_specs=pl.BlockSpec((1, per_step, 2 * n1, width), lambda b, j: (b, j, 0, 0)),
        out_shape=jax.ShapeDtypeStruct((batch, n2, 2 * n1, width), BF16),
        compiler_params=_params(("parallel", "parallel")),
        name="fft_rows",
    )(f1cs, u4)


def _fft_cols_kernel(g_ref, y_ref, bdc_ref, bds_ref, o_ref):
    batch, n2, kb2, width = y_ref.shape
    kb = kb2 // 2
    rows = kb * n2
    bdc = bdc_ref[...]
    bds = bds_ref[...]
    half = MXU_DIM
    for b in range(batch):
        y = y_ref[b].reshape(2 * rows, width)
        z = _dot(g_ref[0], y)
        xr = z[:rows].astype(BF16)
        xi = z[rows:].astype(BF16)
        o = jnp.concatenate(
            [_dot(xr[:, :half], bdc) + _dot(xi[:, :half], bds),
             _dot(xr[:, half:], bdc) + _dot(xi[:, half:], bds)], axis=-1)
        o_ref[b] = o.reshape(n2, kb, width)


def _fft_cols(gbig, y4, bdc, bds):
    batch, n2, rows2, width = y4.shape
    n1 = rows2 // 2
    kb = FFT_KB
    gr = gbig.shape[1]
    return pl.pallas_call(
        _fft_cols_kernel,
        grid=(n1 // kb,),
        in_specs=[
            pl.BlockSpec((1, gr, gr), lambda j: (j, 0, 0)),
            pl.BlockSpec((batch, n2, 2 * kb, width), lambda j: (0, 0, j, 0)),
            _const_spec(bdc.shape), _const_spec(bds.shape),
        ],
        out_specs=pl.BlockSpec((batch, n2, kb, width), lambda j: (0, 0, j, 0)),
        out_shape=jax.ShapeDtypeStruct((batch, n2, n1, width), F32),
        compiler_params=_params(("parallel",)),
        name="fft_cols",
    )(gbig, y4, bdc, bds)


@functools.lru_cache(maxsize=None)
def _fft_tables(n):
    n1 = FFT_N1
    n2 = n // n1
    kb = FFT_KB
    k1 = np.arange(n1)
    ang = 2.0 * np.pi * np.outer(k1, k1) / n1
    f1cs = np.stack([np.cos(ang), -np.sin(ang)], axis=1).reshape(2 * n1, n1)
    scale = 1.0 / math.sqrt(n * F_CH)
    nblk = n1 // kb
    rows = n2 * kb
    gbig = np.zeros((nblk, 2, n2, kb, n2, kb, 2), np.float32)
    n2i = np.arange(n2)
    for j in range(kb):
        kk = (np.arange(nblk)[:, None] * kb + j) + n1 * n2i[None, :]
        a = 2.0 * np.pi * (kk[:, :, None] * n2i[None, None, :] % n) / n
        gr = np.cos(a) * scale
        gi = -np.sin(a) * scale
        gbig[:, 0, :, j, :, j, 0] = gr
        gbig[:, 0, :, j, :, j, 1] = -gi
        gbig[:, 1, :, j, :, j, 0] = gi
        gbig[:, 1, :, j, :, j, 1] = gr
    gbig = gbig.reshape(nblk, 2 * rows, 2 * rows)
    c = np.arange(F_CH)
    angc = 2.0 * np.pi * np.outer(c, c) / F_CH
    eye2 = np.eye(MXU_DIM // F_CH)
    bdc = np.kron(eye2, np.cos(angc))
    bds = np.kron(eye2, np.sin(angc))
    to_bf16 = lambda a: jnp.asarray(a, dtype=F32).astype(BF16)
    return to_bf16(f1cs), to_bf16(gbig), to_bf16(bdc), to_bf16(bds)


def _fourier_mix(u4, batch, n):
    f1cs, gbig, bdc, bds = _fft_tables(n)
    y = _fft_rows(f1cs, u4)
    fo = _fft_cols(gbig, y, bdc, bds)
    return fo.reshape(batch * n, F_WIDTH)


def _tail_kernel(xp_ref, x_ref, xn_ref, fp_ref, f_ref, fn_ref, ap_ref, a_ref, an_ref, gp_ref, g_ref, gn_ref,
                 gate0_ref, wo0_ref, g1_ref, shift_ref, scale_ref, wi_ref, cw_ref, gate1_ref, wo1_ref,
                 o_ref, *, bpb):
    i = pl.program_id(0)
    tm = x_ref.shape[0]
    halo = xp_ref.shape[0]
    ext = tm + 2 * halo
    cat = lambda p, m, nx: jnp.concatenate([p[...], m[...], nx[...]], axis=0)
    x = cat(xp_ref, x_ref, xn_ref)
    gs = cat(gp_ref, g_ref, gn_ref).astype(F32)
    yf = (cat(fp_ref, f_ref, fn_ref) * gs[:, :F_WIDTH]).astype(BF16)
    ya = (cat(ap_ref, a_ref, an_ref).astype(F32) * gs[:, F_WIDTH:]).astype(BF16)
    w = x.shape[-1]
    cs = CHANNEL_SLICE
    gate0 = gate0_ref[0]
    x1_parts = []
    ssq = None
    for part in range(w // cs):
        sl = slice(part * cs, (part + 1) * cs)
        y = _dot(yf, wo0_ref[:F_WIDTH, sl]) + _dot(ya, wo0_ref[F_WIDTH:, sl])
        x1_part = x[:, sl] + gate0[:, sl] * y
        x1_parts.append(x1_part)
        part_sum = jnp.sum(x1_part * x1_part, axis=-1, keepdims=True)
        ssq = part_sum if ssq is None else ssq + part_sum
    x1 = jnp.concatenate(x1_parts, axis=-1)
    inv_rms = lax.rsqrt(ssq * (1.0 / w) + EPS)
    h = ((x1 * inv_rms * g1_ref[...]) * (1.0 + scale_ref[0]) + shift_ref[0]).astype(BF16)
    row = lax.broadcasted_iota(jnp.int32, (ext, cs), 0)
    lo = jnp.where((i % bpb) == 0, halo, 0)
    hi = jnp.where((i % bpb) == bpb - 1, halo + tm, ext)
    in_sequence = jnp.logical_and(row >= lo, row < hi)
    cw = cw_ref[...]
    hm = h[halo:halo + tm]
    y1 = None
    for part in range(w // cs):
        c0 = part * cs
        u = _dot(h, wi_ref[:, w + c0:w + c0 + cs]) * _dot(h, wi_ref[:, 2 * w + c0:2 * w + c0 + cs])
        u = jnp.where(in_sequence, u, 0.0)
        conv = (cw[0:1, c0:c0 + cs] * pltpu.roll(u, 1, axis=0)[halo:halo + tm]
                + cw[1:2, c0:c0 + cs] * u[halo:halo + tm]
                + cw[2:3, c0:c0 + cs] * pltpu.roll(u, ext - 1, axis=0)[halo:halo + tm])
        bg = _dot(hm, wi_ref[:, c0:c0 + cs])
        gg = _dot(hm, wi_ref[:, 3 * w + c0:3 * w + c0 + cs])
        z = (bg * conv * (gg * _sigmoid(gg))).astype(BF16)
        y_part = _dot(z, wo1_ref[c0:c0 + cs, :])
        y1 = y_part if y1 is None else y1 + y_part
    o_ref[...] = x1[halo:halo + tm] + gate1_ref[0] * y1


def _tail(x2, fo, ao, gs, gate0, wo0, g1, shift1, scale1, wi, cw, gate1, wo1, n):
    rows, d = x2.shape
    tm = ROW_BLOCK
    bpb = n // tm
    hb = tm // HALO
    nhalo = rows // HALO
    main = lambda w: pl.BlockSpec((tm, w), lambda i: (i, 0))
    prev = lambda w: pl.BlockSpec((HALO, w), lambda i: (jnp.maximum(i * hb - 1, 0), 0))
    nxt = lambda w: pl.BlockSpec((HALO, w), lambda i: (jnp.minimum((i + 1) * hb, nhalo - 1), 0))
    with_halo = lambda w: [prev(w), main(w), nxt(w)]
    per_batch = pl.BlockSpec((1, 1, d), lambda i: (i // bpb, 0, 0))
    kern = functools.partial(_tail_kernel, bpb=bpb)
    return pl.pallas_call(
        kern,
        grid=(rows // tm,),
        in_specs=(with_halo(d) + with_halo(F_WIDTH) + with_halo(ATT_WIDTH) + with_halo(d)
                  + [per_batch, _const_spec(wo0.shape), _const_spec((1, d)), per_batch, per_batch,
                     _const_spec(wi.shape), _const_spec(cw.shape), per_batch, _const_spec(wo1.shape)]),
        out_specs=main(d),
        out_shape=jax.ShapeDtypeStruct((rows, d), F32),
        compiler_params=_params(("parallel",)),
        name="tail",
    )(x2, x2, x2, fo, fo, fo, ao, ao, ao, gs, gs, gs, gate0, wo0, g1, shift1, scale1, wi, cw, gate1, wo1)


@functools.lru_cache(maxsize=None)
def _rope_tables(n):
    rows = n // GRID_W
    pos_r = np.repeat(np.arange(rows, dtype=np.float64), GRID_W)
    pos_c = np.tile(np.arange(GRID_W, dtype=np.float64), rows)
    half = DH // 2
    freqs = ROPE_BASE ** (-np.arange(0, half, 2, dtype=np.float64) / half)
    ang_r = pos_r[:, None] * freqs
    ang_c = pos_c[:, None] * freqs
    cos64 = np.concatenate([np.cos(ang_r), np.cos(ang_r), np.cos(ang_c), np.cos(ang_c)], axis=-1)
    sin64 = np.concatenate([-np.sin(ang_r), np.sin(ang_r), -np.sin(ang_c), np.sin(ang_c)], axis=-1)
    cos = np.concatenate([cos64, cos64], axis=-1).astype(np.float32)
    sin = np.concatenate([sin64, sin64], axis=-1).astype(np.float32)
    return jnp.asarray(cos), jnp.asarray(sin)


@functools.lru_cache(maxsize=None)
def _group_sum_matrix():
    return jnp.asarray(np.kron(np.eye(MXU_DIM // DH), np.ones((DH, DH))), dtype=BF16)


def _lambda_init(layer_idx):
    return 0.8 - 0.6 * math.exp(-0.3 * layer_idx)


def kernel(x, c, ctx, c_ctx, norm_g, ada_w, ada_b, even_w_in, even_q_norm, even_k_norm, even_lambda_q1,
           even_lambda_k1, even_lambda_q2, even_lambda_k2, even_subln, even_w_out, odd_w_in, odd_conv_w,
           odd_w_out):
    batch, n, d = x.shape
    m_ctx = ctx.shape[1]
    assert d == D_MODEL and n % (FFT_N1 * FFT_KB) == 0 and n % ROW_BLOCK == 0 and n % GRID_W == 0
    assert batch < 8 and norm_g.shape[0] == 2

    cvec = jnp.zeros((8, d), F32).at[:batch].set(c).at[batch].set(c_ctx)
    mods = _mod_params(cvec, ada_w, ada_b)
    shift0, scale0, gate0 = [mods[0, :, j * d:(j + 1) * d].reshape(8, 1, d) for j in range(3)]
    shift1, scale1, gate1 = [mods[1, :, j * d:(j + 1) * d].reshape(8, 1, d) for j in range(3)]

    w_in = even_w_in[0].astype(BF16)
    wf = w_in[:, 0:F_WIDTH]
    wq = w_in[:, F_WIDTH:F_WIDTH + QK_WIDTH]
    wk = w_in[:, F_WIDTH + QK_WIDTH:F_WIDTH + 2 * QK_WIDTH]
    wv = w_in[:, F_WIDTH + 2 * QK_WIDTH:F_WIDTH + 2 * QK_WIDTH + ATT_WIDTH]
    wg = w_in[:, F_WIDTH + 2 * QK_WIDTH + ATT_WIDTH:]
    reps = QK_WIDTH // DH
    qg = (jnp.tile(even_q_norm[0].astype(F32), reps) * (ATTN_SCALE * LOG2E)).reshape(1, QK_WIDTH)
    kg = jnp.tile(even_k_norm[0].astype(F32), reps).reshape(1, QK_WIDTH)
    bd = _group_sum_matrix()
    cos, sin = _rope_tables(n)
    g0 = norm_g[0].reshape(1, d)
    g1 = norm_g[1].reshape(1, d)

    x2 = x.reshape(batch * n, d)
    u, q, k, vt, gs = _inproj0(x2, shift0, scale0, g0, cos, sin, qg, kg, bd, wf, wq, wk, wv, wg, batch, n)
    kc, vtc = _ctx_kv(ctx.reshape(batch * m_ctx, d), shift0[batch], scale0[batch], g0, kg, bd, wk, wv,
                      batch, m_ctx)

    lam_p = jnp.stack([even_lambda_q1[0], even_lambda_k1[0], even_lambda_q2[0], even_lambda_k2[0]]).astype(F32)
    sg = even_subln[0].astype(F32).reshape(1, DV)
    ao = _diff_attention(lam_p, q.reshape(batch, n, QK_WIDTH), k.reshape(batch, n, QK_WIDTH),
                         kc.reshape(batch, m_ctx, QK_WIDTH), vt, vtc, sg, _lambda_init(0))

    fo = _fourier_mix(u, batch, n)

    out = _tail(x2, fo, ao.reshape(batch * n, ATT_WIDTH), gs, gate0, even_w_out[0].astype(BF16), g1, shift1,
                scale1, odd_w_in[0].astype(BF16), odd_conv_w[0].astype(F32), gate1, odd_w_out[0].astype(BF16), n)
    return out.reshape(batch, n, d)
```
